```python
import math
import jax, jax.numpy as jnp
from jax import lax
import numpy as np

D_MODEL = 4096
BATCH = 1
SEQ = 16384
DEPTH = 2

HEAD_DIM = 128
Q_BLOCK = 128
EPS = 1e-6

MLA_HEADS = 8
MLA_Q_RANK = 896
MLA_KV_RANK = 512
MLA_NOPE = 128
MLA_ROPE = 64
MLA_V = 128
ROPE_THETA = 10000.0

DIL_PATTERNS = ((128, 1), (512, 4), (2048, 16))
DIL_HEADS_PER_PATTERN = 3
DIL_HEADS = DIL_HEADS_PER_PATTERN * len(DIL_PATTERNS)

NSA_HEADS = 8
NSA_KV_HEADS = 2
NSA_GROUP = NSA_HEADS // NSA_KV_HEADS
NSA_CMP_BLOCK = 32
NSA_CMP_STRIDE = 16
NSA_SLC_BLOCK = 64
NSA_TOPK = 16
NSA_WINDOW = 512

DIFF_HEADS = 7
DIFF_QK_DIM = 64
DIFF_V = 2 * DIFF_QK_DIM

MIX_WIDTH = MLA_HEADS * MLA_V + DIL_HEADS * HEAD_DIM + NSA_HEADS * HEAD_DIM + DIFF_HEADS * DIFF_V

MEM_TOKENS = 256
XATTN_HEADS = 4
FFN_DIM = 11008
CONV_WIDTH = 3

REL_BUCKETS = 32
REL_MAX_DIST = 2048
N_BIAS_HEADS = DIL_HEADS + NSA_HEADS + DIFF_HEADS
BIAS_DIL = 0
BIAS_NSA = DIL_HEADS
BIAS_DIFF = DIL_HEADS + NSA_HEADS

NSA_KV_WIDTH = NSA_KV_HEADS * HEAD_DIM
IN_SIZES = (MLA_Q_RANK, MLA_KV_RANK, MLA_ROPE,
            DIL_HEADS * HEAD_DIM, DIL_HEADS * HEAD_DIM, DIL_HEADS * HEAD_DIM,
            NSA_HEADS * HEAD_DIM, NSA_KV_WIDTH, NSA_KV_WIDTH, NSA_KV_WIDTH, NSA_KV_WIDTH,
            NSA_KV_WIDTH, NSA_KV_WIDTH, 3 * NSA_HEADS,
            DIFF_HEADS * 2 * DIFF_QK_DIM, DIFF_HEADS * 2 * DIFF_QK_DIM, DIFF_HEADS * DIFF_V)
IN_COLS = sum(IN_SIZES)

kernel_name = 'hybrid_parallel_mixer_trunk'


def rmsnorm(x, g):
    xf = x.astype(jnp.float32)
    y = xf * lax.rsqrt(jnp.mean(xf * xf, axis=-1, keepdims=True) + EPS)
    return (y * g.astype(jnp.float32)).astype(x.dtype)


def masked_softmax(logits, mask):
    logits = jnp.where(mask, logits, -jnp.inf)
    m = jnp.max(logits, axis=-1, keepdims=True)
    m = jnp.where(jnp.isfinite(m), m, 0.0)
    e = jnp.exp(logits - m)
    s = jnp.sum(e, axis=-1, keepdims=True)
    p = e / jnp.where(s > 0, s, 1.0)
    lse = m[..., 0] + jnp.log(s[..., 0])
    return p, lse


def rel_bucket(dist):
    n = jnp.maximum(dist, 0)
    exact = REL_BUCKETS // 2
    log_ratio = jnp.log(jnp.maximum(n, 1).astype(jnp.float32) / exact) / math.log(REL_MAX_DIST / exact)
    large = jnp.minimum(exact + (log_ratio * (REL_BUCKETS - exact)).astype(jnp.int32), REL_BUCKETS - 1)
    return jnp.where(n < exact, n, large)


def rope(x, pos):
    half = x.shape[-1] // 2
    inv = ROPE_THETA ** (-jnp.arange(half, dtype=jnp.float32) / half)
    ang = pos.astype(jnp.float32)[..., None] * inv
    ang = ang.reshape(pos.shape + (1,) * (x.ndim - pos.ndim - 1) + (half,))
    cos, sin = jnp.cos(ang), jnp.sin(ang)
    xf = x.astype(jnp.float32)
    x1, x2 = xf[..., :half], xf[..., half:]
    return jnp.concatenate([x1 * cos - x2 * sin, x1 * sin + x2 * cos], axis=-1).astype(x.dtype)


def banded_attention(q, k, v, span, block, dist_scale, bias):
    n, length, heads, dh = q.shape
    nb = length // block
    pad = -(-span // block) * block
    r = pad // block
    padw = ((0, 0), (pad, 0), (0, 0), (0, 0))
    kb = jnp.pad(k, padw).reshape(n, nb + r, block, heads, dh)
    vb = jnp.pad(v, padw).reshape(n, nb + r, block, heads, dh)
    k_win = jnp.concatenate([kb[:, j:j + nb] for j in range(r + 1)], axis=2)
    v_win = jnp.concatenate([vb[:, j:j + nb] for j in range(r + 1)], axis=2)
    qb = q.reshape(n, nb, block, heads, dh)
    width = pad + block
    rel = pad + jnp.arange(block)[:, None] - jnp.arange(width)[None, :]
    key_idx = jnp.arange(nb)[:, None] * block + jnp.arange(width)[None, :] - pad
    mask = ((rel >= 0) & (rel <= span))[None, :, :] & (key_idx >= 0)[:, None, :]
    rb = jnp.transpose(bias[rel_bucket(rel * dist_scale)], (2, 0, 1))
    logits = jnp.einsum('nbqhd,nbkhd->nbhqk', qb, k_win, preferred_element_type=jnp.float32) * dh ** -0.5 + rb
    p, lse = masked_softmax(logits, mask[None, :, None])
    out = jnp.einsum('nbhqk,nbkhd->nbqhd', p.astype(v.dtype), v_win).reshape(n, length, heads, dh)
    lse = jnp.swapaxes(lse, 2, 3).reshape(n, length, heads)
    return out, lse


def mla_mixer(c_q, c_kv, k_rope, pos, q_norm, kv_norm, w_uq, w_ukv, nope_norm, rope_norm):
    b, s, _ = c_q.shape
    q = (rmsnorm(c_q, q_norm) @ w_uq).reshape(b, s, MLA_HEADS, MLA_NOPE + MLA_ROPE)
    kv = (rmsnorm(c_kv, kv_norm) @ w_ukv).reshape(b, s, MLA_HEADS, MLA_NOPE + MLA_V)
    q_nope = rmsnorm(q[..., :MLA_NOPE], nope_norm[0])
    q_rope = rope(rmsnorm(q[..., MLA_NOPE:], rope_norm[0]), pos)
    k_nope = rmsnorm(kv[..., :MLA_NOPE], nope_norm[1])
    v = kv[..., MLA_NOPE:]
    k_rope = rope(rmsnorm(k_rope, rope_norm[1]), pos)
    scale = (MLA_NOPE + MLA_ROPE) ** -0.5
    nb = s // Q_BLOCK
    key_pos = jnp.arange(s)

    def to_blocks(t):
        return jnp.moveaxis(t.reshape((b, nb, Q_BLOCK) + t.shape[2:]), 1, 0)

    def attend(args):
        i, qn, qr = args
        logits = (jnp.einsum('bqhd,bkhd->bhqk', qn, k_nope, preferred_element_type=jnp.float32)
                  + jnp.einsum('bqhr,bkr->bhqk', qr, k_rope, preferred_element_type=jnp.float32)) * scale
        q_pos = i * Q_BLOCK + jnp.arange(Q_BLOCK)
        p, _ = masked_softmax(logits, key_pos[None, :] <= q_pos[:, None])
        return jnp.einsum('bhqk,bkhd->bqhd', p.astype(v.dtype), v)

    out = lax.map(attend, (jnp.arange(nb), to_blocks(q_nope), to_blocks(q_rope)))
    return jnp.moveaxis(out, 0, 1).reshape(b, s, MLA_HEADS * MLA_V)


def dilated_mixer(q, k, v, qk_norm, bias):
    b, s = q.shape[:2]
    hp = DIL_HEADS_PER_PATTERN
    q = rmsnorm(q, qk_norm[0])
    k = rmsnorm(k, qk_norm[1])
    outs, lses = [], []
    for g, (window, dil) in enumerate(DIL_PATTERNS):
        hs = slice(g * hp, (g + 1) * hp)
        n_sub = s // dil

        def fold(t):
            t = t[:, :, hs].reshape(b, n_sub, dil, hp, HEAD_DIM)
            return jnp.swapaxes(t, 1, 2).reshape(b * dil, n_sub, hp, HEAD_DIM)

        o, lse = banded_attention(fold(q), fold(k), fold(v), window // dil,
                                  math.gcd(n_sub, Q_BLOCK), dil, bias[:, hs])
        outs.append(jnp.swapaxes(o.reshape(b, dil, n_sub, hp, HEAD_DIM), 1, 2).reshape(b, s, hp, HEAD_DIM))
        lses.append(jnp.swapaxes(lse.reshape(b, dil, n_sub, hp), 1, 2).reshape(b, s, hp))
    wts = jax.nn.softmax(jnp.stack(lses, axis=2), axis=2)
    out = jnp.stack(outs, axis=2) * wts[..., None].astype(v.dtype)
    return out.reshape(b, s, DIL_HEADS * HEAD_DIM)


def compress(t, pe, w1, w2):
    b, s, hk, d = t.shape
    r = NSA_CMP_BLOCK // NSA_CMP_STRIDE
    tc = t.reshape(b, s // NSA_CMP_STRIDE, NSA_CMP_STRIDE, hk, d)
    n_cmp = s // NSA_CMP_STRIDE - r + 1
    blocks = jnp.concatenate([tc[:, j:j + n_cmp] for j in range(r)], axis=2) + pe[:, None, :]
    flat = jnp.swapaxes(blocks, 2, 3).reshape(b, n_cmp, hk, NSA_CMP_BLOCK * d)
    return jax.nn.gelu(flat @ w1) @ w2


def nsa_mixer(q, kc, vc, ks, vs, kw, vw, gate, qk_norm, cmp_pe, cmp_w1, cmp_w2, bias):
    b, s = q.shape[:2]
    hd = HEAD_DIM
    hk, grp = NSA_KV_HEADS, NSA_GROUP
    q = rmsnorm(q, qk_norm[0])
    k_cmp = rmsnorm(compress(kc, cmp_pe[0], cmp_w1[0], cmp_w2[0]), qk_norm[1])
    v_cmp = compress(vc, cmp_pe[1], cmp_w1[1], cmp_w2[1])
    ks = rmsnorm(ks, qk_norm[2])
    kw = rmsnorm(kw, qk_norm[3])
    n_cmp = k_cmp.shape[1]
    n_slc = s // NSA_SLC_BLOCK
    k_sel = min(NSA_TOPK, n_slc)
    scale = hd ** -0.5
    cmp_start = np.arange(n_cmp) * NSA_CMP_STRIDE
    cmp_end = cmp_start + NSA_CMP_BLOCK - 1
    slc_start = np.arange(n_slc) * NSA_SLC_BLOCK
    overlap = jnp.asarray(((cmp_start[:, None] < slc_start[None, :] + NSA_SLC_BLOCK)
                           & (cmp_end[:, None] >= slc_start[None, :])).astype(np.float32))
    cmp_end_j = jnp.asarray(cmp_end)
    ks_blk = jnp.moveaxis(ks.reshape(b, n_slc, NSA_SLC_BLOCK, hk, hd), 3, 1)
    vs_blk = jnp.moveaxis(vs.reshape(b, n_slc, NSA_SLC_BLOCK, hk, hd), 3, 1)
    bias_g = bias.reshape(REL_BUCKETS, hk, grp)
    b_idx = jnp.arange(b)[:, None, None, None]
    h_idx = jnp.arange(hk)[None, None, :, None]
    nb = s // Q_BLOCK
    qg = jnp.moveaxis(q.reshape(b, nb, Q_BLOCK, hk, grp, hd), 1, 0)

    def attend(args):
        i, q_b = args
        q_pos = i * Q_BLOCK + jnp.arange(Q_BLOCK)
        dist_c = q_pos[:, None] - cmp_end_j[None, :]
        bias_c = jnp.transpose(bias_g[rel_bucket(dist_c)], (2, 3, 0, 1))
        logits_c = jnp.einsum('bqhgd,bnhd->bhgqn', q_b, k_cmp, preferred_element_type=jnp.float32) * scale + bias_c
        p_c, _ = masked_softmax(logits_c, dist_c >= 0)
        o_c = jnp.einsum('bhgqn,bnhd->bqhgd', p_c.astype(v_cmp.dtype), v_cmp)
        imp = jnp.einsum('bhgqn,ns->bqhs', p_c, overlap)
        q_blk = q_pos // NSA_SLC_BLOCK
        blk = jnp.arange(n_slc)
        causal = blk[None, :] <= q_blk[:, None]
        forced = (blk[None, :] == 0) | (blk[None, :] == q_blk[:, None]) | (blk[None, :] == q_blk[:, None] - 1)
        imp = jnp.where(forced[None, :, None, :], jnp.inf, jnp.where(causal[None, :, None, :], imp, -jnp.inf))
        _, idx = lax.top_k(imp, k_sel)
        k_g = ks_blk[b_idx, h_idx, idx]
        v_g = vs_blk[b_idx, h_idx, idx]
        tok = idx[..., None] * NSA_SLC_BLOCK + jnp.arange(NSA_SLC_BLOCK)
        dist_s = q_pos[None, :, None, None, None] - tok
        bias_s = jnp.moveaxis(bias_g[rel_bucket(dist_s), h_idx[..., None]], -1, 3)
        logits_s = jnp.einsum('bqhgd,bqhkld->bqhgkl', q_b, k_g, preferred_element_type=jnp.float32) * scale + bias_s
        n_keys = k_sel * NSA_SLC_BLOCK
        p_s, _ = masked_softmax(logits_s.reshape(b, Q_BLOCK, hk, grp, n_keys),
                                (dist_s >= 0).reshape(b, Q_BLOCK, hk, 1, n_keys))
        p_s = p_s.reshape(b, Q_BLOCK, hk, grp, k_sel, NSA_SLC_BLOCK)
        o_s = jnp.einsum('bqhgkl,bqhkld->bqhgd', p_s.astype(v_g.dtype), v_g)
        return o_c, o_s

    o_c, o_s = lax.map(attend, (jnp.arange(nb), qg))
    o_c = jnp.moveaxis(o_c, 0, 1).reshape(b, s, NSA_HEADS, hd)
    o_s = jnp.moveaxis(o_s, 0, 1).reshape(b, s, NSA_HEADS, hd)
    o_w, _ = banded_attention(q, jnp.repeat(kw, grp, axis=2), jnp.repeat(vw, grp, axis=2),
                              NSA_WINDOW - 1, Q_BLOCK, 1, bias)
    g = jax.nn.sigmoid(gate.reshape(b, s, NSA_HEADS, 3).astype(jnp.float32)).astype(q.dtype)
    out = g[..., 0:1] * o_c + g[..., 1:2] * o_s + g[..., 2:3] * o_w
    return out.reshape(b, s, NSA_HEADS * hd)


def diff_mixer(q, k, v, qk_norm, lam_vec, out_norm, lam_init, bias):
    b, s = q.shape[:2]
    q = rmsnorm(q.reshape(b, s, DIFF_HEADS, 2, DIFF_QK_DIM), qk_norm[0])
    k = rmsnorm(k.reshape(b, s, DIFF_HEADS, 2, DIFF_QK_DIM), qk_norm[1])
    v = v.reshape(b, s, DIFF_HEADS, DIFF_V)
    lv = lam_vec.astype(jnp.float32)
    lam = jnp.exp(jnp.dot(lv[0], lv[1])) - jnp.exp(jnp.dot(lv[2], lv[3])) + lam_init
    scale = DIFF_QK_DIM ** -0.5
    nb = s // Q_BLOCK
    key_pos = jnp.arange(s)
    qb = jnp.moveaxis(q.reshape(b, nb, Q_BLOCK, DIFF_HEADS, 2, DIFF_QK_DIM), 1, 0)

    def attend(args):
        i, q_b = args
        q_pos = i * Q_BLOCK + jnp.arange(Q_BLOCK)
        dist = q_pos[:, None] - key_pos[None, :]
        rb = jnp.moveaxis(bias[rel_bucket(dist)], -1, 0)
        logits = jnp.einsum('bqhcd,bkhcd->bhcqk', q_b, k, preferred_element_type=jnp.float32) * scale + rb[:, None]
        p, _ = masked_softmax(logits, dist >= 0)
        attn = p[:, :, 0] - lam * p[:, :, 1]
        return jnp.einsum('bhqk,bkhd->bqhd', attn.astype(v.dtype), v)

    o = jnp.moveaxis(lax.map(attend, (jnp.arange(nb), qb)), 0, 1).reshape(b, s, DIFF_HEADS, DIFF_V)
    o = rmsnorm(o, out_norm) * (1.0 - lam_init)
    return o.reshape(b, s, DIFF_HEADS * DIFF_V)


def memory_xattn(h, m, wq, wkv, qk_norm, wo):
    b, s, _ = h.shape
    q = rmsnorm((h @ wq).reshape(b, s, XATTN_HEADS, HEAD_DIM), qk_norm[0])
    kv = (m @ wkv).reshape(b, m.shape[1], 2, XATTN_HEADS, HEAD_DIM)
    k = rmsnorm(kv[:, :, 0], qk_norm[1])
    v = kv[:, :, 1]
    logits = jnp.einsum('bqhd,bmhd->bhqm', q, k, preferred_element_type=jnp.float32) * HEAD_DIM ** -0.5
    p = jax.nn.softmax(logits, axis=-1)
    o = jnp.einsum('bhqm,bmhd->bqhd', p.astype(v.dtype), v).reshape(b, s, XATTN_HEADS * HEAD_DIM)
    return o @ wo


def conv_ffn(h, w_gate, w_up, conv_w, conv_b, w_down):
    g = lax.conv_general_dilated(h @ w_gate, conv_w[:, None, :], window_strides=(1,),
                                 padding=((CONV_WIDTH - 1, 0),), dimension_numbers=('NWC', 'WIO', 'NWC'),
                                 feature_group_count=FFN_DIM) + conv_b
    return (jax.nn.silu(g) * (h @ w_up)) @ w_down


def setup_inputs(seed: int = 0) -> dict:
    key = jax.random.key(seed)
    keys = iter(jax.random.split(key, 40))

    def normal(shape, scale):
        return jax.random.normal(next(keys), shape, jnp.float32) * scale

    def gain(shape):
        return 1.0 + normal(shape, 0.02)

    L = DEPTH
    return {
        'x': normal((BATCH, SEQ, D_MODEL), 1.0),
        'mem': normal((BATCH, MEM_TOKENS, D_MODEL), 1.0),
        'positions': jnp.broadcast_to(jnp.arange(SEQ, dtype=jnp.int32), (BATCH, SEQ)),
        'rel_bias': normal((REL_BUCKETS, N_BIAS_HEADS), 0.2),
        'norm_mix': gain((L, D_MODEL)),
        'w_in': normal((L, D_MODEL, IN_COLS), D_MODEL ** -0.5),
        'mla_q_norm': gain((L, MLA_Q_RANK)),
        'mla_kv_norm': gain((L, MLA_KV_RANK)),
        'mla_w_uq': normal((L, MLA_Q_RANK, MLA_HEADS * (MLA_NOPE + MLA_ROPE)), MLA_Q_RANK ** -0.5),
        'mla_w_ukv': normal((L, MLA_KV_RANK, MLA_HEADS * (MLA_NOPE + MLA_V)), MLA_KV_RANK ** -0.5),
        'mla_nope_norm': gain((L, 2, MLA_NOPE)),
        'mla_rope_norm': gain((L, 2, MLA_ROPE)),
        'dil_qk_norm': gain((L, 2, HEAD_DIM)),
        'nsa_qk_norm': gain((L, 4, HEAD_DIM)),
        'nsa_cmp_pe': normal((L, 2, NSA_CMP_BLOCK, HEAD_DIM), 0.02),
        'nsa_cmp_w1': normal((L, 2, NSA_CMP_BLOCK * HEAD_DIM, HEAD_DIM), (NSA_CMP_BLOCK * HEAD_DIM) ** -0.5),
        'nsa_cmp_w2': normal((L, 2, HEAD_DIM, HEAD_DIM), HEAD_DIM ** -0.5),
        'diff_qk_norm': gain((L, 2, DIFF_QK_DIM)),
        'diff_lambda': normal((L, 4, DIFF_QK_DIM), 0.1),
        'diff_out_norm': gain((L, DIFF_V)),
        'w_out': normal((L, MIX_WIDTH, D_MODEL), MIX_WIDTH ** -0.5),
        'norm_xattn': gain((L, D_MODEL)),
        'norm_mem': gain((L, D_MODEL)),
        'xattn_wq': normal((L, D_MODEL, XATTN_HEADS * HEAD_DIM), D_MODEL ** -0.5),
        'xattn_wkv': normal((L, D_MODEL, 2 * XATTN_HEADS * HEAD_DIM), D_MODEL ** -0.5),
        'xattn_qk_norm': gain((L, 2, HEAD_DIM)),
        'xattn_wo': normal((L, XATTN_HEADS * HEAD_DIM, D_MODEL), (XATTN_HEADS * HEAD_DIM) ** -0.5),
        'norm_ffn': gain((L, D_MODEL)),
        'ffn_w_gate': normal((L, D_MODEL, FFN_DIM), D_MODEL ** -0.5),
        'ffn_w_up': normal((L, D_MODEL, FFN_DIM), D_MODEL ** -0.5),
        'ffn_conv_w': normal((L, CONV_WIDTH, FFN_DIM), CONV_WIDTH ** -0.5),
        'ffn_conv_b': normal((L, FFN_DIM), 0.01),
        'ffn_w_down': normal((L, FFN_DIM, D_MODEL), FFN_DIM ** -0.5),
    }


def reference(x, mem, positions, rel_bias, norm_mix, w_in, mla_q_norm, mla_kv_norm, mla_w_uq, mla_w_ukv,
              mla_nope_norm, mla_rope_norm, dil_qk_norm, nsa_qk_norm, nsa_cmp_pe, nsa_cmp_w1, nsa_cmp_w2,
              diff_qk_norm, diff_lambda, diff_out_norm, w_out, norm_xattn, norm_mem, xattn_wq, xattn_wkv,
              xattn_qk_norm, xattn_wo, norm_ffn, ffn_w_gate, ffn_w_up, ffn_conv_w, ffn_conv_b, ffn_w_down):
    b, s, _ = x.shape
    split_at = np.cumsum(IN_SIZES)[:-1].tolist()
    kv_shape = (b, s, NSA_KV_HEADS, HEAD_DIM)
    dil_shape = (b, s, DIL_HEADS, HEAD_DIM)
    for l in range(DEPTH):
        h = rmsnorm(x, norm_mix[l])
        (c_q, c_kv, k_rope, dq, dk, dv, nq, nkc, nvc, nks, nvs, nkw, nvw, ngate,
         fq, fk, fv) = jnp.split(h @ w_in[l], split_at, axis=-1)
        o_mla = mla_mixer(c_q, c_kv, k_rope, positions, mla_q_norm[l], mla_kv_norm[l], mla_w_uq[l],
                          mla_w_ukv[l], mla_nope_norm[l], mla_rope_norm[l])
        o_dil = dilated_mixer(dq.reshape(dil_shape), dk.reshape(dil_shape), dv.reshape(dil_shape),
                              dil_qk_norm[l], rel_bias[:, BIAS_DIL:BIAS_DIL + DIL_HEADS])
        o_nsa = nsa_mixer(nq.reshape(b, s, NSA_HEADS, HEAD_DIM), nkc.reshape(kv_shape), nvc.reshape(kv_shape),
                          nks.reshape(kv_shape), nvs.reshape(kv_shape), nkw.reshape(kv_shape),
                          nvw.reshape(kv_shape), ngate, nsa_qk_norm[l], nsa_cmp_pe[l], nsa_cmp_w1[l],
                          nsa_cmp_w2[l], rel_bias[:, BIAS_NSA:BIAS_NSA + NSA_HEADS])
        lam_init = 0.8 - 0.6 * math.exp(-0.3 * l)
        o_diff = diff_mixer(fq, fk, fv, diff_qk_norm[l], diff_lambda[l], diff_out_norm[l], lam_init,
                            rel_bias[:, BIAS_DIFF:BIAS_DIFF + DIFF_HEADS])
        x = x + jnp.concatenate([o_mla, o_dil, o_nsa, o_diff], axis=-1) @ w_out[l]
        x = x + memory_xattn(rmsnorm(x, norm_xattn[l]), rmsnorm(mem, norm_mem[l]), xattn_wq[l], xattn_wkv[l],
                             xattn_qk_norm[l], xattn_wo[l])
        x = x + conv_ffn(rmsnorm(x, norm_ffn[l]), ffn_w_gate[l], ffn_w_up[l], ffn_conv_w[l], ffn_conv_b[l],
                         ffn_w_down[l])
    return x
```

```python
import functools
import math

import numpy as np
import jax
import jax.numpy as jnp
from jax import lax
from jax.experimental import pallas as pl
from jax.experimental.pallas import tpu as pltpu

F32 = jnp.float32
BF16 = jnp.bfloat16

D_MODEL = 4096
DEPTH = 2
HEAD_DIM = 128
EPS = 1e-6
MLA_HEADS = 8
MLA_Q_RANK = 896
MLA_KV_RANK = 512
MLA_NOPE = 128
MLA_ROPE = 64
MLA_V = 128
ROPE_THETA = 10000.0
DIL_PATTERNS = ((128, 1), (512, 4), (2048, 16))
DIL_HP = 3
DIL_HEADS = DIL_HP * len(DIL_PATTERNS)
NSA_HEADS = 8
NSA_KV_HEADS = 2
NSA_GROUP = NSA_HEADS // NSA_KV_HEADS
NSA_CMP_BLOCK = 32
NSA_CMP_STRIDE = 16
NSA_SLC_BLOCK = 64
NSA_TOPK = 16
NSA_WINDOW = 512
DIFF_HEADS = 7
DIFF_QK = 64
DIFF_V = 128
XATTN_HEADS = 4
FFN_DIM = 11008
REL_BUCKETS = 32
REL_MAX_DIST = 2048
BIAS_DIL = 0
BIAS_NSA = DIL_HEADS
BIAS_DIFF = DIL_HEADS + NSA_HEADS

LANE = 128
VMEM_LIMIT = 56 * 1024 * 1024
NEG = -1e30
SEL_NEG = -32768.0
SEL_HALF_BLOCKS = 128

C_CQ = 0
C_KR = 896
C_CKV = 1024
C_DQ = 1536
C_DK = 2688
C_DV = 3840
C_NQ = 4992
C_NKC = 6016
C_NVC = 6272
C_NKS = 6528
C_NVS = 6784
C_NKW = 7040
C_NVW = 7296
C_NG = 7552
C_FQ = 7680
C_FK = 8576
C_FV = 9472
NP_COLS = 10752


def _cp(sem):
    return pltpu.CompilerParams(dimension_semantics=sem, vmem_limit_bytes=VMEM_LIMIT)


def _rmsnorm_rows_kernel(x_ref, g_ref, o_ref):
    x = x_ref[...].astype(F32)
    ms = jnp.mean(x * x, axis=-1, keepdims=True)
    o_ref[...] = (x * lax.rsqrt(ms + EPS) * g_ref[...]).astype(o_ref.dtype)


def rmsnorm_rows(x, g, tm=256):
    m, d = x.shape
    tm = min(tm, m)
    return pl.pallas_call(
        _rmsnorm_rows_kernel,
        grid=(m // tm,),
        in_specs=[pl.BlockSpec((tm, d), lambda i: (i, 0)),
                  pl.BlockSpec((1, d), lambda i: (0, 0))],
        out_specs=pl.BlockSpec((tm, d), lambda i: (i, 0)),
        out_shape=jax.ShapeDtypeStruct((m, d), BF16),
        compiler_params=_cp(("parallel",)),
        name="rmsnorm_rows",
    )(x, g.reshape(1, d).astype(F32))


def _matmul_kernel(*refs, has_res):
    if has_res:
        a_ref, w_ref, r_ref, o_ref = refs
    else:
        a_ref, w_ref, o_ref = refs
    acc = jnp.dot(a_ref[...], w_ref[...], preferred_element_type=F32)
    if has_res:
        acc = acc + r_ref[...]
    o_ref[...] = acc.astype(o_ref.dtype)


def matmul(a, w, *, tm, tn, out_dtype, residual=None, name="matmul"):
    m, k = a.shape
    n = w.shape[1]
    tm = min(tm, m)
    tn = min(tn, n)
    assert m % tm == 0 and n % tn == 0
    in_specs = [pl.BlockSpec((tm, k), lambda j, i: (i, 0)),
                pl.BlockSpec((k, tn), lambda j, i: (0, j))]
    args = [a, w]
    if residual is not None:
        in_specs.append(pl.BlockSpec((tm, tn), lambda j, i: (i, j)))
        args.append(residual)
    return pl.pallas_call(
        functools.partial(_matmul_kernel, has_res=residual is not None),
        grid=(n // tn, m // tm),
        in_specs=in_specs,
        out_specs=pl.BlockSpec((tm, tn), lambda j, i: (i, j)),
        out_shape=jax.ShapeDtypeStruct((m, n), out_dtype),
        compiler_params=_cp(("parallel", "parallel")),
        name=name,
    )(*args)


def _ffn_gate_up_kernel(h_ref, wg_ref, wu_ref, cw_ref, cb_ref, o_ref, carry_ref):
    i = pl.program_id(1)

    @pl.when(i == 0)
    def _():
        carry_ref[...] = jnp.zeros_like(carry_ref)

    h = h_ref[...]
    u = jnp.dot(h, wg_ref[...], preferred_element_type=F32)
    up = jnp.dot(h, wu_ref[...], preferred_element_type=F32)
    tm = u.shape[0]
    prev = carry_ref[...]
    row = lax.broadcasted_iota(jnp.int32, u.shape, 0)
    u1 = jnp.where(row == 0, prev[7:8, :], pltpu.roll(u, 1, 0))
    u2 = pltpu.roll(u, 2, 0)
    u2 = jnp.where(row == 0, prev[6:7, :], jnp.where(row == 1, prev[7:8, :], u2))
    cw = cw_ref[...]
    g = cw[0:1, :] * u2 + cw[1:2, :] * u1 + cw[2:3, :] * u + cb_ref[...]
    o_ref[...] = (g * jax.nn.sigmoid(g) * up).astype(o_ref.dtype)
    carry_ref[...] = u[tm - 8:, :]


def ffn_gate_up(h, wg, wu, conv_w, conv_b, tm=1024, tf=256):
    m, d = h.shape
    f = wg.shape[1]
    tm = min(tm, m)
    assert f % tf == 0 and m % tm == 0
    return pl.pallas_call(
        _ffn_gate_up_kernel,
        grid=(f // tf, m // tm),
        in_specs=[pl.BlockSpec((tm, d), lambda j, i: (i, 0)),
                  pl.BlockSpec((d, tf), lambda j, i: (0, j)),
                  pl.BlockSpec((d, tf), lambda j, i: (0, j)),
                  pl.BlockSpec((3, tf), lambda j, i: (0, j)),
                  pl.BlockSpec((1, tf), lambda j, i: (0, j))],
        out_specs=pl.BlockSpec((tm, tf), lambda j, i: (i, j)),
        out_shape=jax.ShapeDtypeStruct((m, f), BF16),
        scratch_shapes=[pltpu.VMEM((8, tf), F32)],
        compiler_params=_cp(("arbitrary", "arbitrary")),
        name="ffn_gate_up",
    )(h, wg, wu, conv_w.astype(F32), conv_b.reshape(1, f).astype(F32))


def _group_norm(x, d):
    sq = x * x
    tot = jnp.sum(sq, axis=-1, keepdims=True)
    if d == LANE:
        ms = tot / d
    else:
        lane = lax.broadcasted_iota(jnp.int32, x.shape, 1)
        lo = jnp.sum(jnp.where(lane < d, sq, 0.0), axis=-1, keepdims=True)
        ms = jnp.where(lane < d, lo, tot - lo) / d
    return x * lax.rsqrt(ms + EPS)


def _head_norm_kernel(x_ref, g_ref, o_ref, *, d, scale):
    x = x_ref[...].astype(F32)
    o_ref[...] = (_group_norm(x, d) * (g_ref[...] * scale)).astype(o_ref.dtype)


def head_rmsnorm(src, col0, ncb, gain, d, scale, tm=2048):
    s = src.shape[0]
    tm = min(tm, s)
    cb0 = col0 // LANE
    g = jnp.tile(gain.astype(F32), LANE // d).reshape(1, LANE)
    return pl.pallas_call(
        functools.partial(_head_norm_kernel, d=d, scale=scale),
        grid=(s // tm, ncb),
        in_specs=[pl.BlockSpec((tm, LANE), lambda i, c: (i, cb0 + c)),
                  pl.BlockSpec((1, LANE), lambda i, c: (0, 0))],
        out_specs=pl.BlockSpec((tm, LANE), lambda i, c: (i, c)),
        out_shape=jax.ShapeDtypeStruct((s, ncb * LANE), BF16),
        compiler_params=_cp(("parallel", "parallel")),
        name="head_rmsnorm",
    )(src, g)


def _nsa_ksel_kernel(x_ref, g_ref, o_ref):
    tm = x_ref.shape[0]
    x = x_ref[...].astype(F32)
    o_ref[:, :LANE] = (_group_norm(x, LANE) * g_ref[...]).astype(o_ref.dtype)
    row = pl.program_id(0) * tm + lax.broadcasted_iota(jnp.int32, (tm, LANE), 0)
    col = lax.broadcasted_iota(jnp.int32, (tm, LANE), 1)
    hot = ((row // NSA_SLC_BLOCK) % SEL_HALF_BLOCKS) == col
    o_ref[:, LANE:] = jnp.where(hot, 1.0, 0.0).astype(o_ref.dtype)


def nsa_sel_keys(proj, gain, tm=2048):
    s = proj.shape[0]
    tm = min(tm, s)
    cb0 = C_NKS // LANE
    return pl.pallas_call(
        _nsa_ksel_kernel,
        grid=(s // tm, NSA_KV_HEADS),
        in_specs=[pl.BlockSpec((tm, LANE), lambda i, c: (i, cb0 + c)),
                  pl.BlockSpec((1, LANE), lambda i, c: (0, 0))],
        out_specs=pl.BlockSpec((tm, 2 * LANE), lambda i, c: (i, c)),
        out_shape=jax.ShapeDtypeStruct((s, NSA_KV_HEADS * 2 * LANE), BF16),
        compiler_params=_cp(("parallel", "parallel")),
        name="nsa_sel_keys",
    )(proj, gain.reshape(1, LANE).astype(F32))


def _rope_tables(pos_ref, inv_ref, sign_ref):
    ang = pos_ref[...].astype(F32) * inv_ref[...]
    return jnp.cos(ang), jnp.sin(ang) * sign_ref[...]


def _rope_apply(x, cosv, sin_signed):
    lane = lax.broadcasted_iota(jnp.int32, x.shape, 1)
    half = MLA_ROPE // 2
    swapped = jnp.where(lane < half, pltpu.roll(x, LANE - half, 1), pltpu.roll(x, half, 1))
    return x * cosv + swapped * sin_signed


def _latent_norm(c_ref, g_ref):
    c = c_ref[...].astype(F32)
    ms = jnp.mean(c * c, axis=-1, keepdims=True)
    return (c * lax.rsqrt(ms + EPS) * g_ref[...]).astype(BF16)


def _mla_q_kernel(c_ref, pos_ref, gq_ref, w_ref, gn_ref, gr_ref, inv_ref, sign_ref, o_ref, *, scale):
    u = jnp.dot(_latent_norm(c_ref, gq_ref), w_ref[...], preferred_element_type=F32)
    cosv, sinv = _rope_tables(pos_ref, inv_ref, sign_ref)
    for h in range(MLA_HEADS):
        b = 2 * LANE * h
        nope = _group_norm(u[:, b:b + LANE], LANE) * (gn_ref[...] * scale)
        rp = u[:, b + LANE:b + 2 * LANE]
        rp = rp * lax.rsqrt(jnp.sum(rp * rp, axis=-1, keepdims=True) / MLA_ROPE + EPS) * gr_ref[...]
        rp = _rope_apply(rp, cosv, sinv) * scale
        o_ref[:, b:b + LANE] = nope.astype(o_ref.dtype)
        o_ref[:, b + LANE:b + 2 * LANE] = rp.astype(o_ref.dtype)


def _mla_kv_kernel(c_ref, kr_ref, pos_ref, gkv_ref, w_ref, gn_ref, gr_ref, inv_ref, sign_ref, k_ref, v_ref):
    kv = jnp.dot(_latent_norm(c_ref, gkv_ref), w_ref[...], preferred_element_type=F32)
    cosv, sinv = _rope_tables(pos_ref, inv_ref, sign_ref)
    kr = kr_ref[...].astype(F32)
    kr = kr * lax.rsqrt(jnp.sum(kr * kr, axis=-1, keepdims=True) / MLA_ROPE + EPS) * gr_ref[...]
    kr = _rope_apply(kr, cosv, sinv).astype(k_ref.dtype)
    nv = MLA_HEADS * LANE
    for h in range(MLA_HEADS):
        kn = _group_norm(kv[:, h * LANE:(h + 1) * LANE], LANE) * gn_ref[...]
        k_ref[:, 2 * LANE * h:2 * LANE * h + LANE] = kn.astype(k_ref.dtype)
        k_ref[:, 2 * LANE * h + LANE:2 * LANE * (h + 1)] = kr
    v_ref[...] = kv[:, nv:].astype(v_ref.dtype)


def _rope_consts():
    half = MLA_ROPE // 2
    inv = ROPE_THETA ** (-jnp.arange(half, dtype=F32) / half)
    inv = jnp.concatenate([inv, inv, jnp.zeros((LANE - MLA_ROPE,), F32)]).reshape(1, LANE)
    sign = np.zeros((1, LANE), np.float32)
    sign[0, :half] = -1.0
    sign[0, half:MLA_ROPE] = 1.0
    return inv, jnp.asarray(sign)


def _pad_lanes(v, n=LANE):
    return jnp.concatenate([v.astype(F32), jnp.zeros((n - v.shape[0],), F32)]).reshape(1, n)


def mla_q_proj(proj, pos, q_norm, w_uq, nope_gain, rope_gain, tm=512):
    s = proj.shape[0]
    tm = min(tm, s)
    inv, sign = _rope_consts()
    w = jnp.pad(w_uq.reshape(MLA_Q_RANK, MLA_HEADS, MLA_NOPE + MLA_ROPE),
                ((0, 0), (0, 0), (0, LANE - MLA_ROPE))).reshape(MLA_Q_RANK, MLA_HEADS * 2 * LANE).astype(BF16)
    wq = MLA_HEADS * 2 * LANE
    const = lambda i: (0, 0)
    return pl.pallas_call(
        functools.partial(_mla_q_kernel, scale=(MLA_NOPE + MLA_ROPE) ** -0.5),
        grid=(s // tm,),
        in_specs=[pl.BlockSpec((tm, MLA_Q_RANK), lambda i: (i, C_CQ // MLA_Q_RANK)),
                  pl.BlockSpec((tm, 1), lambda i: (i, 0)),
                  pl.BlockSpec((1, MLA_Q_RANK), const),
                  pl.BlockSpec((MLA_Q_RANK, wq), const),
                  pl.BlockSpec((1, LANE), const), pl.BlockSpec((1, LANE), const),
                  pl.BlockSpec((1, LANE), const), pl.BlockSpec((1, LANE), const)],
        out_specs=pl.BlockSpec((tm, wq), lambda i: (i, 0)),
        out_shape=jax.ShapeDtypeStruct((s, wq), BF16),
        compiler_params=_cp(("parallel",)),
        name="mla_q_proj",
    )(proj, pos, q_norm.reshape(1, -1).astype(F32), w, nope_gain.reshape(1, LANE).astype(F32),
      _pad_lanes(rope_gain), inv, sign)


def mla_kv_proj(proj, pos, kv_norm, w_ukv, nope_gain, rope_gain, tm=512):
    s = proj.shape[0]
    tm = min(tm, s)
    inv, sign = _rope_consts()
    w = jnp.transpose(w_ukv.reshape(MLA_KV_RANK, MLA_HEADS, 2, LANE), (0, 2, 1, 3))
    w = w.reshape(MLA_KV_RANK, 2 * MLA_HEADS * LANE).astype(BF16)
    wk = MLA_HEADS * 2 * LANE
    wv = MLA_HEADS * MLA_V
    const = lambda i: (0, 0)
    return pl.pallas_call(
        _mla_kv_kernel,
        grid=(s // tm,),
        in_specs=[pl.BlockSpec((tm, MLA_KV_RANK), lambda i: (i, C_CKV // MLA_KV_RANK)),
                  pl.BlockSpec((tm, LANE), lambda i: (i, C_KR // LANE)),
                  pl.BlockSpec((tm, 1), lambda i: (i, 0)),
                  pl.BlockSpec((1, MLA_KV_RANK), const),
                  pl.BlockSpec((MLA_KV_RANK, 2 * MLA_HEADS * LANE), const),
                  pl.BlockSpec((1, LANE), const), pl.BlockSpec((1, LANE), const),
                  pl.BlockSpec((1, LANE), const), pl.BlockSpec((1, LANE), const)],
        out_specs=[pl.BlockSpec((tm, wk), lambda i: (i, 0)),
                   pl.BlockSpec((tm, wv), lambda i: (i, 0))],
        out_shape=[jax.ShapeDtypeStruct((s, wk), BF16), jax.ShapeDtypeStruct((s, wv), BF16)],
        compiler_params=_cp(("parallel",)),
        name="mla_kv_proj",
    )(proj, proj, pos, kv_norm.reshape(1, -1).astype(F32), w, nope_gain.reshape(1, LANE).astype(F32),
      _pad_lanes(rope_gain), inv, sign)


def _rel_bucket_np(dist):
    n = np.maximum(dist, 0)
    exact = REL_BUCKETS // 2
    ratio = np.log(np.maximum(n, 1).astype(np.float32) / np.float32(exact)) / np.float32(math.log(REL_MAX_DIST / exact))
    large = np.minimum(exact + (ratio.astype(np.float32) * (REL_BUCKETS - exact)).astype(np.int32), REL_BUCKETS - 1)
    return np.where(n < exact, n, large).astype(np.int32)


def _strip_x0(tk):
    return -(-(REL_MAX_DIST - 1 + tk) // LANE) * LANE


def bias_strips(table, tq, tk, dist_scale):
    x0 = _strip_x0(tk)
    i = np.arange(tq)[:, None]
    x = np.arange(x0 + max(tq, tk))[None, :]
    bucket = _rel_bucket_np((i + x0 - x) * dist_scale)
    return jnp.transpose(table.astype(F32)[bucket], (2, 0, 1))


def _attn_kernel(*refs, mode, tq, tk, G, dq, dv, span, has_bias, x0, n_halves, want_lse, out_scale):
    refs = list(refs)
    q_ref = refs.pop(0)
    sel_ref = refs.pop(0) if mode == "nsa_sel" else None
    k_ref = refs.pop(0)
    v_ref = refs.pop(0)
    bias_ref = refs.pop(0) if has_bias else None
    lam_ref = gout_ref = None
    if mode == "diff":
        lam_ref = refs.pop(0)
        gout_ref = refs.pop(0)
    o_ref = refs.pop(0)
    lse_ref = refs.pop(0) if want_lse else None
    m_sc, l_sc, acc_sc = refs

    qi = pl.program_id(1)
    q0 = qi * tq
    n_tiles = k_ref.shape[0] // tk

    if mode == "diff":
        q = q_ref[...]
        lane = lax.broadcasted_iota(jnp.int32, q.shape, 1)
        zero = jnp.zeros_like(q)
        lhs = [jnp.concatenate([jnp.where(lane < DIFF_QK, q, zero), jnp.where(lane >= DIFF_QK, q, zero)], axis=0)]
    elif mode == "nsa_sel":
        lhs = []
        for hf in range(n_halves):
            sel = sel_ref[hf]
            lhs.append(jnp.concatenate(
                [jnp.concatenate([q_ref[:, g * dq:(g + 1) * dq], sel], axis=1) for g in range(G)], axis=0))
    else:
        lhs = [jnp.concatenate([q_ref[:, g * dq:(g + 1) * dq] for g in range(G)], axis=0) if G > 1 else q_ref[...]]

    m_sc[...] = jnp.full_like(m_sc, NEG)
    l_sc[...] = jnp.zeros_like(l_sc)
    acc_sc[...] = jnp.zeros_like(acc_sc)

    def tile(j, Q, masked):
        k0 = pl.multiple_of(j * tk, tk)
        kt = k_ref[pl.ds(k0, tk), :]
        vt = v_ref[pl.ds(k0, tk), :]
        s = lax.dot_general(Q, kt, (((1,), (1,)), ((), ())), preferred_element_type=F32)
        if has_bias or masked:
            if masked:
                rel = (q0 - k0) + lax.broadcasted_iota(jnp.int32, (tq, tk), 0) - lax.broadcasted_iota(jnp.int32, (tq, tk), 1)
                vis = rel >= 0
                if span is not None:
                    vis = jnp.logical_and(vis, rel <= span)
            if has_bias:
                start = pl.multiple_of(jnp.maximum(x0 - (q0 - k0), 0), LANE)
            parts = []
            for g in range(G):
                sg = s[g * tq:(g + 1) * tq]
                if has_bias:
                    sg = sg + bias_ref[0 if mode == "diff" else g, :, pl.ds(start, tk)]
                if masked:
                    sg = jnp.where(vis, sg, NEG)
                parts.append(sg)
            s = jnp.concatenate(parts, axis=0) if G > 1 else parts[0]
        m_prev = m_sc[...]
        m_new = jnp.maximum(m_prev, jnp.max(s, axis=-1, keepdims=True))
        alpha = jnp.exp(m_prev - m_new)
        p = jnp.exp(s - m_new)
        l_sc[...] = alpha * l_sc[...] + jnp.sum(p, axis=-1, keepdims=True)
        acc_sc[...] = alpha * acc_sc[...] + jnp.dot(p.astype(vt.dtype), vt, preferred_element_type=F32)
        m_sc[...] = m_new

    def run(lo, hi, Q, masked):
        def body(j, c):
            tile(j, Q, masked)
            return c
        lax.fori_loop(lo, hi, body, 0)

    j_hi = jnp.minimum((q0 + tq - 1) // tk, n_tiles - 1)
    b = (q0 + 1) // tk
    if span is None:
        j_lo = 0
        a = 0
    else:
        j_lo = jnp.maximum(q0 - span, 0) // tk
        a = jnp.maximum(q0 + tq - 1 - span + tk - 1, 0) // tk
        a = jnp.minimum(jnp.maximum(a, j_lo), b)
    phases = [(j_lo, a, True), (a, b, False), (b, j_hi + 1, True)]
    if span is None:
        phases = phases[1:]
    if mode == "nsa_sel":
        tph = (SEL_HALF_BLOCKS * NSA_SLC_BLOCK) // tk
        for hf in range(n_halves):
            for lo, hi, masked in phases:
                run(jnp.maximum(lo, hf * tph), jnp.minimum(hi, (hf + 1) * tph), lhs[hf], masked)
    else:
        for lo, hi, masked in phases:
            run(lo, hi, lhs[0], masked)

    l = l_sc[...]
    acc = acc_sc[...] / l
    if mode == "diff":
        o = acc[:tq] - lam_ref[...] * acc[tq:]
        o = o * lax.rsqrt(jnp.mean(o * o, axis=-1, keepdims=True) + EPS) * (gout_ref[...] * out_scale)
        o_ref[...] = o.astype(o_ref.dtype)
    else:
        for g in range(G):
            o_ref[:, g * dv:(g + 1) * dv] = acc[g * tq:(g + 1) * tq].astype(o_ref.dtype)
        if want_lse:
            lse = m_sc[...] + jnp.log(l)
            for g in range(G):
                lse_ref[:, g * LANE:(g + 1) * LANE] = jnp.broadcast_to(lse[g * tq:(g + 1) * tq], (tq, LANE))


def attention(q, k, v, *, n_streams, seq, out_cols, qcol, kcol, vcol, ocol, mode="plain", tq, tk, G=1, dq,
              dk, dv=LANE, span=None, bias=None, bcol=None, sel=None, lam=None, gout=None, out_scale=1.0,
              want_lse=False, name="attention"):
    tq = min(tq, seq)
    tk = min(tk, seq)
    assert seq % tq == 0 and seq % tk == 0 and tq % LANE == 0 and tk % LANE == 0
    R = 2 * tq if mode == "diff" else G * tq
    Gk = 2 if mode == "diff" else G
    wq = dq if mode == "diff" else G * dq
    in_specs = [pl.BlockSpec((tq, wq), lambda n, i: (i, qcol(n)))]
    args = [q]
    n_halves = 1
    if mode == "nsa_sel":
        n_halves = sel.shape[0]
        in_specs.append(pl.BlockSpec((n_halves, tq, LANE), lambda n, i: (0, i, n)))
        args.append(sel)
    in_specs += [pl.BlockSpec((seq, dk), lambda n, i: (0, kcol(n))),
                 pl.BlockSpec((seq, dv), lambda n, i: (0, vcol(n)))]
    args += [k, v]
    x0 = 0
    if bias is not None:
        x0 = _strip_x0(tk)
        gb = 1 if mode == "diff" else G
        in_specs.append(pl.BlockSpec((gb, tq, x0 + max(tq, tk)), lambda n, i: (bcol(n), 0, 0)))
        args.append(bias)
    if mode == "diff":
        in_specs += [pl.BlockSpec((1, LANE), lambda n, i: (0, 0)), pl.BlockSpec((1, LANE), lambda n, i: (0, 0))]
        args += [lam, gout]
    wo = dv if mode == "diff" else G * dv
    out_specs = [pl.BlockSpec((tq, wo), lambda n, i: (i, ocol(n)))]
    out_shape = [jax.ShapeDtypeStruct((seq, out_cols), BF16)]
    if want_lse:
        out_specs.append(pl.BlockSpec((tq, G * LANE), lambda n, i: (i, ocol(n))))
        out_shape.append(jax.ShapeDtypeStruct((seq, out_cols), F32))
    kern = functools.partial(_attn_kernel, mode=mode, tq=tq, tk=tk, G=Gk, dq=dq, dv=dv, span=span,
                             has_bias=bias is not None, x0=x0, n_halves=n_halves, want_lse=want_lse,
                             out_scale=out_scale)
    res = pl.pallas_call(
        kern,
        grid=(n_streams, seq // tq),
        in_specs=in_specs,
        out_specs=out_specs,
        out_shape=out_shape,
        scratch_shapes=[pltpu.VMEM((R, 1), F32), pltpu.VMEM((R, 1), F32), pltpu.VMEM((R, dv), F32)],
        compiler_params=_cp(("parallel", "parallel")),
        name=name,
    )(*args)
    return res if want_lse else res[0]


def _dil_combine_kernel(o0, o1, o2, l0, l1, l2, out_ref):
    os_ = (o0, o1, o2)
    ls_ = (l0, l1, l2)
    for j in range(DIL_HP):
        sl = slice(j * LANE, (j + 1) * LANE)
        lse = [r[:, sl] for r in ls_]
        m = jnp.maximum(jnp.maximum(lse[0], lse[1]), lse[2])
        e = [jnp.exp(x - m) for x in lse]
        inv = 1.0 / (e[0] + e[1] + e[2])
        for g in range(len(DIL_PATTERNS)):
            c = (g * DIL_HP + j) * LANE
            out_ref[:, c:c + LANE] = (os_[g][:, sl].astype(F32) * (e[g] * inv)).astype(out_ref.dtype)


def dil_combine(outs, lses, tm=512):
    s = outs[0].shape[0]
    tm = min(tm, s)
    w = DIL_HP * LANE
    spec = pl.BlockSpec((tm, w), lambda i: (i, 0))
    return pl.pallas_call(
        _dil_combine_kernel,
        grid=(s // tm,),
        in_specs=[spec] * 6,
        out_specs=pl.BlockSpec((tm, DIL_HEADS * LANE), lambda i: (i, 0)),
        out_shape=jax.ShapeDtypeStruct((s, DIL_HEADS * LANE), BF16),
        compiler_params=_cp(("parallel",)),
        name="dil_combine",
    )(*outs, *lses)


def _gelu_tanh(y):
    return 0.5 * y * (1.0 + jnp.tanh(0.7978845608028654 * (y + 0.044715 * y * y * y)))


def _compress_kernel(a_ref, pe_ref, w1_ref, w2_ref, g_ref, o_ref, *, norm, transpose):
    a = a_ref[...].astype(F32)
    nc = a.shape[0]
    lo = jnp.dot((a + pe_ref[0]).astype(BF16), w1_ref[0], preferred_element_type=F32)
    hi = jnp.dot((a + pe_ref[1]).astype(BF16), w1_ref[1], preferred_element_type=F32)
    y = lo + pltpu.roll(hi, nc - 1, 0)
    z = jnp.dot(_gelu_tanh(y).astype(BF16), w2_ref[...], preferred_element_type=F32)
    if norm:
        z = _group_norm(z, LANE) * g_ref[...]
    o_ref[...] = (z.T if transpose else z).astype(o_ref.dtype)


def nsa_compress(chunks, pe, w1, w2, gain, *, norm, transpose):
    kvh, nc, w = chunks.shape
    half = NSA_CMP_STRIDE * LANE
    pe2 = pe.astype(F32).reshape(2, 1, half)
    w1s = w1.reshape(2, half, LANE).astype(BF16)
    oshape = (kvh, LANE, nc) if transpose else (kvh, nc, LANE)
    oblock = (None, LANE, nc) if transpose else (None, nc, LANE)
    return pl.pallas_call(
        functools.partial(_compress_kernel, norm=norm, transpose=transpose),
        grid=(kvh,),
        in_specs=[pl.BlockSpec((None, nc, w), lambda h: (h, 0, 0)),
                  pl.BlockSpec((2, 1, half), lambda h: (0, 0, 0)),
                  pl.BlockSpec((2, half, LANE), lambda h: (0, 0, 0)),
                  pl.BlockSpec((LANE, LANE), lambda h: (0, 0)),
                  pl.BlockSpec((1, LANE), lambda h: (0, 0))],
        out_specs=pl.BlockSpec(oblock, lambda h: (h, 0, 0)),
        out_shape=jax.ShapeDtypeStruct(oshape, BF16),
        compiler_params=_cp(("parallel",)),
        name="nsa_compress",
    )(chunks, pe2, w1s, w2.astype(BF16), gain.reshape(1, LANE).astype(F32))


def _cmp_topk_kernel(q_ref, kc_ref, vt_ref, ov_ref, strip_ref, o_ref, sel_ref, *, tq, y0, k_sel, n_halves):
    qi = pl.program_id(1)
    q0 = qi * tq
    G = NSA_GROUP
    nc = kc_ref.shape[0]
    qs = jnp.concatenate([q_ref[:, g * LANE:(g + 1) * LANE] for g in range(G)], axis=0)
    st = lax.dot_general(kc_ref[...], qs, (((1,), (1,)), ((), ())), preferred_element_type=F32)
    n_io = lax.broadcasted_iota(jnp.int32, (nc, tq), 0)
    i_io = lax.broadcasted_iota(jnp.int32, (nc, tq), 1)
    vis = (q0 + i_io - NSA_CMP_STRIDE * n_io - (NSA_CMP_BLOCK - 1)) >= 0
    start = pl.multiple_of(y0 - qi * (tq // NSA_CMP_STRIDE), 8)
    psum = jnp.zeros((nc, tq), F32)
    pts = []
    for g in range(G):
        s = st[:, g * tq:(g + 1) * tq] + strip_ref[g, pl.ds(start, nc), :]
        s = jnp.where(vis, s, NEG)
        m = jnp.max(s, axis=0, keepdims=True)
        e = jnp.where(vis, jnp.exp(s - m), 0.0)
        ssum = jnp.sum(e, axis=0, keepdims=True)
        p = e / jnp.where(ssum > 0, ssum, 1.0)
        psum = psum + p
        pts.append(p.astype(BF16))
    pt = jnp.concatenate(pts, axis=1)
    ot = jnp.dot(vt_ref[...], pt, preferred_element_type=F32)
    for g in range(G):
        o_ref[:, g * LANE:(g + 1) * LANE] = ot[:, g * tq:(g + 1) * tq].T.astype(o_ref.dtype)
    p_hi = psum.astype(BF16)
    p_lo = (psum - p_hi.astype(F32)).astype(BF16)
    imp = (jnp.dot(ov_ref[...], p_hi, preferred_element_type=F32)
           + jnp.dot(ov_ref[...], p_lo, preferred_element_type=F32))
    nb = imp.shape[0]
    b_io = lax.broadcasted_iota(jnp.int32, (nb, tq), 0)
    qblk = (q0 + lax.broadcasted_iota(jnp.int32, (nb, tq), 1)) // NSA_SLC_BLOCK
    forced = (b_io == 0) | (b_io == qblk) | (b_io == qblk - 1)
    val = jnp.where(forced, 3e38, jnp.where(b_io <= qblk, imp, -1.0))
    seln = jnp.full((nb, tq), SEL_NEG, F32)
    for _ in range(k_sel):
        mx = jnp.max(val, axis=0, keepdims=True)
        idx = jnp.min(jnp.where(val == mx, b_io, nb), axis=0, keepdims=True)
        pick = (b_io == idx) & (mx >= 0.0)
        seln = jnp.where(pick, 0.0, seln)
        val = jnp.where(pick, -2.0, val)
    sel = seln.T
    for hf in range(n_halves):
        sel_ref[hf] = sel[:, hf * LANE:(hf + 1) * LANE].astype(sel_ref.dtype)


def nsa_cmp_topk(nqn, k_cmp, v_cmp_t, table, s):
    tq = LANE
    nc = s // NSA_CMP_STRIDE
    n_cmp = nc - NSA_CMP_BLOCK // NSA_CMP_STRIDE + 1
    n_slc = s // NSA_SLC_BLOCK
    nbp = -(-n_slc // LANE) * LANE
    n_halves = nbp // LANE
    cs = np.arange(nc) * NSA_CMP_STRIDE
    ce = cs + NSA_CMP_BLOCK - 1
    ss = np.arange(nbp) * NSA_SLC_BLOCK
    ov = ((cs[None, :] < ss[:, None] + NSA_SLC_BLOCK) & (ce[None, :] >= ss[:, None])).astype(np.float32)
    ov[:, n_cmp:] = 0.0
    ov[n_slc:, :] = 0.0
    y0 = (tq // NSA_CMP_STRIDE) * (s // tq - 1)
    y = np.arange(y0 + nc)[:, None]
    i = np.arange(tq)[None, :]
    bucket = _rel_bucket_np(NSA_CMP_STRIDE * (y0 - y) + i - (NSA_CMP_BLOCK - 1))
    strips = jnp.transpose(table.astype(F32)[bucket], (2, 0, 1))
    k_sel = min(NSA_TOPK, n_slc)
    return pl.pallas_call(
        functools.partial(_cmp_topk_kernel, tq=tq, y0=y0, k_sel=k_sel, n_halves=n_halves),
        grid=(NSA_KV_HEADS, s // tq),
        in_specs=[pl.BlockSpec((tq, NSA_GROUP * LANE), lambda h, i: (i, h)),
                  pl.BlockSpec((None, nc, LANE), lambda h, i: (h, 0, 0)),
                  pl.BlockSpec((None, LANE, nc), lambda h, i: (h, 0, 0)),
                  pl.BlockSpec((nbp, nc), lambda h, i: (0, 0)),
                  pl.BlockSpec((NSA_GROUP, y0 + nc, tq), lambda h, i: (h, 0, 0))],
        out_specs=[pl.BlockSpec((tq, NSA_GROUP * LANE), lambda h, i: (i, h)),
                   pl.BlockSpec((n_halves, tq, LANE), lambda h, i: (0, i, h))],
        out_shape=[jax.ShapeDtypeStruct((s, NSA_HEADS * LANE), BF16),
                   jax.ShapeDtypeStruct((n_halves, s, NSA_KV_HEADS * LANE), BF16)],
        compiler_params=_cp(("parallel", "parallel")),
        name="nsa_cmp_topk",
    )(nqn, k_cmp, v_cmp_t, jnp.asarray(ov, BF16), strips)


def _nsa_combine_kernel(oc_ref, os_ref, ow_ref, g_ref, o_ref):
    gate = jax.nn.sigmoid(g_ref[...].astype(F32))
    for h in range(NSA_HEADS):
        sl = slice(h * LANE, (h + 1) * LANE)
        o = (gate[:, 3 * h:3 * h + 1] * oc_ref[:, sl].astype(F32)
             + gate[:, 3 * h + 1:3 * h + 2] * os_ref[:, sl].astype(F32)
             + gate[:, 3 * h + 2:3 * h + 3] * ow_ref[:, sl].astype(F32))
        o_ref[:, sl] = o.astype(o_ref.dtype)


def nsa_combine(o_c, o_s, o_w, proj, tm=512):
    s = o_c.shape[0]
    tm = min(tm, s)
    w = NSA_HEADS * LANE
    spec = pl.BlockSpec((tm, w), lambda i: (i, 0))
    return pl.pallas_call(
        _nsa_combine_kernel,
        grid=(s // tm,),
        in_specs=[spec, spec, spec, pl.BlockSpec((tm, LANE), lambda i: (i, C_NG // LANE))],
        out_specs=spec,
        out_shape=jax.ShapeDtypeStruct((s, w), BF16),
        compiler_params=_cp(("parallel",)),
        name="nsa_combine",
    )(o_c, o_s, o_w, proj)


def _xattn_kernel(q_ref, kv_ref, gq_ref, gk_ref, o_ref, *, scale):
    nk = XATTN_HEADS * LANE
    for h in range(XATTN_HEADS):
        sl = slice(h * LANE, (h + 1) * LANE)
        qn = (_group_norm(q_ref[:, sl].astype(F32), LANE) * (gq_ref[...] * scale)).astype(BF16)
        kn = (_group_norm(kv_ref[:, sl].astype(F32), LANE) * gk_ref[...]).astype(BF16)
        s = lax.dot_general(qn, kn, (((1,), (1,)), ((), ())), preferred_element_type=F32)
        m = jnp.max(s, axis=-1, keepdims=True)
        e = jnp.exp(s - m)
        p = e / jnp.sum(e, axis=-1, keepdims=True)
        v = kv_ref[:, nk + h * LANE:nk + (h + 1) * LANE]
        o_ref[:, sl] = jnp.dot(p.astype(v.dtype), v, preferred_element_type=F32).astype(o_ref.dtype)


def xattn_core(qx, kv, qk_norm, tq=512):
    s, w = qx.shape
    tq = min(tq, s)
    m = kv.shape[0]
    return pl.pallas_call(
        functools.partial(_xattn_kernel, scale=HEAD_DIM ** -0.5),
        grid=(s // tq,),
        in_specs=[pl.BlockSpec((tq, w), lambda i: (i, 0)),
                  pl.BlockSpec((m, 2 * w), lambda i: (0, 0)),
                  pl.BlockSpec((1, LANE), lambda i: (0, 0)),
                  pl.BlockSpec((1, LANE), lambda i: (0, 0))],
        out_specs=pl.BlockSpec((tq, w), lambda i: (i, 0)),
        out_shape=jax.ShapeDtypeStruct((s, w), BF16),
        compiler_params=_cp(("parallel",)),
        name="xattn_core",
    )(qx, kv, qk_norm[0].reshape(1, LANE).astype(F32), qk_norm[1].reshape(1, LANE).astype(F32))


def _pack_w_in(w):
    d = w.shape[0]
    z = lambda n: jnp.zeros((d, n), w.dtype)
    parts = [w[:, 0:896], w[:, 1408:1472], z(64), w[:, 896:1408], w[:, 1472:7488],
             w[:, 7488:7512], z(LANE - 3 * NSA_HEADS), w[:, 7512:10200], z(NP_COLS - 10368)]
    return jnp.concatenate(parts, axis=1).astype(BF16)


def _mla_mixer(proj, pos, l, p):
    s = proj.shape[0]
    qm = mla_q_proj(proj, pos, p["mla_q_norm"][l], p["mla_w_uq"][l], p["mla_nope_norm"][l, 0], p["mla_rope_norm"][l, 0])
    km, vm = mla_kv_proj(proj, pos, p["mla_kv_norm"][l], p["mla_w_ukv"][l], p["mla_nope_norm"][l, 1],
                         p["mla_rope_norm"][l, 1])
    ident = lambda n: n
    return attention(qm, km, vm, n_streams=MLA_HEADS, seq=s, out_cols=MLA_HEADS * MLA_V, qcol=ident, kcol=ident,
                     vcol=ident, ocol=ident, tq=512, tk=512, dq=2 * LANE, dk=2 * LANE, name="mla_attention")


def _dilated_mixer(proj, l, p, rel_bias):
    s = proj.shape[0]
    w = DIL_HEADS * LANE
    dqn = head_rmsnorm(proj, C_DQ, DIL_HEADS, p["dil_qk_norm"][l, 0], LANE, HEAD_DIM ** -0.5)
    dkn = head_rmsnorm(proj, C_DK, DIL_HEADS, p["dil_qk_norm"][l, 1], LANE, 1.0)
    npb = NP_COLS // LANE
    outs, lses = [], []
    for g, (window, dil) in enumerate(DIL_PATTERNS):
        n_sub = s // dil
        span = window // dil
        tq = min(256, n_sub)
        strips = bias_strips(rel_bias[:, BIAS_DIL + g * DIL_HP:BIAS_DIL + (g + 1) * DIL_HP], tq, tq, dil)
        qcol = lambda n, g=g: (n // DIL_HP) * DIL_HEADS + g * DIL_HP + n % DIL_HP
        vcol = lambda n, g=g: (n // DIL_HP) * npb + C_DV // LANE + g * DIL_HP + n % DIL_HP
        o, lse = attention(dqn.reshape(n_sub, dil * w), dkn.reshape(n_sub, dil * w), proj.reshape(n_sub, dil * NP_COLS),
                           n_streams=dil * DIL_HP, seq=n_sub, out_cols=dil * DIL_HP * LANE, qcol=qcol, kcol=qcol,
                           vcol=vcol, ocol=lambda n: n, tq=tq, tk=tq, dq=LANE, dk=LANE, span=span, bias=strips,
                           bcol=lambda n: n % DIL_HP, want_lse=True, name=f"dil_attention_{g}")
        outs.append(o.reshape(s, DIL_HP * LANE))
        lses.append(lse.reshape(s, DIL_HP * LANE))
    return dil_combine(outs, lses)


def _nsa_mixer(proj, l, p, rel_bias):
    s = proj.shape[0]
    gains = p["nsa_qk_norm"][l]
    table = rel_bias[:, BIAS_NSA:BIAS_NSA + NSA_HEADS]
    nqn = head_rmsnorm(proj, C_NQ, NSA_HEADS, gains[0], LANE, HEAD_DIM ** -0.5)
    kwn = head_rmsnorm(proj, C_NKW, NSA_KV_HEADS, gains[3], LANE, 1.0)
    ksel = nsa_sel_keys(proj, gains[2])
    nc = s // NSA_CMP_STRIDE

    def chunks(c0):
        t = proj[:, c0:c0 + NSA_KV_HEADS * LANE].reshape(nc, NSA_CMP_STRIDE, NSA_KV_HEADS, LANE)
        return jnp.transpose(t, (2, 0, 1, 3)).reshape(NSA_KV_HEADS, nc, NSA_CMP_STRIDE * LANE)

    k_cmp = nsa_compress(chunks(C_NKC), p["nsa_cmp_pe"][l, 0], p["nsa_cmp_w1"][l, 0], p["nsa_cmp_w2"][l, 0],
                         gains[1], norm=True, transpose=False)
    v_cmp_t = nsa_compress(chunks(C_NVC), p["nsa_cmp_pe"][l, 1], p["nsa_cmp_w1"][l, 1], p["nsa_cmp_w2"][l, 1],
                           gains[1], norm=False, transpose=True)
    o_c, sel = nsa_cmp_topk(nqn, k_cmp, v_cmp_t, table, s)
    tq, tk = LANE, min(512, s)
    strips = bias_strips(table, tq, tk, 1)
    ident = lambda n: n
    common = dict(n_streams=NSA_KV_HEADS, seq=s, out_cols=NSA_HEADS * LANE, qcol=ident, ocol=ident, tq=tq, tk=tk,
                  G=NSA_GROUP, dq=LANE, bias=strips, bcol=ident)
    o_s = attention(nqn, ksel, proj, kcol=ident, vcol=lambda n: C_NVS // LANE + n, mode="nsa_sel", dk=2 * LANE,
                    sel=sel, name="nsa_sel_attention", **common)
    o_w = attention(nqn, kwn, proj, kcol=ident, vcol=lambda n: C_NVW // LANE + n, dk=LANE, span=NSA_WINDOW - 1,
                    name="nsa_win_attention", **common)
    return nsa_combine(o_c, o_s, o_w, proj)


def _diff_mixer(proj, l, p, rel_bias):
    s = proj.shape[0]
    fqn = head_rmsnorm(proj, C_FQ, DIFF_HEADS, p["diff_qk_norm"][l, 0], DIFF_QK, DIFF_QK ** -0.5)
    fkn = head_rmsnorm(proj, C_FK, DIFF_HEADS, p["diff_qk_norm"][l, 1], DIFF_QK, 1.0)
    lam_init = 0.8 - 0.6 * math.exp(-0.3 * l)
    lv = p["diff_lambda"][l].astype(F32)
    lam = jnp.exp(jnp.sum(lv[0] * lv[1])) - jnp.exp(jnp.sum(lv[2] * lv[3])) + lam_init
    tq, tk = min(256, s), min(512, s)
    strips = bias_strips(rel_bias[:, BIAS_DIFF:BIAS_DIFF + DIFF_HEADS], tq, tk, 1)
    ident = lambda n: n
    return attention(fqn, fkn, proj, n_streams=DIFF_HEADS, seq=s, out_cols=DIFF_HEADS * DIFF_V, qcol=ident,
                     kcol=ident, vcol=lambda n: C_FV // LANE + n, ocol=ident, mode="diff", tq=tq, tk=tk, dq=LANE,
                     dk=LANE, bias=strips, bcol=ident, lam=jnp.full((1, LANE), lam, F32),
                     gout=p["diff_out_norm"][l].reshape(1, LANE).astype(F32), out_scale=1.0 - lam_init,
                     name="diff_attention")


def kernel(x, mem, positions, rel_bias, norm_mix, w_in, mla_q_norm, mla_kv_norm, mla_w_uq, mla_w_ukv, mla_nope_norm, mla_rope_norm, dil_qk_norm, nsa_qk_norm, nsa_cmp_pe, nsa_cmp_w1, nsa_cmp_w2, diff_qk_norm, diff_lambda, diff_out_norm, w_out, norm_xattn, norm_mem, xattn_wq, xattn_wkv, xattn_qk_norm, xattn_wo, norm_ffn, ffn_w_gate, ffn_w_up, ffn_conv_w, ffn_conv_b, ffn_w_down):
    p = dict(mla_q_norm=mla_q_norm, mla_kv_norm=mla_kv_norm, mla_w_uq=mla_w_uq, mla_w_ukv=mla_w_ukv,
             mla_nope_norm=mla_nope_norm, mla_rope_norm=mla_rope_norm, dil_qk_norm=dil_qk_norm,
             nsa_qk_norm=nsa_qk_norm, nsa_cmp_pe=nsa_cmp_pe, nsa_cmp_w1=nsa_cmp_w1, nsa_cmp_w2=nsa_cmp_w2,
             diff_qk_norm=diff_qk_norm, diff_lambda=diff_lambda, diff_out_norm=diff_out_norm)
    b, s, d = x.shape
    assert b == 1
    xs = x.reshape(s, d).astype(F32)
    mems = mem.reshape(mem.shape[1], d).astype(F32)
    pos = positions.reshape(s, 1).astype(jnp.int32)
    for l in range(DEPTH):
        h = rmsnorm_rows(xs, norm_mix[l])
        proj = matmul(h, _pack_w_in(w_in[l]), tm=512, tn=1536, out_dtype=BF16, name="in_proj")
        mix = jnp.concatenate([_mla_mixer(proj, pos, l, p), _dilated_mixer(proj, l, p, rel_bias),
                               _nsa_mixer(proj, l, p, rel_bias), _diff_mixer(proj, l, p, rel_bias)], axis=1)
        xs = matmul(mix, w_out[l].astype(BF16), tm=512, tn=1024, out_dtype=F32, residual=xs, name="out_proj")
        hx = rmsnorm_rows(xs, norm_xattn[l])
        qx = matmul(hx, xattn_wq[l].astype(BF16), tm=1024, tn=512, out_dtype=BF16, name="xattn_q")
        kvm = matmul(rmsnorm_rows(mems, norm_mem[l]), xattn_wkv[l].astype(BF16), tm=256, tn=1024, out_dtype=BF16,
                     name="xattn_kv")
        ox = xattn_core(qx, kvm, xattn_qk_norm[l])
        xs = matmul(ox, xattn_wo[l].astype(BF16), tm=1024, tn=1024, out_dtype=F32, residual=xs, name="xattn_o")
        hf = rmsnorm_rows(xs, norm_ffn[l])
        act = ffn_gate_up(hf, ffn_w_gate[l].astype(BF16), ffn_w_up[l].astype(BF16), ffn_conv_w[l], ffn_conv_b[l])
        xs = matmul(act, ffn_w_down[l].astype(BF16), tm=512, tn=512, out_dtype=F32, residual=xs, name="ffn_down")
    return xs.reshape(b, s, d)
```

```python
import functools
import math

import numpy as np
import jax
import jax.numpy as jnp
from jax import lax
from jax.experimental import pallas as pl
from jax.experimental.pallas import tpu as pltpu

F32 = jnp.float32
BF16 = jnp.bfloat16

D_MODEL = 4096
DEPTH = 2
HEAD_DIM = 128
EPS = 1e-6
MLA_HEADS = 8
MLA_Q_RANK = 896
MLA_KV_RANK = 512
MLA_NOPE = 128
MLA_ROPE = 64
MLA_V = 128
ROPE_THETA = 10000.0
DIL_PATTERNS = ((128, 1), (512, 4), (2048, 16))
DIL_HP = 3
DIL_HEADS = DIL_HP * len(DIL_PATTERNS)
NSA_HEADS = 8
NSA_KV_HEADS = 2
NSA_GROUP = NSA_HEADS // NSA_KV_HEADS
NSA_CMP_BLOCK = 32
NSA_CMP_STRIDE = 16
NSA_SLC_BLOCK = 64
NSA_TOPK = 16
NSA_WINDOW = 512
DIFF_HEADS = 7
DIFF_QK = 64
DIFF_V = 128
XATTN_HEADS = 4
FFN_DIM = 11008
REL_BUCKETS = 32
REL_MAX_DIST = 2048
BIAS_DIL = 0
BIAS_NSA = DIL_HEADS
BIAS_DIFF = DIL_HEADS + NSA_HEADS

LANE = 128
VMEM_LIMIT = 56 * 1024 * 1024
NEG = -1e30
SEL_NEG = -32768.0
SEL_HALF_BLOCKS = 128
ATT_ROWS = 512
ATT_TK = 1024
ATT_TKB = 2048
DIL_TQ = 256

C_CQ = 0
C_KR = 896
C_CKV = 1024
C_DQ = 1536
C_DK = 2688
C_DV = 3840
C_NQ = 4992
C_NKC = 6016
C_NVC = 6272
C_NKS = 6528
C_NVS = 6784
C_NKW = 7040
C_NVW = 7296
C_NG = 7552
C_FQ = 7680
C_FK = 8576
C_FV = 9472
NP_COLS = 10752


def _cp(sem):
    return pltpu.CompilerParams(dimension_semantics=sem, vmem_limit_bytes=VMEM_LIMIT)


def _rmsnorm_rows_kernel(x_ref, g_ref, o_ref):
    x = x_ref[...].astype(F32)
    ms = jnp.mean(x * x, axis=-1, keepdims=True)
    o_ref[...] = (x * lax.rsqrt(ms + EPS) * g_ref[...]).astype(o_ref.dtype)


def rmsnorm_rows(x, g, tm=256):
    m, d = x.shape
    tm = min(tm, m)
    return pl.pallas_call(
        _rmsnorm_rows_kernel,
        grid=(m // tm,),
        in_specs=[pl.BlockSpec((tm, d), lambda i: (i, 0)),
                  pl.BlockSpec((1, d), lambda i: (0, 0))],
        out_specs=pl.BlockSpec((tm, d), lambda i: (i, 0)),
        out_shape=jax.ShapeDtypeStruct((m, d), BF16),
        compiler_params=_cp(("parallel",)),
        name="rmsnorm_rows",
    )(x, g.reshape(1, d).astype(F32))


def _matmul_kernel(*refs, has_res):
    if has_res:
        a_ref, w_ref, r_ref, o_ref = refs
    else:
        a_ref, w_ref, o_ref = refs
    acc = jnp.dot(a_ref[...], w_ref[...], preferred_element_type=F32)
    if has_res:
        acc = acc + r_ref[...]
    o_ref[...] = acc.astype(o_ref.dtype)


def matmul(a, w, *, tm, tn, out_dtype, residual=None, name="matmul"):
    m, k = a.shape
    n = w.shape[1]
    tm = min(tm, m)
    tn = min(tn, n)
    assert m % tm == 0 and n % tn == 0
    in_specs = [pl.BlockSpec((tm, k), lambda j, i: (i, 0)),
                pl.BlockSpec((k, tn), lambda j, i: (0, j))]
    args = [a, w]
    if residual is not None:
        in_specs.append(pl.BlockSpec((tm, tn), lambda j, i: (i, j)))
        args.append(residual)
    return pl.pallas_call(
        functools.partial(_matmul_kernel, has_res=residual is not None),
        grid=(n // tn, m // tm),
        in_specs=in_specs,
        out_specs=pl.BlockSpec((tm, tn), lambda j, i: (i, j)),
        out_shape=jax.ShapeDtypeStruct((m, n), out_dtype),
        compiler_params=_cp(("parallel", "parallel")),
        name=name,
    )(*args)


def _ffn_gate_up_kernel(h_ref, wg_ref, wu_ref, cw_ref, cb_ref, o_ref, carry_ref):
    i = pl.program_id(1)

    @pl.when(i == 0)
    def _():
        carry_ref[...] = jnp.zeros_like(carry_ref)

    h = h_ref[...]
    u = jnp.dot(h, wg_ref[...], preferred_element_type=F32)
    up = jnp.dot(h, wu_ref[...], preferred_element_type=F32)
    tm = u.shape[0]
    prev = carry_ref[...]
    row = lax.broadcasted_iota(jnp.int32, u.shape, 0)
    u1 = jnp.where(row == 0, prev[7:8, :], pltpu.roll(u, 1, 0))
    u2 = pltpu.roll(u, 2, 0)
    u2 = jnp.where(row == 0, prev[6:7, :], jnp.where(row == 1, prev[7:8, :], u2))
    cw = cw_ref[...]
    g = cw[0:1, :] * u2 + cw[1:2, :] * u1 + cw[2:3, :] * u + cb_ref[...]
    o_ref[...] = (g * jax.nn.sigmoid(g) * up).astype(o_ref.dtype)
    carry_ref[...] = u[tm - 8:, :]


def ffn_gate_up(h, wg, wu, conv_w, conv_b, tm=1024, tf=256):
    m, d = h.shape
    f = wg.shape[1]
    tm = min(tm, m)
    assert f % tf == 0 and m % tm == 0
    return pl.pallas_call(
        _ffn_gate_up_kernel,
        grid=(f // tf, m // tm),
        in_specs=[pl.BlockSpec((tm, d), lambda j, i: (i, 0)),
                  pl.BlockSpec((d, tf), lambda j, i: (0, j)),
                  pl.BlockSpec((d, tf), lambda j, i: (0, j)),
                  pl.BlockSpec((3, tf), lambda j, i: (0, j)),
                  pl.BlockSpec((1, tf), lambda j, i: (0, j))],
        out_specs=pl.BlockSpec((tm, tf), lambda j, i: (i, j)),
        out_shape=jax.ShapeDtypeStruct((m, f), BF16),
        scratch_shapes=[pltpu.VMEM((8, tf), F32)],
        compiler_params=_cp(("arbitrary", "arbitrary")),
        name="ffn_gate_up",
    )(h, wg, wu, conv_w.astype(F32), conv_b.reshape(1, f).astype(F32))


def _group_norm(x, d):
    sq = x * x
    tot = jnp.sum(sq, axis=-1, keepdims=True)
    if d == LANE:
        ms = tot / d
    else:
        lane = lax.broadcasted_iota(jnp.int32, x.shape, 1)
        lo = jnp.sum(jnp.where(lane < d, sq, 0.0), axis=-1, keepdims=True)
        ms = jnp.where(lane < d, lo, tot - lo) / d
    return x * lax.rsqrt(ms + EPS)


def _head_norm_kernel(x_ref, g_ref, o_ref, *, d, scale):
    x = x_ref[...].astype(F32)
    o_ref[...] = (_group_norm(x, d) * (g_ref[...] * scale)).astype(o_ref.dtype)


def head_rmsnorm(src, col0, ncb, gain, d, scale, tm=2048):
    s = src.shape[0]
    tm = min(tm, s)
    cb0 = col0 // LANE
    g = jnp.tile(gain.astype(F32), LANE // d).reshape(1, LANE)
    return pl.pallas_call(
        functools.partial(_head_norm_kernel, d=d, scale=scale),
        grid=(s // tm, ncb),
        in_specs=[pl.BlockSpec((tm, LANE), lambda i, c: (i, cb0 + c)),
                  pl.BlockSpec((1, LANE), lambda i, c: (0, 0))],
        out_specs=pl.BlockSpec((tm, LANE), lambda i, c: (i, c)),
        out_shape=jax.ShapeDtypeStruct((s, ncb * LANE), BF16),
        compiler_params=_cp(("parallel", "parallel")),
        name="head_rmsnorm",
    )(src, g)


def _nsa_ksel_kernel(x_ref, g_ref, o_ref):
    tm = x_ref.shape[0]
    x = x_ref[...].astype(F32)
    o_ref[:, :LANE] = (_group_norm(x, LANE) * g_ref[...]).astype(o_ref.dtype)
    row = pl.program_id(0) * tm + lax.broadcasted_iota(jnp.int32, (tm, LANE), 0)
    col = lax.broadcasted_iota(jnp.int32, (tm, LANE), 1)
    hot = ((row // NSA_SLC_BLOCK) % SEL_HALF_BLOCKS) == col
    o_ref[:, LANE:] = jnp.where(hot, 1.0, 0.0).astype(o_ref.dtype)


def nsa_sel_keys(proj, gain, tm=2048):
    s = proj.shape[0]
    tm = min(tm, s)
    cb0 = C_NKS // LANE
    return pl.pallas_call(
        _nsa_ksel_kernel,
        grid=(s // tm, NSA_KV_HEADS),
        in_specs=[pl.BlockSpec((tm, LANE), lambda i, c: (i, cb0 + c)),
                  pl.BlockSpec((1, LANE), lambda i, c: (0, 0))],
        out_specs=pl.BlockSpec((tm, 2 * LANE), lambda i, c: (i, c)),
        out_shape=jax.ShapeDtypeStruct((s, NSA_KV_HEADS * 2 * LANE), BF16),
        compiler_params=_cp(("parallel", "parallel")),
        name="nsa_sel_keys",
    )(proj, gain.reshape(1, LANE).astype(F32))


def _rope_tables(pos_ref, inv_ref, sign_ref):
    ang = pos_ref[...].astype(F32) * inv_ref[...]
    return jnp.cos(ang), jnp.sin(ang) * sign_ref[...]


def _rope_apply(x, cosv, sin_signed):
    lane = lax.broadcasted_iota(jnp.int32, x.shape, 1)
    half = MLA_ROPE // 2
    swapped = jnp.where(lane < half, pltpu.roll(x, LANE - half, 1), pltpu.roll(x, half, 1))
    return x * cosv + swapped * sin_signed


def _latent_norm(c_ref, g_ref):
    c = c_ref[...].astype(F32)
    ms = jnp.mean(c * c, axis=-1, keepdims=True)
    return (c * lax.rsqrt(ms + EPS) * g_ref[...]).astype(BF16)


def _mla_q_kernel(c_ref, pos_ref, gq_ref, w_ref, gn_ref, gr_ref, inv_ref, sign_ref, o_ref, *, scale):
    u = jnp.dot(_latent_norm(c_ref, gq_ref), w_ref[...], preferred_element_type=F32)
    cosv, sinv = _rope_tables(pos_ref, inv_ref, sign_ref)
    for h in range(MLA_HEADS):
        b = 2 * LANE * h
        nope = _group_norm(u[:, b:b + LANE], LANE) * (gn_ref[...] * scale)
        rp = u[:, b + LANE:b + 2 * LANE]
        rp = rp * lax.rsqrt(jnp.sum(rp * rp, axis=-1, keepdims=True) / MLA_ROPE + EPS) * gr_ref[...]
        rp = _rope_apply(rp, cosv, sinv) * scale
        o_ref[:, b:b + LANE] = nope.astype(o_ref.dtype)
        o_ref[:, b + LANE:b + 2 * LANE] = rp.astype(o_ref.dtype)


def _mla_kv_kernel(c_ref, kr_ref, pos_ref, gkv_ref, w_ref, gn_ref, gr_ref, inv_ref, sign_ref, k_ref, v_ref):
    kv = jnp.dot(_latent_norm(c_ref, gkv_ref), w_ref[...], preferred_element_type=F32)
    cosv, sinv = _rope_tables(pos_ref, inv_ref, sign_ref)
    kr = kr_ref[...].astype(F32)
    kr = kr * lax.rsqrt(jnp.sum(kr * kr, axis=-1, keepdims=True) / MLA_ROPE + EPS) * gr_ref[...]
    kr = _rope_apply(kr, cosv, sinv).astype(k_ref.dtype)
    nv = MLA_HEADS * LANE
    for h in range(MLA_HEADS):
        kn = _group_norm(kv[:, h * LANE:(h + 1) * LANE], LANE) * gn_ref[...]
        k_ref[:, 2 * LANE * h:2 * LANE * h + LANE] = kn.astype(k_ref.dtype)
        k_ref[:, 2 * LANE * h + LANE:2 * LANE * (h + 1)] = kr
    v_ref[...] = kv[:, nv:].astype(v_ref.dtype)


def _rope_consts():
    half = MLA_ROPE // 2
    inv = ROPE_THETA ** (-jnp.arange(half, dtype=F32) / half)
    inv = jnp.concatenate([inv, inv, jnp.zeros((LANE - MLA_ROPE,), F32)]).reshape(1, LANE)
    sign = np.zeros((1, LANE), np.float32)
    sign[0, :half] = -1.0
    sign[0, half:MLA_ROPE] = 1.0
    return inv, jnp.asarray(sign)


def _pad_lanes(v, n=LANE):
    return jnp.concatenate([v.astype(F32), jnp.zeros((n - v.shape[0],), F32)]).reshape(1, n)


def mla_q_proj(proj, pos, q_norm, w_uq, nope_gain, rope_gain, tm=512):
    s = proj.shape[0]
    tm = min(tm, s)
    inv, sign = _rope_consts()
    w = jnp.pad(w_uq.reshape(MLA_Q_RANK, MLA_HEADS, MLA_NOPE + MLA_ROPE),
                ((0, 0), (0, 0), (0, LANE - MLA_ROPE))).reshape(MLA_Q_RANK, MLA_HEADS * 2 * LANE).astype(BF16)
    wq = MLA_HEADS * 2 * LANE
    const = lambda i: (0, 0)
    return pl.pallas_call(
        functools.partial(_mla_q_kernel, scale=(MLA_NOPE + MLA_ROPE) ** -0.5),
        grid=(s // tm,),
        in_specs=[pl.BlockSpec((tm, MLA_Q_RANK), lambda i: (i, C_CQ // MLA_Q_RANK)),
                  pl.BlockSpec((tm, 1), lambda i: (i, 0)),
                  pl.BlockSpec((1, MLA_Q_RANK), const),
                  pl.BlockSpec((MLA_Q_RANK, wq), const),
                  pl.BlockSpec((1, LANE), const), pl.BlockSpec((1, LANE), const),
                  pl.BlockSpec((1, LANE), const), pl.BlockSpec((1, LANE), const)],
        out_specs=pl.BlockSpec((tm, wq), lambda i: (i, 0)),
        out_shape=jax.ShapeDtypeStruct((s, wq), BF16),
        compiler_params=_cp(("parallel",)),
        name="mla_q_proj",
    )(proj, pos, q_norm.reshape(1, -1).astype(F32), w, nope_gain.reshape(1, LANE).astype(F32),
      _pad_lanes(rope_gain), inv, sign)


def mla_kv_proj(proj, pos, kv_norm, w_ukv, nope_gain, rope_gain, tm=512):
    s = proj.shape[0]
    tm = min(tm, s)
    inv, sign = _rope_consts()
    w = jnp.transpose(w_ukv.reshape(MLA_KV_RANK, MLA_HEADS, 2, LANE), (0, 2, 1, 3))
    w = w.reshape(MLA_KV_RANK, 2 * MLA_HEADS * LANE).astype(BF16)
    wk = MLA_HEADS * 2 * LANE
    wv = MLA_HEADS * MLA_V
    const = lambda i: (0, 0)
    return pl.pallas_call(
        _mla_kv_kernel,
        grid=(s // tm,),
        in_specs=[pl.BlockSpec((tm, MLA_KV_RANK), lambda i: (i, C_CKV // MLA_KV_RANK)),
                  pl.BlockSpec((tm, LANE), lambda i: (i, C_KR // LANE)),
                  pl.BlockSpec((tm, 1), lambda i: (i, 0)),
                  pl.BlockSpec((1, MLA_KV_RANK), const),
                  pl.BlockSpec((MLA_KV_RANK, 2 * MLA_HEADS * LANE), const),
                  pl.BlockSpec((1, LANE), const), pl.BlockSpec((1, LANE), const),
                  pl.BlockSpec((1, LANE), const), pl.BlockSpec((1, LANE), const)],
        out_specs=[pl.BlockSpec((tm, wk), lambda i: (i, 0)),
                   pl.BlockSpec((tm, wv), lambda i: (i, 0))],
        out_shape=[jax.ShapeDtypeStruct((s, wk), BF16), jax.ShapeDtypeStruct((s, wv), BF16)],
        compiler_params=_cp(("parallel",)),
        name="mla_kv_proj",
    )(proj, proj, pos, kv_norm.reshape(1, -1).astype(F32), w, nope_gain.reshape(1, LANE).astype(F32),
      _pad_lanes(rope_gain), inv, sign)


def _rel_bucket_np(dist):
    n = np.maximum(dist, 0)
    exact = REL_BUCKETS // 2
    ratio = np.log(np.maximum(n, 1).astype(np.float32) / np.float32(exact)) / np.float32(math.log(REL_MAX_DIST / exact))
    large = np.minimum(exact + (ratio.astype(np.float32) * (REL_BUCKETS - exact)).astype(np.int32), REL_BUCKETS - 1)
    return np.where(n < exact, n, large).astype(np.int32)


def _strip_x0(tk):
    return -(-(REL_MAX_DIST - 1 + tk) // LANE) * LANE


def bias_strips(table, tq, tk, tkb, dist_scale):
    x0 = _strip_x0(tkb)
    width = x0 + max(tq, tk)
    period = width + tq
    t = np.arange(period)
    t = np.where(t < width, t, t - period)
    u = table.astype(F32)[_rel_bucket_np((x0 - t) * dist_scale)].T
    rows = jnp.tile(u, (1, tq))[:, :tq * (period - 1)].reshape(u.shape[0], tq, period - 1)
    return rows[:, :, :width]


def _attn_kernel(*refs, mode, tq, tk, tkb, G, dq, dv, span, has_bias, x0, n_halves, want_lse, out_scale):
    refs = list(refs)
    q_ref = refs.pop(0)
    sel_ref = refs.pop(0) if mode == "nsa_sel" else None
    k_ref = refs.pop(0)
    v_ref = refs.pop(0)
    bias_ref = refs.pop(0) if has_bias else None
    lam_ref = gout_ref = None
    if mode == "diff":
        lam_ref = refs.pop(0)
        gout_ref = refs.pop(0)
    o_ref = refs.pop(0)
    lse_ref = refs.pop(0) if want_lse else None
    m_sc, l_sc, acc_sc = refs

    qi = pl.program_id(1)
    q0 = qi * tq
    n_tiles = k_ref.shape[0] // tk

    if mode == "diff":
        q = q_ref[...]
        lane = lax.broadcasted_iota(jnp.int32, q.shape, 1)
        zero = jnp.zeros_like(q)
        lhs = [jnp.concatenate([jnp.where(lane < DIFF_QK, q, zero), jnp.where(lane >= DIFF_QK, q, zero)], axis=0)]
    elif mode == "nsa_sel":
        lhs = []
        for hf in range(n_halves):
            sel = sel_ref[hf]
            lhs.append(jnp.concatenate(
                [jnp.concatenate([q_ref[:, g * dq:(g + 1) * dq], sel], axis=1) for g in range(G)], axis=0))
    else:
        lhs = [jnp.concatenate([q_ref[:, g * dq:(g + 1) * dq] for g in range(G)], axis=0) if G > 1 else q_ref[...]]

    m_sc[...] = jnp.full_like(m_sc, NEG)
    l_sc[...] = jnp.zeros_like(l_sc)
    acc_sc[...] = jnp.zeros_like(acc_sc)

    def tile(k0, w, Q, masked):
        kt = k_ref[pl.ds(k0, w), :]
        vt = v_ref[pl.ds(k0, w), :]
        s = lax.dot_general(Q, kt, (((1,), (1,)), ((), ())), preferred_element_type=F32)
        if has_bias or masked:
            if masked:
                rel = (q0 - k0) + lax.broadcasted_iota(jnp.int32, (tq, w), 0) - lax.broadcasted_iota(jnp.int32, (tq, w), 1)
                vis = rel >= 0
                if span is not None:
                    vis = jnp.logical_and(vis, rel <= span)
            if has_bias:
                start = pl.multiple_of(jnp.maximum(x0 - (q0 - k0), 0), LANE)
            parts = []
            for g in range(G):
                sg = s[g * tq:(g + 1) * tq]
                if has_bias:
                    sg = sg + bias_ref[0 if mode == "diff" else g, :, pl.ds(start, w)]
                if masked:
                    sg = jnp.where(vis, sg, NEG)
                parts.append(sg)
            s = jnp.concatenate(parts, axis=0) if G > 1 else parts[0]
        m_prev = m_sc[...]
        m_new = jnp.maximum(m_prev, jnp.max(s, axis=-1, keepdims=True))
        alpha = jnp.exp(m_prev - m_new)
        p = jnp.exp(s - pltpu.repeat(m_new, w // LANE, axis=1))
        l_sc[...] = alpha * l_sc[...] + jnp.sum(p, axis=-1, keepdims=True)
        acc_sc[...] = alpha * acc_sc[...] + jnp.dot(p.astype(vt.dtype), vt, preferred_element_type=F32)
        m_sc[...] = m_new

    def run(lo, hi, w, Q, masked):
        def body(j, c):
            tile(pl.multiple_of(j * w, w), w, Q, masked)
            return c
        lax.fori_loop(lo, hi, body, 0)

    def run_visible(lo, hi, Q):
        if tkb == tk:
            run(lo, hi, tk, Q, False)
            return
        r = tkb // tk
        up = jnp.minimum(-(-lo // r) * r, hi)
        dn = jnp.maximum((hi // r) * r, up)
        if not (isinstance(lo, int) and lo % r == 0):
            run(lo, up, tk, Q, False)
        run(up // r, dn // r, tkb, Q, False)
        run(dn, hi, tk, Q, False)

    j_hi = jnp.minimum((q0 + tq - 1) // tk, n_tiles - 1)
    b = (q0 + 1) // tk
    if span is None:
        j_lo = 0
        a = 0
    else:
        j_lo = jnp.maximum(q0 - span, 0) // tk
        a = jnp.maximum(q0 + tq - 1 - span + tk - 1, 0) // tk
        a = jnp.minimum(jnp.maximum(a, j_lo), b)
    tph = (SEL_HALF_BLOCKS * NSA_SLC_BLOCK) // tk
    for hf in range(n_halves):
        Q = lhs[hf]
        if mode == "nsa_sel":
            clip = lambda lo, hi: (jnp.maximum(lo, hf * tph) if hf else lo, jnp.minimum(hi, (hf + 1) * tph))
        else:
            clip = lambda lo, hi: (lo, hi)
        if span is not None:
            run(*clip(j_lo, a), tk, Q, True)
        run_visible(*clip(a, b), Q)
        run(*clip(b, j_hi + 1), tk, Q, True)

    l = l_sc[...]
    acc = acc_sc[...] / l
    if mode == "diff":
        o = acc[:tq] - lam_ref[...] * acc[tq:]
        o = o * lax.rsqrt(jnp.mean(o * o, axis=-1, keepdims=True) + EPS) * (gout_ref[...] * out_scale)
        o_ref[...] = o.astype(o_ref.dtype)
    else:
        for g in range(G):
            o_ref[:, g * dv:(g + 1) * dv] = acc[g * tq:(g + 1) * tq].astype(o_ref.dtype)
        if want_lse:
            lse = m_sc[...] + jnp.log(l)
            for g in range(G):
                lse_ref[:, g * LANE:(g + 1) * LANE] = lse[g * tq:(g + 1) * tq]


def attention(q, k, v, *, n_streams, seq, out_cols, qcol, kcol, vcol, ocol, mode="plain", tq, tk, tkb=None, G=1,
              dq, dk, dv=LANE, span=None, bias=None, bcol=None, sel=None, lam=None, gout=None, out_scale=1.0,
              want_lse=False, name="attention"):
    tq = min(tq, seq)
    tk = min(tk, seq)
    tkb = tk if tkb is None else min(tkb, seq)
    assert seq % tq == 0 and seq % tkb == 0 and tkb % tk == 0 and tq % LANE == 0 and tk % LANE == 0
    resident = dict(pipeline_mode=pl.Buffered(1))
    R = 2 * tq if mode == "diff" else G * tq
    Gk = 2 if mode == "diff" else G
    wq = dq if mode == "diff" else G * dq
    in_specs = [pl.BlockSpec((tq, wq), lambda n, i: (i, qcol(n)))]
    args = [q]
    n_halves = 1
    if mode == "nsa_sel":
        n_halves = sel.shape[0]
        in_specs.append(pl.BlockSpec((n_halves, tq, LANE), lambda n, i: (0, i, n)))
        args.append(sel)
    in_specs += [pl.BlockSpec((seq, dk), lambda n, i: (0, kcol(n)), **resident),
                 pl.BlockSpec((seq, dv), lambda n, i: (0, vcol(n)), **resident)]
    args += [k, v]
    x0 = 0
    if bias is not None:
        x0 = _strip_x0(tkb)
        gb = 1 if mode == "diff" else G
        in_specs.append(pl.BlockSpec((gb, tq, x0 + max(tq, tk)), lambda n, i: (bcol(n), 0, 0), **resident))
        args.append(bias)
    if mode == "diff":
        in_specs += [pl.BlockSpec((1, LANE), lambda n, i: (0, 0)), pl.BlockSpec((1, LANE), lambda n, i: (0, 0))]
        args += [lam, gout]
    wo = dv if mode == "diff" else G * dv
    out_specs = [pl.BlockSpec((tq, wo), lambda n, i: (i, ocol(n)))]
    out_shape = [jax.ShapeDtypeStruct((seq, out_cols), BF16)]
    if want_lse:
        out_specs.append(pl.BlockSpec((tq, G * LANE), lambda n, i: (i, ocol(n))))
        out_shape.append(jax.ShapeDtypeStruct((seq, out_cols), F32))
    kern = functools.partial(_attn_kernel, mode=mode, tq=tq, tk=tk, tkb=tkb, G=Gk, dq=dq, dv=dv, span=span,
                             has_bias=bias is not None, x0=x0, n_halves=n_halves, want_lse=want_lse,
                             out_scale=out_scale)
    res = pl.pallas_call(
        kern,
        grid=(n_streams, seq // tq),
        in_specs=in_specs,
        out_specs=out_specs,
        out_shape=out_shape,
        scratch_shapes=[pltpu.VMEM((R, LANE), F32), pltpu.VMEM((R, LANE), F32), pltpu.VMEM((R, dv), F32)],
        compiler_params=_cp(("parallel", "parallel")),
        name=name,
    )(*args)
    return res if want_lse else res[0]


def _dil_combine_kernel(o0, o1, o2, l0, l1, l2, out_ref):
    os_ = (o0, o1, o2)
    ls_ = (l0, l1, l2)
    for j in range(DIL_HP):
        sl = slice(j * LANE, (j + 1) * LANE)
        lse = [r[:, sl] for r in ls_]
        m = jnp.maximum(jnp.maximum(lse[0], lse[1]), lse[2])
        e = [jnp.exp(x - m) for x in lse]
        inv = 1.0 / (e[0] + e[1] + e[2])
        for g in range(len(DIL_PATTERNS)):
            c = (g * DIL_HP + j) * LANE
            out_ref[:, c:c + LANE] = (os_[g][:, sl].astype(F32) * (e[g] * inv)).astype(out_ref.dtype)


def dil_combine(outs, lses, tm=512):
    s = outs[0].shape[0]
    tm = min(tm, s)
    w = DIL_HP * LANE
    spec = pl.BlockSpec((tm, w), lambda i: (i, 0))
    return pl.pallas_call(
        _dil_combine_kernel,
        grid=(s // tm,),
        in_specs=[spec] * 6,
        out_specs=pl.BlockSpec((tm, DIL_HEADS * LANE), lambda i: (i, 0)),
        out_shape=jax.ShapeDtypeStruct((s, DIL_HEADS * LANE), BF16),
        compiler_params=_cp(("parallel",)),
        name="dil_combine",
    )(*outs, *lses)


def _gelu_tanh(y):
    return 0.5 * y * (1.0 + jnp.tanh(0.7978845608028654 * (y + 0.044715 * y * y * y)))


def _compress_kernel(a_ref, pe_ref, w1_ref, w2_ref, g_ref, o_ref, *, norm, transpose):
    a = a_ref[...].astype(F32)
    nc = a.shape[0]
    lo = jnp.dot((a + pe_ref[0]).astype(BF16), w1_ref[0], preferred_element_type=F32)
    hi = jnp.dot((a + pe_ref[1]).astype(BF16), w1_ref[1], preferred_element_type=F32)
    y = lo + pltpu.roll(hi, nc - 1, 0)
    z = jnp.dot(_gelu_tanh(y).astype(BF16), w2_ref[...], preferred_element_type=F32)
    if norm:
        z = _group_norm(z, LANE) * g_ref[...]
    o_ref[...] = (z.T if transpose else z).astype(o_ref.dtype)


def nsa_compress(chunks, pe, w1, w2, gain, *, norm, transpose):
    kvh, nc, w = chunks.shape
    half = NSA_CMP_STRIDE * LANE
    pe2 = pe.astype(F32).reshape(2, 1, half)
    w1s = w1.reshape(2, half, LANE).astype(BF16)
    oshape = (kvh, LANE, nc) if transpose else (kvh, nc, LANE)
    oblock = (None, LANE, nc) if transpose else (None, nc, LANE)
    return pl.pallas_call(
        functools.partial(_compress_kernel, norm=norm, transpose=transpose),
        grid=(kvh,),
        in_specs=[pl.BlockSpec((None, nc, w), lambda h: (h, 0, 0)),
                  pl.BlockSpec((2, 1, half), lambda h: (0, 0, 0)),
                  pl.BlockSpec((2, half, LANE), lambda h: (0, 0, 0)),
                  pl.BlockSpec((LANE, LANE), lambda h: (0, 0)),
                  pl.BlockSpec((1, LANE), lambda h: (0, 0))],
        out_specs=pl.BlockSpec(oblock, lambda h: (h, 0, 0)),
        out_shape=jax.ShapeDtypeStruct(oshape, BF16),
        compiler_params=_cp(("parallel",)),
        name="nsa_compress",
    )(chunks, pe2, w1s, w2.astype(BF16), gain.reshape(1, LANE).astype(F32))


def _cmp_topk_kernel(q_ref, kc_ref, vt_ref, ov_ref, strip_ref, o_ref, sel_ref, *, tq, y0, k_sel, n_halves):
    qi = pl.program_id(1)
    q0 = qi * tq
    G = NSA_GROUP
    nc = kc_ref.shape[0]
    qs = jnp.concatenate([q_ref[:, g * LANE:(g + 1) * LANE] for g in range(G)], axis=0)
    st = lax.dot_general(kc_ref[...], qs, (((1,), (1,)), ((), ())), preferred_element_type=F32)
    n_io = lax.broadcasted_iota(jnp.int32, (nc, tq), 0)
    i_io = lax.broadcasted_iota(jnp.int32, (nc, tq), 1)
    vis = (q0 + i_io - NSA_CMP_STRIDE * n_io - (NSA_CMP_BLOCK - 1)) >= 0
    start = pl.multiple_of(y0 - qi * (tq // NSA_CMP_STRIDE), 8)
    psum = jnp.zeros((nc, tq), F32)
    pts = []
    for g in range(G):
        s = st[:, g * tq:(g + 1) * tq] + strip_ref[g, pl.ds(start, nc), :]
        s = jnp.where(vis, s, NEG)
        m = jnp.max(s, axis=0, keepdims=True)
        e = jnp.where(vis, jnp.exp(s - m), 0.0)
        ssum = jnp.sum(e, axis=0, keepdims=True)
        p = e / jnp.where(ssum > 0, ssum, 1.0)
        psum = psum + p
        pts.append(p.astype(BF16))
    pt = jnp.concatenate(pts, axis=1)
    ot = jnp.dot(vt_ref[...], pt, preferred_element_type=F32)
    for g in range(G):
        o_ref[:, g * LANE:(g + 1) * LANE] = ot[:, g * tq:(g + 1) * tq].T.astype(o_ref.dtype)
    p_hi = psum.astype(BF16)
    p_lo = (psum - p_hi.astype(F32)).astype(BF16)
    imp = (jnp.dot(ov_ref[...], p_hi, preferred_element_type=F32)
           + jnp.dot(ov_ref[...], p_lo, preferred_element_type=F32))
    nb = imp.shape[0]
    b_io = lax.broadcasted_iota(jnp.int32, (nb, tq), 0)
    qblk = (q0 + lax.broadcasted_iota(jnp.int32, (nb, tq), 1)) // NSA_SLC_BLOCK
    forced = (b_io == 0) | (b_io == qblk) | (b_io == qblk - 1)
    val = jnp.where(forced, 3e38, jnp.where(b_io <= qblk, imp, -1.0))
    seln = jnp.full((nb, tq), SEL_NEG, F32)
    for _ in range(k_sel):
        mx = jnp.max(val, axis=0, keepdims=True)
        idx = jnp.min(jnp.where(val == mx, b_io, nb), axis=0, keepdims=True)
        pick = (b_io == idx) & (mx >= 0.0)
        seln = jnp.where(pick, 0.0, seln)
        val = jnp.where(pick, -2.0, val)
    sel = seln.T
    for hf in range(n_halves):
        sel_ref[hf] = sel[:, hf * LANE:(hf + 1) * LANE].astype(sel_ref.dtype)


def _cmp_y0(s):
    return (LANE // NSA_CMP_STRIDE) * (s // LANE - 1)


def cmp_bias_strips(table, s):
    nc = s // NSA_CMP_STRIDE
    y0 = _cmp_y0(s)
    far_rows = -(-(REL_MAX_DIST + NSA_CMP_BLOCK) // NSA_CMP_STRIDE)
    ya = max(y0 - far_rows, 0)
    yb = min(y0 + LANE // NSA_CMP_STRIDE, y0 + nc)
    y = np.arange(ya, yb)[:, None]
    i = np.arange(LANE)[None, :]
    band = jnp.transpose(table.astype(F32)[_rel_bucket_np(NSA_CMP_STRIDE * (y0 - y) + i - (NSA_CMP_BLOCK - 1))], (2, 0, 1))
    h = table.shape[1]
    far = jnp.broadcast_to(table.astype(F32)[REL_BUCKETS - 1][:, None, None], (h, ya, LANE))
    return jnp.concatenate([far, band, jnp.zeros((h, y0 + nc - yb, LANE), F32)], axis=1)


def nsa_cmp_topk(nqn, k_cmp, v_cmp_t, strips, s):
    tq = LANE
    nc = s // NSA_CMP_STRIDE
    n_cmp = nc - NSA_CMP_BLOCK // NSA_CMP_STRIDE + 1
    n_slc = s // NSA_SLC_BLOCK
    nbp = -(-n_slc // LANE) * LANE
    n_halves = nbp // LANE
    cs = np.arange(nc) * NSA_CMP_STRIDE
    ce = cs + NSA_CMP_BLOCK - 1
    ss = np.arange(nbp) * NSA_SLC_BLOCK
    ov = ((cs[None, :] < ss[:, None] + NSA_SLC_BLOCK) & (ce[None, :] >= ss[:, None])).astype(np.float32)
    ov[:, n_cmp:] = 0.0
    ov[n_slc:, :] = 0.0
    y0 = _cmp_y0(s)
    k_sel = min(NSA_TOPK, n_slc)
    return pl.pallas_call(
        functools.partial(_cmp_topk_kernel, tq=tq, y0=y0, k_sel=k_sel, n_halves=n_halves),
        grid=(NSA_KV_HEADS, s // tq),
        in_specs=[pl.BlockSpec((tq, NSA_GROUP * LANE), lambda h, i: (i, h)),
                  pl.BlockSpec((None, nc, LANE), lambda h, i: (h, 0, 0)),
                  pl.BlockSpec((None, LANE, nc), lambda h, i: (h, 0, 0)),
                  pl.BlockSpec((nbp, nc), lambda h, i: (0, 0)),
                  pl.BlockSpec((NSA_GROUP, y0 + nc, tq), lambda h, i: (h, 0, 0))],
        out_specs=[pl.BlockSpec((tq, NSA_GROUP * LANE), lambda h, i: (i, h)),
                   pl.BlockSpec((n_halves, tq, LANE), lambda h, i: (0, i, h))],
        out_shape=[jax.ShapeDtypeStruct((s, NSA_HEADS * LANE), BF16),
                   jax.ShapeDtypeStruct((n_halves, s, NSA_KV_HEADS * LANE), BF16)],
        compiler_params=_cp(("parallel", "parallel")),
        name="nsa_cmp_topk",
    )(nqn, k_cmp, v_cmp_t, jnp.asarray(ov, BF16), strips)


def _nsa_combine_kernel(oc_ref, os_ref, ow_ref, g_ref, o_ref):
    gate = jax.nn.sigmoid(g_ref[...].astype(F32))
    for h in range(NSA_HEADS):
        sl = slice(h * LANE, (h + 1) * LANE)
        o = (gate[:, 3 * h:3 * h + 1] * oc_ref[:, sl].astype(F32)
             + gate[:, 3 * h + 1:3 * h + 2] * os_ref[:, sl].astype(F32)
             + gate[:, 3 * h + 2:3 * h + 3] * ow_ref[:, sl].astype(F32))
        o_ref[:, sl] = o.astype(o_ref.dtype)


def nsa_combine(o_c, o_s, o_w, proj, tm=512):
    s = o_c.shape[0]
    tm = min(tm, s)
    w = NSA_HEADS * LANE
    spec = pl.BlockSpec((tm, w), lambda i: (i, 0))
    return pl.pallas_call(
        _nsa_combine_kernel,
        grid=(s // tm,),
        in_specs=[spec, spec, spec, pl.BlockSpec((tm, LANE), lambda i: (i, C_NG // LANE))],
        out_specs=spec,
        out_shape=jax.ShapeDtypeStruct((s, w), BF16),
        compiler_params=_cp(("parallel",)),
        name="nsa_combine",
    )(o_c, o_s, o_w, proj)


def _xattn_kernel(q_ref, kv_ref, gq_ref, gk_ref, o_ref, *, scale):
    nk = XATTN_HEADS * LANE
    for h in range(XATTN_HEADS):
        sl = slice(h * LANE, (h + 1) * LANE)
        qn = (_group_norm(q_ref[:, sl].astype(F32), LANE) * (gq_ref[...] * scale)).astype(BF16)
        kn = (_group_norm(kv_ref[:, sl].astype(F32), LANE) * gk_ref[...]).astype(BF16)
        s = lax.dot_general(qn, kn, (((1,), (1,)), ((), ())), preferred_element_type=F32)
        m = jnp.max(s, axis=-1, keepdims=True)
        e = jnp.exp(s - m)
        p = e / jnp.sum(e, axis=-1, keepdims=True)
        v = kv_ref[:, nk + h * LANE:nk + (h + 1) * LANE]
        o_ref[:, sl] = jnp.dot(p.astype(v.dtype), v, preferred_element_type=F32).astype(o_ref.dtype)


def xattn_core(qx, kv, qk_norm, tq=512):
    s, w = qx.shape
    tq = min(tq, s)
    m = kv.shape[0]
    return pl.pallas_call(
        functools.partial(_xattn_kernel, scale=HEAD_DIM ** -0.5),
        grid=(s // tq,),
        in_specs=[pl.BlockSpec((tq, w), lambda i: (i, 0)),
                  pl.BlockSpec((m, 2 * w), lambda i: (0, 0)),
                  pl.BlockSpec((1, LANE), lambda i: (0, 0)),
                  pl.BlockSpec((1, LANE), lambda i: (0, 0))],
        out_specs=pl.BlockSpec((tq, w), lambda i: (i, 0)),
        out_shape=jax.ShapeDtypeStruct((s, w), BF16),
        compiler_params=_cp(("parallel",)),
        name="xattn_core",
    )(qx, kv, qk_norm[0].reshape(1, LANE).astype(F32), qk_norm[1].reshape(1, LANE).astype(F32))


def _pack_w_in(w):
    d = w.shape[0]
    z = lambda n: jnp.zeros((d, n), w.dtype)
    parts = [w[:, 0:896], w[:, 1408:1472], z(64), w[:, 896:1408], w[:, 1472:7488],
             w[:, 7488:7512], z(LANE - 3 * NSA_HEADS), w[:, 7512:10200], z(NP_COLS - 10368)]
    return jnp.concatenate(parts, axis=1).astype(BF16)


def _mla_mixer(proj, pos, l, p):
    s = proj.shape[0]
    qm = mla_q_proj(proj, pos, p["mla_q_norm"][l], p["mla_w_uq"][l], p["mla_nope_norm"][l, 0], p["mla_rope_norm"][l, 0])
    km, vm = mla_kv_proj(proj, pos, p["mla_kv_norm"][l], p["mla_w_ukv"][l], p["mla_nope_norm"][l, 1],
                         p["mla_rope_norm"][l, 1])
    ident = lambda n: n
    return attention(qm, km, vm, n_streams=MLA_HEADS, seq=s, out_cols=MLA_HEADS * MLA_V, qcol=ident, kcol=ident,
                     vcol=ident, ocol=ident, tq=ATT_ROWS, tk=ATT_TK, tkb=ATT_TKB, dq=2 * LANE, dk=2 * LANE,
                     name="mla_attention")


def bias_tables(rel_bias, s):
    tabs = {"dil": []}
    for g, (_, dil) in enumerate(DIL_PATTERNS):
        tq = min(DIL_TQ, s // dil)
        tabs["dil"].append(bias_strips(rel_bias[:, BIAS_DIL + g * DIL_HP:BIAS_DIL + (g + 1) * DIL_HP], tq, tq, tq, dil))
    nsa = rel_bias[:, BIAS_NSA:BIAS_NSA + NSA_HEADS]
    tk, tkb = min(ATT_TK, s), min(ATT_TKB, s)
    tabs["nsa"] = bias_strips(nsa, ATT_ROWS // NSA_GROUP, tk, tkb, 1)
    tabs["nsa_cmp"] = cmp_bias_strips(nsa, s)
    tabs["diff"] = bias_strips(rel_bias[:, BIAS_DIFF:BIAS_DIFF + DIFF_HEADS], min(ATT_ROWS // 2, s), tk, tkb, 1)
    return tabs


def _dilated_mixer(proj, l, p, tabs):
    s = proj.shape[0]
    w = DIL_HP * LANE
    dqn = head_rmsnorm(proj, C_DQ, DIL_HEADS, p["dil_qk_norm"][l, 0], LANE, HEAD_DIM ** -0.5)
    dkn = head_rmsnorm(proj, C_DK, DIL_HEADS, p["dil_qk_norm"][l, 1], LANE, 1.0)
    outs, lses = [], []
    ident = lambda n: n
    for g, (window, dil) in enumerate(DIL_PATTERNS):
        n_sub = s // dil
        tq = min(DIL_TQ, n_sub)
        fold = lambda t, c0: t[:, c0 + g * w:c0 + (g + 1) * w].reshape(n_sub, dil * w)
        o, lse = attention(fold(dqn, 0), fold(dkn, 0), fold(proj, C_DV), n_streams=dil * DIL_HP, seq=n_sub,
                           out_cols=dil * w, qcol=ident, kcol=ident, vcol=ident, ocol=ident, tq=tq, tk=tq, dq=LANE,
                           dk=LANE, span=window // dil, bias=tabs["dil"][g], bcol=lambda n: n % DIL_HP,
                           want_lse=True, name=f"dil_attention_{g}")
        outs.append(o.reshape(s, w))
        lses.append(lse.reshape(s, w))
    return dil_combine(outs, lses)


def _nsa_mixer(proj, l, p, tabs):
    s = proj.shape[0]
    gains = p["nsa_qk_norm"][l]
    nqn = head_rmsnorm(proj, C_NQ, NSA_HEADS, gains[0], LANE, HEAD_DIM ** -0.5)
    kwn = head_rmsnorm(proj, C_NKW, NSA_KV_HEADS, gains[3], LANE, 1.0)
    ksel = nsa_sel_keys(proj, gains[2])
    nc = s // NSA_CMP_STRIDE

    def chunks(c0):
        t = proj[:, c0:c0 + NSA_KV_HEADS * LANE].reshape(nc, NSA_CMP_STRIDE, NSA_KV_HEADS, LANE)
        return jnp.transpose(t, (2, 0, 1, 3)).reshape(NSA_KV_HEADS, nc, NSA_CMP_STRIDE * LANE)

    k_cmp = nsa_compress(chunks(C_NKC), p["nsa_cmp_pe"][l, 0], p["nsa_cmp_w1"][l, 0], p["nsa_cmp_w2"][l, 0],
                         gains[1], norm=True, transpose=False)
    v_cmp_t = nsa_compress(chunks(C_NVC), p["nsa_cmp_pe"][l, 1], p["nsa_cmp_w1"][l, 1], p["nsa_cmp_w2"][l, 1],
                           gains[1], norm=False, transpose=True)
    o_c, sel = nsa_cmp_topk(nqn, k_cmp, v_cmp_t, tabs["nsa_cmp"], s)
    ident = lambda n: n
    common = dict(n_streams=NSA_KV_HEADS, seq=s, out_cols=NSA_HEADS * LANE, qcol=ident, ocol=ident,
                  tq=ATT_ROWS // NSA_GROUP, tk=ATT_TK, tkb=ATT_TKB, G=NSA_GROUP, dq=LANE, bias=tabs["nsa"], bcol=ident)
    o_s = attention(nqn, ksel, proj, kcol=ident, vcol=lambda n: C_NVS // LANE + n, mode="nsa_sel", dk=2 * LANE,
                    sel=sel, name="nsa_sel_attention", **common)
    o_w = attention(nqn, kwn, proj, kcol=ident, vcol=lambda n: C_NVW // LANE + n, dk=LANE, span=NSA_WINDOW - 1,
                    name="nsa_win_attention", **common)
    return nsa_combine(o_c, o_s, o_w, proj)


def _diff_mixer(proj, l, p, tabs):
    s = proj.shape[0]
    fqn = head_rmsnorm(proj, C_FQ, DIFF_HEADS, p["diff_qk_norm"][l, 0], DIFF_QK, DIFF_QK ** -0.5)
    fkn = head_rmsnorm(proj, C_FK, DIFF_HEADS, p["diff_qk_norm"][l, 1], DIFF_QK, 1.0)
    lam_init = 0.8 - 0.6 * math.exp(-0.3 * l)
    lv = p["diff_lambda"][l].astype(F32)
    lam = jnp.exp(jnp.sum(lv[0] * lv[1])) - jnp.exp(jnp.sum(lv[2] * lv[3])) + lam_init
    ident = lambda n: n
    return attention(fqn, fkn, proj, n_streams=DIFF_HEADS, seq=s, out_cols=DIFF_HEADS * DIFF_V, qcol=ident,
                     kcol=ident, vcol=lambda n: C_FV // LANE + n, ocol=ident, mode="diff", tq=ATT_ROWS // 2,
                     tk=ATT_TK, tkb=ATT_TKB, dq=LANE, dk=LANE, bias=tabs["diff"], bcol=ident,
                     lam=jnp.full((1, LANE), lam, F32),
                     gout=p["diff_out_norm"][l].reshape(1, LANE).astype(F32), out_scale=1.0 - lam_init,
                     name="diff_attention")


def kernel(x, mem, positions, rel_bias, norm_mix, w_in, mla_q_norm, mla_kv_norm, mla_w_uq, mla_w_ukv, mla_nope_norm, mla_rope_norm, dil_qk_norm, nsa_qk_norm, nsa_cmp_pe, nsa_cmp_w1, nsa_cmp_w2, diff_qk_norm, diff_lambda, diff_out_norm, w_out, norm_xattn, norm_mem, xattn_wq, xattn_wkv, xattn_qk_norm, xattn_wo, norm_ffn, ffn_w_gate, ffn_w_up, ffn_conv_w, ffn_conv_b, ffn_w_down):
    p = dict(mla_q_norm=mla_q_norm, mla_kv_norm=mla_kv_norm, mla_w_uq=mla_w_uq, mla_w_ukv=mla_w_ukv,
             mla_nope_norm=mla_nope_norm, mla_rope_norm=mla_rope_norm, dil_qk_norm=dil_qk_norm,
             nsa_qk_norm=nsa_qk_norm, nsa_cmp_pe=nsa_cmp_pe, nsa_cmp_w1=nsa_cmp_w1, nsa_cmp_w2=nsa_cmp_w2,
             diff_qk_norm=diff_qk_norm, diff_lambda=diff_lambda, diff_out_norm=diff_out_norm)
    b, s, d = x.shape
    assert b == 1
    xs = x.reshape(s, d).astype(F32)
    mems = mem.reshape(mem.shape[1], d).astype(F32)
    pos = positions.reshape(s, 1).astype(jnp.int32)
    tabs = bias_tables(rel_bias, s)
    for l in range(DEPTH):
        h = rmsnorm_rows(xs, norm_mix[l])
        proj = matmul(h, _pack_w_in(w_in[l]), tm=512, tn=1536, out_dtype=BF16, name="in_proj")
        mix = jnp.concatenate([_mla_mixer(proj, pos, l, p), _dilated_mixer(proj, l, p, tabs),
                               _nsa_mixer(proj, l, p, tabs), _diff_mixer(proj, l, p, tabs)], axis=1)
        xs = matmul(mix, w_out[l].astype(BF16), tm=512, tn=1024, out_dtype=F32, residual=xs, name="out_proj")
        hx = rmsnorm_rows(xs, norm_xattn[l])
        qx = matmul(hx, xattn_wq[l].astype(BF16), tm=1024, tn=512, out_dtype=BF16, name="xattn_q")
        kvm = matmul(rmsnorm_rows(mems, norm_mem[l]), xattn_wkv[l].astype(BF16), tm=256, tn=1024, out_dtype=BF16,
                     name="xattn_kv")
        ox = xattn_core(qx, kvm, xattn_qk_norm[l])
        xs = matmul(ox, xattn_wo[l].astype(BF16), tm=1024, tn=1024, out_dtype=F32, residual=xs, name="xattn_o")
        hf = rmsnorm_rows(xs, norm_ffn[l])
        act = ffn_gate_up(hf, ffn_w_gate[l].astype(BF16), ffn_w_up[l].astype(BF16), ffn_conv_w[l], ffn_conv_b[l])
        xs = matmul(act, ffn_w_down[l].astype(BF16), tm=512, tn=512, out_dtype=F32, residual=xs, name="ffn_down")
    return xs.reshape(b, s, d)
```

```python
import functools
import math

import numpy as np
import jax
import jax.numpy as jnp
from jax import lax
from jax.experimental import pallas as pl
from jax.experimental.pallas import tpu as pltpu

F32 = jnp.float32
BF16 = jnp.bfloat16

D_MODEL = 4096
DEPTH = 2
HEAD_DIM = 128
EPS = 1e-6
MLA_HEADS = 8
MLA_Q_RANK = 896
MLA_KV_RANK = 512
MLA_NOPE = 128
MLA_ROPE = 64
MLA_V = 128
ROPE_THETA = 10000.0
DIL_PATTERNS = ((128, 1), (512, 4), (2048, 16))
DIL_HP = 3
DIL_HEADS = DIL_HP * len(DIL_PATTERNS)
NSA_HEADS = 8
NSA_KV_HEADS = 2
NSA_GROUP = NSA_HEADS // NSA_KV_HEADS
NSA_CMP_BLOCK = 32
NSA_CMP_STRIDE = 16
NSA_SLC_BLOCK = 64
NSA_TOPK = 16
NSA_WINDOW = 512
DIFF_HEADS = 7
DIFF_QK = 64
DIFF_V = 128
XATTN_HEADS = 4
FFN_DIM = 11008
REL_BUCKETS = 32
REL_MAX_DIST = 2048
BIAS_DIL = 0
BIAS_NSA = DIL_HEADS
BIAS_DIFF = DIL_HEADS + NSA_HEADS

LANE = 128
VMEM_LIMIT = 56 * 1024 * 1024
NEG = -1e30
SEL_NEG = -32768.0
SEL_HALF_BLOCKS = 128
ATT_ROWS = 512
ATT_TK = 1024
ATT_TKB = 2048
DIL_TQ = 256

C_CQ = 0
C_KR = 896
C_CKV = 1024
C_DQ = 1536
C_DK = 2688
C_DV = 3840
C_NQ = 4992
C_NKC = 6016
C_NVC = 6272
C_NKS = 6528
C_NVS = 6784
C_NKW = 7040
C_NVW = 7296
C_NG = 7552
C_FQ = 7680
C_FK = 8576
C_FV = 9472
NP_COLS = 10752


def _cp(sem):
    return pltpu.CompilerParams(dimension_semantics=sem, vmem_limit_bytes=VMEM_LIMIT)


def _rmsnorm_rows_kernel(x_ref, g_ref, o_ref):
    x = x_ref[...].astype(F32)
    ms = jnp.mean(x * x, axis=-1, keepdims=True)
    o_ref[...] = (x * lax.rsqrt(ms + EPS) * g_ref[...]).astype(o_ref.dtype)


def rmsnorm_rows(x, g, tm=256):
    m, d = x.shape
    tm = min(tm, m)
    return pl.pallas_call(
        _rmsnorm_rows_kernel,
        grid=(m // tm,),
        in_specs=[pl.BlockSpec((tm, d), lambda i: (i, 0)),
                  pl.BlockSpec((1, d), lambda i: (0, 0))],
        out_specs=pl.BlockSpec((tm, d), lambda i: (i, 0)),
        out_shape=jax.ShapeDtypeStruct((m, d), BF16),
        compiler_params=_cp(("parallel",)),
        name="rmsnorm_rows",
    )(x, g.reshape(1, d).astype(F32))


def _matmul_kernel(*refs, has_res):
    if has_res:
        a_ref, w_ref, r_ref, o_ref = refs
    else:
        a_ref, w_ref, o_ref = refs
    acc = jnp.dot(a_ref[...], w_ref[...], preferred_element_type=F32)
    if has_res:
        acc = acc + r_ref[...]
    o_ref[...] = acc.astype(o_ref.dtype)


def matmul(a, w, *, tm, tn, out_dtype, residual=None, name="matmul"):
    m, k = a.shape
    n = w.shape[1]
    tm = min(tm, m)
    tn = min(tn, n)
    assert m % tm == 0 and n % tn == 0
    in_specs = [pl.BlockSpec((tm, k), lambda j, i: (i, 0)),
                pl.BlockSpec((k, tn), lambda j, i: (0, j))]
    args = [a, w]
    if residual is not None:
        in_specs.append(pl.BlockSpec((tm, tn), lambda j, i: (i, j)))
        args.append(residual)
    return pl.pallas_call(
        functools.partial(_matmul_kernel, has_res=residual is not None),
        grid=(n // tn, m // tm),
        in_specs=in_specs,
        out_specs=pl.BlockSpec((tm, tn), lambda j, i: (i, j)),
        out_shape=jax.ShapeDtypeStruct((m, n), out_dtype),
        compiler_params=_cp(("parallel", "parallel")),
        name=name,
    )(*args)


def _ffn_gate_up_kernel(h_ref, wg_ref, wu_ref, cw_ref, cb_ref, o_ref, carry_ref):
    i = pl.program_id(1)

    @pl.when(i == 0)
    def _():
        carry_ref[...] = jnp.zeros_like(carry_ref)

    h = h_ref[...]
    u = jnp.dot(h, wg_ref[...], preferred_element_type=F32)
    up = jnp.dot(h, wu_ref[...], preferred_element_type=F32)
    tm = u.shape[0]
    prev = carry_ref[...]
    row = lax.broadcasted_iota(jnp.int32, u.shape, 0)
    u1 = jnp.where(row == 0, prev[7:8, :], pltpu.roll(u, 1, 0))
    u2 = pltpu.roll(u, 2, 0)
    u2 = jnp.where(row == 0, prev[6:7, :], jnp.where(row == 1, prev[7:8, :], u2))
    cw = cw_ref[...]
    g = cw[0:1, :] * u2 + cw[1:2, :] * u1 + cw[2:3, :] * u + cb_ref[...]
    o_ref[...] = (g * jax.nn.sigmoid(g) * up).astype(o_ref.dtype)
    carry_ref[...] = u[tm - 8:, :]


def ffn_gate_up(h, wg, wu, conv_w, conv_b, tm=1024, tf=256):
    m, d = h.shape
    f = wg.shape[1]
    tm = min(tm, m)
    assert f % tf == 0 and m % tm == 0
    return pl.pallas_call(
        _ffn_gate_up_kernel,
        grid=(f // tf, m // tm),
        in_specs=[pl.BlockSpec((tm, d), lambda j, i: (i, 0)),
                  pl.BlockSpec((d, tf), lambda j, i: (0, j)),
                  pl.BlockSpec((d, tf), lambda j, i: (0, j)),
                  pl.BlockSpec((3, tf), lambda j, i: (0, j)),
                  pl.BlockSpec((1, tf), lambda j, i: (0, j))],
        out_specs=pl.BlockSpec((tm, tf), lambda j, i: (i, j)),
        out_shape=jax.ShapeDtypeStruct((m, f), BF16),
        scratch_shapes=[pltpu.VMEM((8, tf), F32)],
        compiler_params=_cp(("arbitrary", "arbitrary")),
        name="ffn_gate_up",
    )(h, wg, wu, conv_w.astype(F32), conv_b.reshape(1, f).astype(F32))


def _group_norm(x, d):
    sq = x * x
    tot = jnp.sum(sq, axis=-1, keepdims=True)
    if d == LANE:
        ms = tot / d
    else:
        lane = lax.broadcasted_iota(jnp.int32, x.shape, 1)
        lo = jnp.sum(jnp.where(lane < d, sq, 0.0), axis=-1, keepdims=True)
        ms = jnp.where(lane < d, lo, tot - lo) / d
    return x * lax.rsqrt(ms + EPS)


def _head_norm_kernel(x_ref, g_ref, o_ref, *, d, scale):
    x = x_ref[...].astype(F32)
    o_ref[...] = (_group_norm(x, d) * (g_ref[...] * scale)).astype(o_ref.dtype)


def head_rmsnorm(src, col0, ncb, gain, d, scale, tm=2048):
    s = src.shape[0]
    tm = min(tm, s)
    cb0 = col0 // LANE
    g = jnp.tile(gain.astype(F32), LANE // d).reshape(1, LANE)
    return pl.pallas_call(
        functools.partial(_head_norm_kernel, d=d, scale=scale),
        grid=(s // tm, ncb),
        in_specs=[pl.BlockSpec((tm, LANE), lambda i, c: (i, cb0 + c)),
                  pl.BlockSpec((1, LANE), lambda i, c: (0, 0))],
        out_specs=pl.BlockSpec((tm, LANE), lambda i, c: (i, c)),
        out_shape=jax.ShapeDtypeStruct((s, ncb * LANE), BF16),
        compiler_params=_cp(("parallel", "parallel")),
        name="head_rmsnorm",
    )(src, g)


def _nsa_ksel_kernel(x_ref, g_ref, o_ref):
    tm = x_ref.shape[0]
    x = x_ref[...].astype(F32)
    o_ref[:, :LANE] = (_group_norm(x, LANE) * g_ref[...]).astype(o_ref.dtype)
    row = pl.program_id(0) * tm + lax.broadcasted_iota(jnp.int32, (tm, LANE), 0)
    col = lax.broadcasted_iota(jnp.int32, (tm, LANE), 1)
    hot = ((row // NSA_SLC_BLOCK) % SEL_HALF_BLOCKS) == col
    o_ref[:, LANE:] = jnp.where(hot, 1.0, 0.0).astype(o_ref.dtype)


def nsa_sel_keys(proj, gain, tm=2048):
    s = proj.shape[0]
    tm = min(tm, s)
    cb0 = C_NKS // LANE
    return pl.pallas_call(
        _nsa_ksel_kernel,
        grid=(s // tm, NSA_KV_HEADS),
        in_specs=[pl.BlockSpec((tm, LANE), lambda i, c: (i, cb0 + c)),
                  pl.BlockSpec((1, LANE), lambda i, c: (0, 0))],
        out_specs=pl.BlockSpec((tm, 2 * LANE), lambda i, c: (i, c)),
        out_shape=jax.ShapeDtypeStruct((s, NSA_KV_HEADS * 2 * LANE), BF16),
        compiler_params=_cp(("parallel", "parallel")),
        name="nsa_sel_keys",
    )(proj, gain.reshape(1, LANE).astype(F32))


def _rope_tables(pos_ref, inv_ref, sign_ref):
    ang = pos_ref[...].astype(F32) * inv_ref[...]
    return jnp.cos(ang), jnp.sin(ang) * sign_ref[...]


def _rope_apply(x, cosv, sin_signed):
    lane = lax.broadcasted_iota(jnp.int32, x.shape, 1)
    half = MLA_ROPE // 2
    swapped = jnp.where(lane < half, pltpu.roll(x, LANE - half, 1), pltpu.roll(x, half, 1))
    return x * cosv + swapped * sin_signed


def _latent_norm(c_ref, g_ref):
    c = c_ref[...].astype(F32)
    ms = jnp.mean(c * c, axis=-1, keepdims=True)
    return (c * lax.rsqrt(ms + EPS) * g_ref[...]).astype(BF16)


def _mla_q_kernel(c_ref, pos_ref, gq_ref, w_ref, gn_ref, gr_ref, inv_ref, sign_ref, o_ref, *, scale):
    u = jnp.dot(_latent_norm(c_ref, gq_ref), w_ref[...], preferred_element_type=F32)
    cosv, sinv = _rope_tables(pos_ref, inv_ref, sign_ref)
    for h in range(MLA_HEADS):
        b = 2 * LANE * h
        nope = _group_norm(u[:, b:b + LANE], LANE) * (gn_ref[...] * scale)
        rp = u[:, b + LANE:b + 2 * LANE]
        rp = rp * lax.rsqrt(jnp.sum(rp * rp, axis=-1, keepdims=True) / MLA_ROPE + EPS) * gr_ref[...]
        rp = _rope_apply(rp, cosv, sinv) * scale
        o_ref[:, b:b + LANE] = nope.astype(o_ref.dtype)
        o_ref[:, b + LANE:b + 2 * LANE] = rp.astype(o_ref.dtype)


def _mla_kv_kernel(c_ref, kr_ref, pos_ref, gkv_ref, w_ref, gn_ref, gr_ref, inv_ref, sign_ref, k_ref, v_ref):
    kv = jnp.dot(_latent_norm(c_ref, gkv_ref), w_ref[...], preferred_element_type=F32)
    cosv, sinv = _rope_tables(pos_ref, inv_ref, sign_ref)
    kr = kr_ref[...].astype(F32)
    kr = kr * lax.rsqrt(jnp.sum(kr * kr, axis=-1, keepdims=True) / MLA_ROPE + EPS) * gr_ref[...]
    kr = _rope_apply(kr, cosv, sinv).astype(k_ref.dtype)
    nv = MLA_HEADS * LANE
    for h in range(MLA_HEADS):
        kn = _group_norm(kv[:, h * LANE:(h + 1) * LANE], LANE) * gn_ref[...]
        k_ref[:, 2 * LANE * h:2 * LANE * h + LANE] = kn.astype(k_ref.dtype)
        k_ref[:, 2 * LANE * h + LANE:2 * LANE * (h + 1)] = kr
    v_ref[...] = kv[:, nv:].astype(v_ref.dtype)


def _rope_consts():
    half = MLA_ROPE // 2
    inv = ROPE_THETA ** (-jnp.arange(half, dtype=F32) / half)
    inv = jnp.concatenate([inv, inv, jnp.zeros((LANE - MLA_ROPE,), F32)]).reshape(1, LANE)
    sign = np.zeros((1, LANE), np.float32)
    sign[0, :half] = -1.0
    sign[0, half:MLA_ROPE] = 1.0
    return inv, jnp.asarray(sign)


def _pad_lanes(v, n=LANE):
    return jnp.concatenate([v.astype(F32), jnp.zeros((n - v.shape[0],), F32)]).reshape(1, n)


def mla_q_proj(proj, pos, q_norm, w_uq, nope_gain, rope_gain, tm=512):
    s = proj.shape[0]
    tm = min(tm, s)
    inv, sign = _rope_consts()
    w = jnp.pad(w_uq.reshape(MLA_Q_RANK, MLA_HEADS, MLA_NOPE + MLA_ROPE),
                ((0, 0), (0, 0), (0, LANE - MLA_ROPE))).reshape(MLA_Q_RANK, MLA_HEADS * 2 * LANE).astype(BF16)
    wq = MLA_HEADS * 2 * LANE
    const = lambda i: (0, 0)
    return pl.pallas_call(
        functools.partial(_mla_q_kernel, scale=(MLA_NOPE + MLA_ROPE) ** -0.5),
        grid=(s // tm,),
        in_specs=[pl.BlockSpec((tm, MLA_Q_RANK), lambda i: (i, C_CQ // MLA_Q_RANK)),
                  pl.BlockSpec((tm, 1), lambda i: (i, 0)),
                  pl.BlockSpec((1, MLA_Q_RANK), const),
                  pl.BlockSpec((MLA_Q_RANK, wq), const),
                  pl.BlockSpec((1, LANE), const), pl.BlockSpec((1, LANE), const),
                  pl.BlockSpec((1, LANE), const), pl.BlockSpec((1, LANE), const)],
        out_specs=pl.BlockSpec((tm, wq), lambda i: (i, 0)),
        out_shape=jax.ShapeDtypeStruct((s, wq), BF16),
        compiler_params=_cp(("parallel",)),
        name="mla_q_proj",
    )(proj, pos, q_norm.reshape(1, -1).astype(F32), w, nope_gain.reshape(1, LANE).astype(F32),
      _pad_lanes(rope_gain), inv, sign)


def mla_kv_proj(proj, pos, kv_norm, w_ukv, nope_gain, rope_gain, tm=512):
    s = proj.shape[0]
    tm = min(tm, s)
    inv, sign = _rope_consts()
    w = jnp.transpose(w_ukv.reshape(MLA_KV_RANK, MLA_HEADS, 2, LANE), (0, 2, 1, 3))
    w = w.reshape(MLA_KV_RANK, 2 * MLA_HEADS * LANE).astype(BF16)
    wk = MLA_HEADS * 2 * LANE
    wv = MLA_HEADS * MLA_V
    const = lambda i: (0, 0)
    return pl.pallas_call(
        _mla_kv_kernel,
        grid=(s // tm,),
        in_specs=[pl.BlockSpec((tm, MLA_KV_RANK), lambda i: (i, C_CKV // MLA_KV_RANK)),
                  pl.BlockSpec((tm, LANE), lambda i: (i, C_KR // LANE)),
                  pl.BlockSpec((tm, 1), lambda i: (i, 0)),
                  pl.BlockSpec((1, MLA_KV_RANK), const),
                  pl.BlockSpec((MLA_KV_RANK, 2 * MLA_HEADS * LANE), const),
                  pl.BlockSpec((1, LANE), const), pl.BlockSpec((1, LANE), const),
                  pl.BlockSpec((1, LANE), const), pl.BlockSpec((1, LANE), const)],
        out_specs=[pl.BlockSpec((tm, wk), lambda i: (i, 0)),
                   pl.BlockSpec((tm, wv), lambda i: (i, 0))],
        out_shape=[jax.ShapeDtypeStruct((s, wk), BF16), jax.ShapeDtypeStruct((s, wv), BF16)],
        compiler_params=_cp(("parallel",)),
        name="mla_kv_proj",
    )(proj, proj, pos, kv_norm.reshape(1, -1).astype(F32), w, nope_gain.reshape(1, LANE).astype(F32),
      _pad_lanes(rope_gain), inv, sign)


def _rel_bucket_np(dist):
    n = np.maximum(dist, 0)
    exact = REL_BUCKETS // 2
    ratio = np.log(np.maximum(n, 1).astype(np.float32) / np.float32(exact)) / np.float32(math.log(REL_MAX_DIST / exact))
    large = np.minimum(exact + (ratio.astype(np.float32) * (REL_BUCKETS - exact)).astype(np.int32), REL_BUCKETS - 1)
    return np.where(n < exact, n, large).astype(np.int32)


def _strip_x0(tk):
    return -(-(REL_MAX_DIST - 1 + tk) // LANE) * LANE


def bias_strips(table, tq, tk, tkb, dist_scale):
    x0 = _strip_x0(tkb)
    width = x0 + max(tq, tk)
    period = -(-(width + tq) // LANE) * LANE
    t = np.arange(period)
    t = np.where(t < width, t, t - period)
    u = table.astype(F32)[_rel_bucket_np((x0 - t) * dist_scale)].T
    h = u.shape[0]
    return pl.pallas_call(
        functools.partial(_toeplitz_kernel, tq=tq, width=width),
        grid=(h,),
        in_specs=[pl.BlockSpec((None, 1, period), lambda n: (n, 0, 0))],
        out_specs=pl.BlockSpec((None, tq, width), lambda n: (n, 0, 0)),
        out_shape=jax.ShapeDtypeStruct((h, tq, width), F32),
        compiler_params=_cp(("parallel",)),
        name="bias_strips",
    )(u.reshape(h, 1, period))


def _toeplitz_kernel(u_ref, o_ref, *, tq, width):
    rows = jnp.broadcast_to(u_ref[...], (tq, u_ref.shape[1]))
    o_ref[...] = pltpu.roll(rows, 0, 1, stride=1, stride_axis=0)[:, :width]


def _attn_kernel(*refs, mode, tq, tk, tkb, G, dq, dv, span, has_bias, far_zero, x0, n_halves, want_lse, out_scale):
    refs = list(refs)
    q_ref = refs.pop(0)
    sel_ref = refs.pop(0) if mode == "nsa_sel" else None
    k_ref = refs.pop(0)
    v_ref = refs.pop(0)
    bias_ref = refs.pop(0) if has_bias else None
    lam_ref = gout_ref = None
    if mode == "diff":
        lam_ref = refs.pop(0)
        gout_ref = refs.pop(0)
    o_ref = refs.pop(0)
    lse_ref = refs.pop(0) if want_lse else None
    m_sc, l_sc, acc_sc = refs

    qi = pl.program_id(1)
    q0 = qi * tq
    n_tiles = k_ref.shape[0] // tk

    if mode == "diff":
        q = q_ref[...]
        lane = lax.broadcasted_iota(jnp.int32, q.shape, 1)
        zero = jnp.zeros_like(q)
        lhs = [jnp.concatenate([jnp.where(lane < DIFF_QK, q, zero), jnp.where(lane >= DIFF_QK, q, zero)], axis=0)]
    elif mode == "nsa_sel":
        lhs = []
        for hf in range(n_halves):
            sel = sel_ref[hf]
            lhs.append(jnp.concatenate(
                [jnp.concatenate([q_ref[:, g * dq:(g + 1) * dq], sel], axis=1) for g in range(G)], axis=0))
    else:
        lhs = [jnp.concatenate([q_ref[:, g * dq:(g + 1) * dq] for g in range(G)], axis=0) if G > 1 else q_ref[...]]

    m_sc[...] = jnp.full_like(m_sc, NEG)
    l_sc[...] = jnp.zeros_like(l_sc)
    acc_sc[...] = jnp.zeros_like(acc_sc)

    def logits(k0, w, Q, masked, biased):
        kt = k_ref[pl.ds(k0, w), :]
        s = lax.dot_general(Q, kt, (((1,), (1,)), ((), ())), preferred_element_type=F32)
        has_bias = biased
        if has_bias or masked:
            if masked:
                rel = (q0 - k0) + lax.broadcasted_iota(jnp.int32, (tq, w), 0) - lax.broadcasted_iota(jnp.int32, (tq, w), 1)
                vis = rel >= 0
                if span is not None:
                    vis = jnp.logical_and(vis, rel <= span)
            if has_bias:
                start = pl.multiple_of(jnp.maximum(x0 - (q0 - k0), 0), LANE)
            parts = []
            for g in range(G):
                sg = s[g * tq:(g + 1) * tq]
                if has_bias:
                    sg = sg + bias_ref[0 if mode == "diff" else g, :, pl.ds(start, w)]
                if masked:
                    sg = jnp.where(vis, sg, NEG)
                parts.append(sg)
            s = jnp.concatenate(parts, axis=0) if G > 1 else parts[0]
        return s

    def update(s, k0, w):
        vt = v_ref[pl.ds(k0, w), :]
        m_prev = m_sc[...]
        m_new = jnp.maximum(m_prev, jnp.max(s, axis=-1, keepdims=True))
        alpha = jnp.exp(m_prev - m_new)
        p = jnp.exp(s - jnp.tile(m_new, (1, w // LANE)))
        l_sc[...] = alpha * l_sc[...] + jnp.sum(p, axis=-1, keepdims=True)
        acc_sc[...] = alpha * acc_sc[...] + jnp.dot(p.astype(vt.dtype), vt, preferred_element_type=F32)
        m_sc[...] = m_new

    def tile(k0, w, Q, masked, biased):
        if w == tkb and tkb > tk:
            h = w // 2
            sa = logits(k0, h, Q, masked, biased)
            sb = logits(k0 + h, h, Q, masked, biased)
            update(sa, k0, h)
            update(sb, k0 + h, h)
        else:
            update(logits(k0, w, Q, masked, biased), k0, w)

    def run(lo, hi, w, Q, masked, biased=has_bias):
        def body(j, c):
            tile(pl.multiple_of(j * w, w), w, Q, masked, biased)
            return c
        lax.fori_loop(lo, hi, body, 0)

    def run_visible(lo, hi, Q):
        if tkb == tk:
            run(lo, hi, tk, Q, False)
            return
        r = tkb // tk
        up = jnp.minimum(-(-lo // r) * r, hi)
        dn = jnp.maximum((hi // r) * r, up)
        if not (isinstance(lo, int) and lo % r == 0):
            run(lo, up, tk, Q, False)
        if has_bias and far_zero:
            far = jnp.clip(jnp.maximum(q0 - (REL_MAX_DIST - 1), 0) // tkb, up // r, dn // r)
            run(up // r, far, tkb, Q, False, False)
            run(far, dn // r, tkb, Q, False)
        else:
            run(up // r, dn // r, tkb, Q, False)
        run(dn, hi, tk, Q, False)

    j_hi = jnp.minimum((q0 + tq - 1) // tk, n_tiles - 1)
    b = (q0 + 1) // tk
    if span is None:
        j_lo = 0
        a = 0
    else:
        j_lo = jnp.maximum(q0 - span, 0) // tk
        a = jnp.maximum(q0 + tq - 1 - span + tk - 1, 0) // tk
        a = jnp.minimum(jnp.maximum(a, j_lo), b)
    tph = (SEL_HALF_BLOCKS * NSA_SLC_BLOCK) // tk
    for hf in range(n_halves):
        Q = lhs[hf]
        if mode == "nsa_sel":
            clip = lambda lo, hi: (jnp.maximum(lo, hf * tph) if hf else lo, jnp.minimum(hi, (hf + 1) * tph))
        else:
            clip = lambda lo, hi: (lo, hi)
        if span is not None:
            run(*clip(j_lo, a), tk, Q, True)
        run_visible(*clip(a, b), Q)
        run(*clip(b, j_hi + 1), tk, Q, True)

    l = l_sc[...]
    acc = acc_sc[...] / l
    if mode == "diff":
        o = acc[:tq] - lam_ref[...] * acc[tq:]
        o = o * lax.rsqrt(jnp.mean(o * o, axis=-1, keepdims=True) + EPS) * (gout_ref[...] * out_scale)
        o_ref[...] = o.astype(o_ref.dtype)
    else:
        for g in range(G):
            o_ref[:, g * dv:(g + 1) * dv] = acc[g * tq:(g + 1) * tq].astype(o_ref.dtype)
        if want_lse:
            lse = m_sc[...] + jnp.log(l)
            for g in range(G):
                lse_ref[:, g * LANE:(g + 1) * LANE] = lse[g * tq:(g + 1) * tq]


def attention(q, k, v, *, n_streams, seq, out_cols, qcol, kcol, vcol, ocol, mode="plain", tq, tk, tkb=None, G=1,
              dq, dk, dv=LANE, span=None, bias=None, bcol=None, sel=None, lam=None, gout=None, out_scale=1.0,
              want_lse=False, bias_far_zero=False, name="attention"):
    tq = min(tq, seq)
    tk = min(tk, seq)
    tkb = tk if tkb is None else min(tkb, seq)
    assert seq % tq == 0 and seq % tkb == 0 and tkb % tk == 0 and tq % LANE == 0 and tk % LANE == 0
    resident = dict(pipeline_mode=pl.Buffered(1))
    R = 2 * tq if mode == "diff" else G * tq
    Gk = 2 if mode == "diff" else G
    wq = dq if mode == "diff" else G * dq
    in_specs = [pl.BlockSpec((tq, wq), lambda n, i: (i, qcol(n)))]
    args = [q]
    n_halves = 1
    if mode == "nsa_sel":
        n_halves = sel.shape[0]
        in_specs.append(pl.BlockSpec((n_halves, tq, LANE), lambda n, i: (0, i, n)))
        args.append(sel)
    in_specs += [pl.BlockSpec((seq, dk), lambda n, i: (0, kcol(n)), **resident),
                 pl.BlockSpec((seq, dv), lambda n, i: (0, vcol(n)), **resident)]
    args += [k, v]
    x0 = 0
    if bias is not None:
        x0 = _strip_x0(tkb)
        gb = 1 if mode == "diff" else G
        in_specs.append(pl.BlockSpec((gb, tq, x0 + max(tq, tk)), lambda n, i: (bcol(n), 0, 0), **resident))
        args.append(bias)
    if mode == "diff":
        in_specs += [pl.BlockSpec((1, LANE), lambda n, i: (0, 0)), pl.BlockSpec((1, LANE), lambda n, i: (0, 0))]
        args += [lam, gout]
    wo = dv if mode == "diff" else G * dv
    out_specs = [pl.BlockSpec((tq, wo), lambda n, i: (i, ocol(n)))]
    out_shape = [jax.ShapeDtypeStruct((seq, out_cols), BF16)]
    if want_lse:
        out_specs.append(pl.BlockSpec((tq, G * LANE), lambda n, i: (i, ocol(n))))
        out_shape.append(jax.ShapeDtypeStruct((seq, out_cols), F32))
    kern = functools.partial(_attn_kernel, mode=mode, tq=tq, tk=tk, tkb=tkb, G=Gk, dq=dq, dv=dv, span=span,
                             has_bias=bias is not None, far_zero=bias_far_zero, x0=x0, n_halves=n_halves,
                             want_lse=want_lse,
                             out_scale=out_scale)
    res = pl.pallas_call(
        kern,
        grid=(n_streams, seq // tq),
        in_specs=in_specs,
        out_specs=out_specs,
        out_shape=out_shape,
        scratch_shapes=[pltpu.VMEM((R, LANE), F32), pltpu.VMEM((R, LANE), F32), pltpu.VMEM((R, dv), F32)],
        compiler_params=_cp(("parallel", "parallel")),
        name=name,
    )(*args)
    return res if want_lse else res[0]


def _dil_combine_kernel(o0, o1, o2, l0, l1, l2, out_ref):
    os_ = (o0, o1, o2)
    ls_ = (l0, l1, l2)
    for j in range(DIL_HP):
        sl = slice(j * LANE, (j + 1) * LANE)
        lse = [r[:, sl] for r in ls_]
        m = jnp.maximum(jnp.maximum(lse[0], lse[1]), lse[2])
        e = [jnp.exp(x - m) for x in lse]
        inv = 1.0 / (e[0] + e[1] + e[2])
        for g in range(len(DIL_PATTERNS)):
            c = (g * DIL_HP + j) * LANE
            out_ref[:, c:c + LANE] = (os_[g][:, sl].astype(F32) * (e[g] * inv)).astype(out_ref.dtype)


def dil_combine(outs, lses, tm=512):
    s = outs[0].shape[0]
    tm = min(tm, s)
    w = DIL_HP * LANE
    spec = pl.BlockSpec((tm, w), lambda i: (i, 0))
    return pl.pallas_call(
        _dil_combine_kernel,
        grid=(s // tm,),
        in_specs=[spec] * 6,
        out_specs=pl.BlockSpec((tm, DIL_HEADS * LANE), lambda i: (i, 0)),
        out_shape=jax.ShapeDtypeStruct((s, DIL_HEADS * LANE), BF16),
        compiler_params=_cp(("parallel",)),
        name="dil_combine",
    )(*outs, *lses)


def _gelu_tanh(y):
    return 0.5 * y * (1.0 + jnp.tanh(0.7978845608028654 * (y + 0.044715 * y * y * y)))


def _compress_kernel(a_ref, pe_ref, w1_ref, w2_ref, g_ref, o_ref, *, norm, transpose):
    a = a_ref[...].astype(F32)
    nc = a.shape[0]
    lo = jnp.dot((a + pe_ref[0]).astype(BF16), w1_ref[0], preferred_element_type=F32)
    hi = jnp.dot((a + pe_ref[1]).astype(BF16), w1_ref[1], preferred_element_type=F32)
    y = lo + pltpu.roll(hi, nc - 1, 0)
    z = jnp.dot(_gelu_tanh(y).astype(BF16), w2_ref[...], preferred_element_type=F32)
    if norm:
        z = _group_norm(z, LANE) * g_ref[...]
    o_ref[...] = (z.T if transpose else z).astype(o_ref.dtype)


def nsa_compress(chunks, pe, w1, w2, gain, *, norm, transpose):
    kvh, nc, w = chunks.shape
    half = NSA_CMP_STRIDE * LANE
    pe2 = pe.astype(F32).reshape(2, 1, half)
    w1s = w1.reshape(2, half, LANE).astype(BF16)
    oshape = (kvh, LANE, nc) if transpose else (kvh, nc, LANE)
    oblock = (None, LANE, nc) if transpose else (None, nc, LANE)
    return pl.pallas_call(
        functools.partial(_compress_kernel, norm=norm, transpose=transpose),
        grid=(kvh,),
        in_specs=[pl.BlockSpec((None, nc, w), lambda h: (h, 0, 0)),
                  pl.BlockSpec((2, 1, half), lambda h: (0, 0, 0)),
                  pl.BlockSpec((2, half, LANE), lambda h: (0, 0, 0)),
                  pl.BlockSpec((LANE, LANE), lambda h: (0, 0)),
                  pl.BlockSpec((1, LANE), lambda h: (0, 0))],
        out_specs=pl.BlockSpec(oblock, lambda h: (h, 0, 0)),
        out_shape=jax.ShapeDtypeStruct(oshape, BF16),
        compiler_params=_cp(("parallel",)),
        name="nsa_compress",
    )(chunks, pe2, w1s, w2.astype(BF16), gain.reshape(1, LANE).astype(F32))


def _cmp_topk_kernel(q_ref, kc_ref, vt_ref, ov_ref, strip_ref, o_ref, sel_ref, *, tq, y0, k_sel, n_halves):
    qi = pl.program_id(1)
    q0 = qi * tq
    G = NSA_GROUP
    nc = kc_ref.shape[0]
    qs = jnp.concatenate([q_ref[:, g * LANE:(g + 1) * LANE] for g in range(G)], axis=0)
    st = lax.dot_general(kc_ref[...], qs, (((1,), (1,)), ((), ())), preferred_element_type=F32)
    n_io = lax.broadcasted_iota(jnp.int32, (nc, tq), 0)
    i_io = lax.broadcasted_iota(jnp.int32, (nc, tq), 1)
    vis = (q0 + i_io - NSA_CMP_STRIDE * n_io - (NSA_CMP_BLOCK - 1)) >= 0
    start = pl.multiple_of(y0 - qi * (tq // NSA_CMP_STRIDE), 8)
    psum = jnp.zeros((nc, tq), F32)
    pts = []
    for g in range(G):
        s = st[:, g * tq:(g + 1) * tq] + strip_ref[g, pl.ds(start, nc), :]
        s = jnp.where(vis, s, NEG)
        m = jnp.max(s, axis=0, keepdims=True)
        e = jnp.where(vis, jnp.exp(s - m), 0.0)
        ssum = jnp.sum(e, axis=0, keepdims=True)
        p = e / jnp.where(ssum > 0, ssum, 1.0)
        psum = psum + p
        pts.append(p.astype(BF16))
    pt = jnp.concatenate(pts, axis=1)
    ot = jnp.dot(vt_ref[...], pt, preferred_element_type=F32)
    for g in range(G):
        o_ref[:, g * LANE:(g + 1) * LANE] = ot[:, g * tq:(g + 1) * tq].T.astype(o_ref.dtype)
    p_hi = psum.astype(BF16)
    p_lo = (psum - p_hi.astype(F32)).astype(BF16)
    imp = (jnp.dot(ov_ref[...], p_hi, preferred_element_type=F32)
           + jnp.dot(ov_ref[...], p_lo, preferred_element_type=F32))
    nb = imp.shape[0]
    b_io = lax.broadcasted_iota(jnp.int32, (nb, tq), 0)
    qblk = (q0 + lax.broadcasted_iota(jnp.int32, (nb, tq), 1)) // NSA_SLC_BLOCK
    forced = (b_io == 0) | (b_io == qblk) | (b_io == qblk - 1)
    val = jnp.where(forced, 3e38, jnp.where(b_io <= qblk, imp, -1.0))
    seln = jnp.full((nb, tq), SEL_NEG, F32)
    for _ in range(k_sel):
        mx = jnp.max(val, axis=0, keepdims=True)
        idx = jnp.min(jnp.where(val == mx, b_io, nb), axis=0, keepdims=True)
        pick = (b_io == idx) & (mx >= 0.0)
        seln = jnp.where(pick, 0.0, seln)
        val = jnp.where(pick, -2.0, val)
    sel = seln.T
    for hf in range(n_halves):
        sel_ref[hf] = sel[:, hf * LANE:(hf + 1) * LANE].astype(sel_ref.dtype)


def _cmp_y0(s):
    return (LANE // NSA_CMP_STRIDE) * (s // LANE - 1)


def cmp_bias_strips(table, s):
    nc = s // NSA_CMP_STRIDE
    y0 = _cmp_y0(s)
    far_rows = -(-(REL_MAX_DIST + NSA_CMP_BLOCK) // NSA_CMP_STRIDE)
    ya = max(y0 - far_rows, 0)
    yb = min(y0 + LANE // NSA_CMP_STRIDE, y0 + nc)
    y = np.arange(ya, yb)[:, None]
    i = np.arange(LANE)[None, :]
    band = jnp.transpose(table.astype(F32)[_rel_bucket_np(NSA_CMP_STRIDE * (y0 - y) + i - (NSA_CMP_BLOCK - 1))], (2, 0, 1))
    h = table.shape[1]
    far = jnp.broadcast_to(table.astype(F32)[REL_BUCKETS - 1][:, None, None], (h, ya, LANE))
    return jnp.concatenate([far, band, jnp.zeros((h, y0 + nc - yb, LANE), F32)], axis=1)


def nsa_cmp_topk(nqn, k_cmp, v_cmp_t, strips, s):
    tq = LANE
    nc = s // NSA_CMP_STRIDE
    n_cmp = nc - NSA_CMP_BLOCK // NSA_CMP_STRIDE + 1
    n_slc = s // NSA_SLC_BLOCK
    nbp = -(-n_slc // LANE) * LANE
    n_halves = nbp // LANE
    cs = np.arange(nc) * NSA_CMP_STRIDE
    ce = cs + NSA_CMP_BLOCK - 1
    ss = np.arange(nbp) * NSA_SLC_BLOCK
    ov = ((cs[None, :] < ss[:, None] + NSA_SLC_BLOCK) & (ce[None, :] >= ss[:, None])).astype(np.float32)
    ov[:, n_cmp:] = 0.0
    ov[n_slc:, :] = 0.0
    y0 = _cmp_y0(s)
    k_sel = min(NSA_TOPK, n_slc)
    return pl.pallas_call(
        functools.partial(_cmp_topk_kernel, tq=tq, y0=y0, k_sel=k_sel, n_halves=n_halves),
        grid=(NSA_KV_HEADS, s // tq),
        in_specs=[pl.BlockSpec((tq, NSA_GROUP * LANE), lambda h, i: (i, h)),
                  pl.BlockSpec((None, nc, LANE), lambda h, i: (h, 0, 0)),
                  pl.BlockSpec((None, LANE, nc), lambda h, i: (h, 0, 0)),
                  pl.BlockSpec((nbp, nc), lambda h, i: (0, 0)),
                  pl.BlockSpec((NSA_GROUP, y0 + nc, tq), lambda h, i: (h, 0, 0))],
        out_specs=[pl.BlockSpec((tq, NSA_GROUP * LANE), lambda h, i: (i, h)),
                   pl.BlockSpec((n_halves, tq, LANE), lambda h, i: (0, i, h))],
        out_shape=[jax.ShapeDtypeStruct((s, NSA_HEADS * LANE), BF16),
                   jax.ShapeDtypeStruct((n_halves, s, NSA_KV_HEADS * LANE), BF16)],
        compiler_params=_cp(("parallel", "parallel")),
        name="nsa_cmp_topk",
    )(nqn, k_cmp, v_cmp_t, jnp.asarray(ov, BF16), strips)


def _nsa_combine_kernel(oc_ref, os_ref, ow_ref, g_ref, o_ref):
    gate = jax.nn.sigmoid(g_ref[...].astype(F32))
    for h in range(NSA_HEADS):
        sl = slice(h * LANE, (h + 1) * LANE)
        o = (gate[:, 3 * h:3 * h + 1] * oc_ref[:, sl].astype(F32)
             + gate[:, 3 * h + 1:3 * h + 2] * os_ref[:, sl].astype(F32)
             + gate[:, 3 * h + 2:3 * h + 3] * ow_ref[:, sl].astype(F32))
        o_ref[:, sl] = o.astype(o_ref.dtype)


def nsa_combine(o_c, o_s, o_w, proj, tm=512):
    s = o_c.shape[0]
    tm = min(tm, s)
    w = NSA_HEADS * LANE
    spec = pl.BlockSpec((tm, w), lambda i: (i, 0))
    return pl.pallas_call(
        _nsa_combine_kernel,
        grid=(s // tm,),
        in_specs=[spec, spec, spec, pl.BlockSpec((tm, LANE), lambda i: (i, C_NG // LANE))],
        out_specs=spec,
        out_shape=jax.ShapeDtypeStruct((s, w), BF16),
        compiler_params=_cp(("parallel",)),
        name="nsa_combine",
    )(o_c, o_s, o_w, proj)


def _xattn_kernel(q_ref, kv_ref, gq_ref, gk_ref, o_ref, *, scale):
    nk = XATTN_HEADS * LANE
    for h in range(XATTN_HEADS):
        sl = slice(h * LANE, (h + 1) * LANE)
        qn = (_group_norm(q_ref[:, sl].astype(F32), LANE) * (gq_ref[...] * scale)).astype(BF16)
        kn = (_group_norm(kv_ref[:, sl].astype(F32), LANE) * gk_ref[...]).astype(BF16)
        s = lax.dot_general(qn, kn, (((1,), (1,)), ((), ())), preferred_element_type=F32)
        m = jnp.max(s, axis=-1, keepdims=True)
        e = jnp.exp(s - m)
        p = e / jnp.sum(e, axis=-1, keepdims=True)
        v = kv_ref[:, nk + h * LANE:nk + (h + 1) * LANE]
        o_ref[:, sl] = jnp.dot(p.astype(v.dtype), v, preferred_element_type=F32).astype(o_ref.dtype)


def xattn_core(qx, kv, qk_norm, tq=512):
    s, w = qx.shape
    tq = min(tq, s)
    m = kv.shape[0]
    return pl.pallas_call(
        functools.partial(_xattn_kernel, scale=HEAD_DIM ** -0.5),
        grid=(s // tq,),
        in_specs=[pl.BlockSpec((tq, w), lambda i: (i, 0)),
                  pl.BlockSpec((m, 2 * w), lambda i: (0, 0)),
                  pl.BlockSpec((1, LANE), lambda i: (0, 0)),
                  pl.BlockSpec((1, LANE), lambda i: (0, 0))],
        out_specs=pl.BlockSpec((tq, w), lambda i: (i, 0)),
        out_shape=jax.ShapeDtypeStruct((s, w), BF16),
        compiler_params=_cp(("parallel",)),
        name="xattn_core",
    )(qx, kv, qk_norm[0].reshape(1, LANE).astype(F32), qk_norm[1].reshape(1, LANE).astype(F32))


def _pack_w_in(w):
    d = w.shape[0]
    z = lambda n: jnp.zeros((d, n), w.dtype)
    parts = [w[:, 0:896], w[:, 1408:1472], z(64), w[:, 896:1408], w[:, 1472:7488],
             w[:, 7488:7512], z(LANE - 3 * NSA_HEADS), w[:, 7512:10200], z(NP_COLS - 10368)]
    return jnp.concatenate(parts, axis=1).astype(BF16)


def _mla_mixer(proj, pos, l, p):
    s = proj.shape[0]
    qm = mla_q_proj(proj, pos, p["mla_q_norm"][l], p["mla_w_uq"][l], p["mla_nope_norm"][l, 0], p["mla_rope_norm"][l, 0])
    km, vm = mla_kv_proj(proj, pos, p["mla_kv_norm"][l], p["mla_w_ukv"][l], p["mla_nope_norm"][l, 1],
                         p["mla_rope_norm"][l, 1])
    ident = lambda n: n
    return attention(qm, km, vm, n_streams=MLA_HEADS, seq=s, out_cols=MLA_HEADS * MLA_V, qcol=ident, kcol=ident,
                     vcol=ident, ocol=ident, tq=ATT_ROWS, tk=ATT_TK, tkb=ATT_TKB, dq=2 * LANE, dk=2 * LANE,
                     name="mla_attention")


def bias_tables(rel_bias, s):
    tabs = {"dil": []}
    for g, (_, dil) in enumerate(DIL_PATTERNS):
        tq = min(DIL_TQ, s // dil)
        tabs["dil"].append(bias_strips(rel_bias[:, BIAS_DIL + g * DIL_HP:BIAS_DIL + (g + 1) * DIL_HP], tq, tq, tq, dil))
    far_zero = lambda t: t - t[REL_BUCKETS - 1:REL_BUCKETS]
    nsa = rel_bias[:, BIAS_NSA:BIAS_NSA + NSA_HEADS].astype(F32)
    tk, tkb = min(ATT_TK, s), min(ATT_TKB, s)
    tabs["nsa"] = bias_strips(far_zero(nsa), ATT_ROWS // NSA_GROUP, tk, tkb, 1)
    tabs["nsa_cmp"] = cmp_bias_strips(nsa, s)
    diff = rel_bias[:, BIAS_DIFF:BIAS_DIFF + DIFF_HEADS].astype(F32)
    tabs["diff"] = bias_strips(far_zero(diff), min(ATT_ROWS // 2, s), tk, tkb, 1)
    return tabs


def _dilated_mixer(proj, l, p, tabs):
    s = proj.shape[0]
    w = DIL_HP * LANE
    dqn = head_rmsnorm(proj, C_DQ, DIL_HEADS, p["dil_qk_norm"][l, 0], LANE, HEAD_DIM ** -0.5)
    dkn = head_rmsnorm(proj, C_DK, DIL_HEADS, p["dil_qk_norm"][l, 1], LANE, 1.0)
    outs, lses = [], []
    ident = lambda n: n
    for g, (window, dil) in enumerate(DIL_PATTERNS):
        n_sub = s // dil
        tq = min(DIL_TQ, n_sub)
        fold = lambda t, c0: t[:, c0 + g * w:c0 + (g + 1) * w].reshape(n_sub, dil * w)
        o, lse = attention(fold(dqn, 0), fold(dkn, 0), fold(proj, C_DV), n_streams=dil * DIL_HP, seq=n_sub,
                           out_cols=dil * w, qcol=ident, kcol=ident, vcol=ident, ocol=ident, tq=tq, tk=tq, dq=LANE,
                           dk=LANE, span=window // dil, bias=tabs["dil"][g], bcol=lambda n: n % DIL_HP,
                           want_lse=True, name=f"dil_attention_{g}")
        outs.append(o.reshape(s, w))
        lses.append(lse.reshape(s, w))
    return dil_combine(outs, lses)


def _nsa_mixer(proj, l, p, tabs):
    s = proj.shape[0]
    gains = p["nsa_qk_norm"][l]
    nqn = head_rmsnorm(proj, C_NQ, NSA_HEADS, gains[0], LANE, HEAD_DIM ** -0.5)
    kwn = head_rmsnorm(proj, C_NKW, NSA_KV_HEADS, gains[3], LANE, 1.0)
    ksel = nsa_sel_keys(proj, gains[2])
    nc = s // NSA_CMP_STRIDE

    def chunks(c0):
        t = proj[:, c0:c0 + NSA_KV_HEADS * LANE].reshape(nc, NSA_CMP_STRIDE, NSA_KV_HEADS, LANE)
        return jnp.transpose(t, (2, 0, 1, 3)).reshape(NSA_KV_HEADS, nc, NSA_CMP_STRIDE * LANE)

    k_cmp = nsa_compress(chunks(C_NKC), p["nsa_cmp_pe"][l, 0], p["nsa_cmp_w1"][l, 0], p["nsa_cmp_w2"][l, 0],
                         gains[1], norm=True, transpose=False)
    v_cmp_t = nsa_compress(chunks(C_NVC), p["nsa_cmp_pe"][l, 1], p["nsa_cmp_w1"][l, 1], p["nsa_cmp_w2"][l, 1],
                           gains[1], norm=False, transpose=True)
    o_c, sel = nsa_cmp_topk(nqn, k_cmp, v_cmp_t, tabs["nsa_cmp"], s)
    ident = lambda n: n
    common = dict(n_streams=NSA_KV_HEADS, seq=s, out_cols=NSA_HEADS * LANE, qcol=ident, ocol=ident,
                  tq=ATT_ROWS // NSA_GROUP, tk=ATT_TK, tkb=ATT_TKB, G=NSA_GROUP, dq=LANE, bias=tabs["nsa"], bcol=ident,
                  bias_far_zero=True)
    o_s = attention(nqn, ksel, proj, kcol=ident, vcol=lambda n: C_NVS // LANE + n, mode="nsa_sel", dk=2 * LANE,
                    sel=sel, name="nsa_sel_attention", **common)
    o_w = attention(nqn, kwn, proj, kcol=ident, vcol=lambda n: C_NVW // LANE + n, dk=LANE, span=NSA_WINDOW - 1,
                    name="nsa_win_attention", **common)
    return nsa_combine(o_c, o_s, o_w, proj)


def _diff_mixer(proj, l, p, tabs):
    s = proj.shape[0]
    fqn = head_rmsnorm(proj, C_FQ, DIFF_HEADS, p["diff_qk_norm"][l, 0], DIFF_QK, DIFF_QK ** -0.5)
    fkn = head_rmsnorm(proj, C_FK, DIFF_HEADS, p["diff_qk_norm"][l, 1], DIFF_QK, 1.0)
    lam_init = 0.8 - 0.6 * math.exp(-0.3 * l)
    lv = p["diff_lambda"][l].astype(F32)
    lam = jnp.exp(jnp.sum(lv[0] * lv[1])) - jnp.exp(jnp.sum(lv[2] * lv[3])) + lam_init
    ident = lambda n: n
    return attention(fqn, fkn, proj, n_streams=DIFF_HEADS, seq=s, out_cols=DIFF_HEADS * DIFF_V, qcol=ident,
                     kcol=ident, vcol=lambda n: C_FV // LANE + n, ocol=ident, mode="diff", tq=ATT_ROWS // 2,
                     tk=ATT_TK, tkb=ATT_TKB, dq=LANE, dk=LANE, bias=tabs["diff"], bcol=ident, bias_far_zero=True,
                     lam=jnp.full((1, LANE), lam, F32),
                     gout=p["diff_out_norm"][l].reshape(1, LANE).astype(F32), out_scale=1.0 - lam_init,
                     name="diff_attention")


def kernel(x, mem, positions, rel_bias, norm_mix, w_in, mla_q_norm, mla_kv_norm, mla_w_uq, mla_w_ukv, mla_nope_norm, mla_rope_norm, dil_qk_norm, nsa_qk_norm, nsa_cmp_pe, nsa_cmp_w1, nsa_cmp_w2, diff_qk_norm, diff_lambda, diff_out_norm, w_out, norm_xattn, norm_mem, xattn_wq, xattn_wkv, xattn_qk_norm, xattn_wo, norm_ffn, ffn_w_gate, ffn_w_up, ffn_conv_w, ffn_conv_b, ffn_w_down):
    p = dict(mla_q_norm=mla_q_norm, mla_kv_norm=mla_kv_norm, mla_w_uq=mla_w_uq, mla_w_ukv=mla_w_ukv,
             mla_nope_norm=mla_nope_norm, mla_rope_norm=mla_rope_norm, dil_qk_norm=dil_qk_norm,
             nsa_qk_norm=nsa_qk_norm, nsa_cmp_pe=nsa_cmp_pe, nsa_cmp_w1=nsa_cmp_w1, nsa_cmp_w2=nsa_cmp_w2,
             diff_qk_norm=diff_qk_norm, diff_lambda=diff_lambda, diff_out_norm=diff_out_norm)
    b, s, d = x.shape
    assert b == 1
    xs = x.reshape(s, d).astype(F32)
    mems = mem.reshape(mem.shape[1], d).astype(F32)
    pos = positions.reshape(s, 1).astype(jnp.int32)
    tabs = bias_tables(rel_bias, s)
    for l in range(DEPTH):
        h = rmsnorm_rows(xs, norm_mix[l])
        proj = matmul(h, _pack_w_in(w_in[l]), tm=512, tn=1536, out_dtype=BF16, name="in_proj")
        mix = jnp.concatenate([_mla_mixer(proj, pos, l, p), _dilated_mixer(proj, l, p, tabs),
                               _nsa_mixer(proj, l, p, tabs), _diff_mixer(proj, l, p, tabs)], axis=1)
        xs = matmul(mix, w_out[l].astype(BF16), tm=512, tn=1024, out_dtype=F32, residual=xs, name="out_proj")
        hx = rmsnorm_rows(xs, norm_xattn[l])
        qx = matmul(hx, xattn_wq[l].astype(BF16), tm=1024, tn=512, out_dtype=BF16, name="xattn_q")
        kvm = matmul(rmsnorm_rows(mems, norm_mem[l]), xattn_wkv[l].astype(BF16), tm=256, tn=1024, out_dtype=BF16,
                     name="xattn_kv")
        ox = xattn_core(qx, kvm, xattn_qk_norm[l])
        xs = matmul(ox, xattn_wo[l].astype(BF16), tm=1024, tn=1024, out_dtype=F32, residual=xs, name="xattn_o")
        hf = rmsnorm_rows(xs, norm_ffn[l])
        act = ffn_gate_up(hf, ffn_w_gate[l].astype(BF16), ffn_w_up[l].astype(BF16), ffn_conv_w[l], ffn_conv_b[l])
        xs = matmul(act, ffn_w_down[l].astype(BF16), tm=512, tn=512, out_dtype=F32, residual=xs, name="ffn_down")
    return xs.reshape(b, s, d)
```

```python
import functools
import math

import numpy as np
import jax
import jax.numpy as jnp
from jax import lax
from jax.experimental import pallas as pl
from jax.experimental.pallas import tpu as pltpu

F32 = jnp.float32
BF16 = jnp.bfloat16

D_MODEL = 4096
DEPTH = 2
HEAD_DIM = 128
EPS = 1e-6
MLA_HEADS = 8
MLA_Q_RANK = 896
MLA_KV_RANK = 512
MLA_NOPE = 128
MLA_ROPE = 64
MLA_V = 128
ROPE_THETA = 10000.0
DIL_PATTERNS = ((128, 1), (512, 4), (2048, 16))
DIL_HP = 3
DIL_HEADS = DIL_HP * len(DIL_PATTERNS)
NSA_HEADS = 8
NSA_KV_HEADS = 2
NSA_GROUP = NSA_HEADS // NSA_KV_HEADS
NSA_CMP_BLOCK = 32
NSA_CMP_STRIDE = 16
NSA_SLC_BLOCK = 64
NSA_TOPK = 16
NSA_WINDOW = 512
DIFF_HEADS = 7
DIFF_QK = 64
DIFF_V = 128
XATTN_HEADS = 4
FFN_DIM = 11008
REL_BUCKETS = 32
REL_MAX_DIST = 2048
BIAS_DIL = 0
BIAS_NSA = DIL_HEADS
BIAS_DIFF = DIL_HEADS + NSA_HEADS

LANE = 128
VMEM_LIMIT = 56 * 1024 * 1024
NEG = -1e30
SEL_NEG = -32768.0
SEL_HALF_BLOCKS = 128
ATT_ROWS = 512
BIG_ROWS = 1024
ATT_TK = 1024
ATT_TKB = 2048
DIL_TQ = 512

C_CQ = 0
C_KR = 896
C_CKV = 1024
C_DQ = 1536
C_DK = 2688
C_DV = 3840
C_NQ = 4992
C_NKC = 6016
C_NVC = 6272
C_NKS = 6528
C_NVS = 6784
C_NKW = 7040
C_NVW = 7296
C_NG = 7552
C_FQ = 7680
C_FK = 8576
C_FV = 9472
NP_COLS = 10752


def _cp(sem):
    return pltpu.CompilerParams(dimension_semantics=sem, vmem_limit_bytes=VMEM_LIMIT)


def _rmsnorm_rows_kernel(x_ref, g_ref, o_ref):
    x = x_ref[...].astype(F32)
    ms = jnp.mean(x * x, axis=-1, keepdims=True)
    o_ref[...] = (x * lax.rsqrt(ms + EPS) * g_ref[...]).astype(o_ref.dtype)


def rmsnorm_rows(x, g, tm=256):
    m, d = x.shape
    tm = min(tm, m)
    return pl.pallas_call(
        _rmsnorm_rows_kernel,
        grid=(m // tm,),
        in_specs=[pl.BlockSpec((tm, d), lambda i: (i, 0)),
                  pl.BlockSpec((1, d), lambda i: (0, 0))],
        out_specs=pl.BlockSpec((tm, d), lambda i: (i, 0)),
        out_shape=jax.ShapeDtypeStruct((m, d), BF16),
        compiler_params=_cp(("parallel",)),
        name="rmsnorm_rows",
    )(x, g.reshape(1, d).astype(F32))


def _matmul_kernel(*refs, has_res):
    if has_res:
        a_ref, w_ref, r_ref, o_ref = refs
    else:
        a_ref, w_ref, o_ref = refs
    acc = jnp.dot(a_ref[...], w_ref[...], preferred_element_type=F32)
    if has_res:
        acc = acc + r_ref[...]
    o_ref[...] = acc.astype(o_ref.dtype)


def matmul(a, w, *, tm, tn, out_dtype, residual=None, name="matmul"):
    m, k = a.shape
    n = w.shape[1]
    tm = min(tm, m)
    tn = min(tn, n)
    assert m % tm == 0 and n % tn == 0
    in_specs = [pl.BlockSpec((tm, k), lambda j, i: (i, 0)),
                pl.BlockSpec((k, tn), lambda j, i: (0, j))]
    args = [a, w]
    if residual is not None:
        in_specs.append(pl.BlockSpec((tm, tn), lambda j, i: (i, j)))
        args.append(residual)
    return pl.pallas_call(
        functools.partial(_matmul_kernel, has_res=residual is not None),
        grid=(n // tn, m // tm),
        in_specs=in_specs,
        out_specs=pl.BlockSpec((tm, tn), lambda j, i: (i, j)),
        out_shape=jax.ShapeDtypeStruct((m, n), out_dtype),
        compiler_params=_cp(("parallel", "parallel")),
        name=name,
    )(*args)


def _ffn_gate_up_kernel(h_ref, wg_ref, wu_ref, cw_ref, cb_ref, o_ref, carry_ref):
    i = pl.program_id(1)

    @pl.when(i == 0)
    def _():
        carry_ref[...] = jnp.zeros_like(carry_ref)

    h = h_ref[...]
    u = jnp.dot(h, wg_ref[...], preferred_element_type=F32)
    up = jnp.dot(h, wu_ref[...], preferred_element_type=F32)
    tm = u.shape[0]
    prev = carry_ref[...]
    row = lax.broadcasted_iota(jnp.int32, u.shape, 0)
    u1 = jnp.where(row == 0, prev[7:8, :], pltpu.roll(u, 1, 0))
    u2 = pltpu.roll(u, 2, 0)
    u2 = jnp.where(row == 0, prev[6:7, :], jnp.where(row == 1, prev[7:8, :], u2))
    cw = cw_ref[...]
    g = cw[0:1, :] * u2 + cw[1:2, :] * u1 + cw[2:3, :] * u + cb_ref[...]
    o_ref[...] = (g * jax.nn.sigmoid(g) * up).astype(o_ref.dtype)
    carry_ref[...] = u[tm - 8:, :]


def ffn_gate_up(h, wg, wu, conv_w, conv_b, tm=1024, tf=256):
    m, d = h.shape
    f = wg.shape[1]
    tm = min(tm, m)
    assert f % tf == 0 and m % tm == 0
    return pl.pallas_call(
        _ffn_gate_up_kernel,
        grid=(f // tf, m // tm),
        in_specs=[pl.BlockSpec((tm, d), lambda j, i: (i, 0)),
                  pl.BlockSpec((d, tf), lambda j, i: (0, j)),
                  pl.BlockSpec((d, tf), lambda j, i: (0, j)),
                  pl.BlockSpec((3, tf), lambda j, i: (0, j)),
                  pl.BlockSpec((1, tf), lambda j, i: (0, j))],
        out_specs=pl.BlockSpec((tm, tf), lambda j, i: (i, j)),
        out_shape=jax.ShapeDtypeStruct((m, f), BF16),
        scratch_shapes=[pltpu.VMEM((8, tf), F32)],
        compiler_params=_cp(("arbitrary", "arbitrary")),
        name="ffn_gate_up",
    )(h, wg, wu, conv_w.astype(F32), conv_b.reshape(1, f).astype(F32))


def _group_norm(x, d):
    sq = x * x
    tot = jnp.sum(sq, axis=-1, keepdims=True)
    if d == LANE:
        ms = tot / d
    else:
        lane = lax.broadcasted_iota(jnp.int32, x.shape, 1)
        lo = jnp.sum(jnp.where(lane < d, sq, 0.0), axis=-1, keepdims=True)
        ms = jnp.where(lane < d, lo, tot - lo) / d
    return x * lax.rsqrt(ms + EPS)


def _head_norm_kernel(x_ref, g_ref, o_ref, *, d, scale):
    x = x_ref[...].astype(F32)
    o_ref[...] = (_group_norm(x, d) * (g_ref[...] * scale)).astype(o_ref.dtype)


def head_rmsnorm(src, col0, ncb, gain, d, scale, tm=2048):
    s = src.shape[0]
    tm = min(tm, s)
    cb0 = col0 // LANE
    g = jnp.tile(gain.astype(F32), LANE // d).reshape(1, LANE)
    return pl.pallas_call(
        functools.partial(_head_norm_kernel, d=d, scale=scale),
        grid=(s // tm, ncb),
        in_specs=[pl.BlockSpec((tm, LANE), lambda i, c: (i, cb0 + c)),
                  pl.BlockSpec((1, LANE), lambda i, c: (0, 0))],
        out_specs=pl.BlockSpec((tm, LANE), lambda i, c: (i, c)),
        out_shape=jax.ShapeDtypeStruct((s, ncb * LANE), BF16),
        compiler_params=_cp(("parallel", "parallel")),
        name="head_rmsnorm",
    )(src, g)


def _nsa_ksel_kernel(x_ref, g_ref, o_ref):
    tm = x_ref.shape[0]
    x = x_ref[...].astype(F32)
    o_ref[:, :LANE] = (_group_norm(x, LANE) * g_ref[...]).astype(o_ref.dtype)
    row = pl.program_id(0) * tm + lax.broadcasted_iota(jnp.int32, (tm, LANE), 0)
    col = lax.broadcasted_iota(jnp.int32, (tm, LANE), 1)
    hot = ((row // NSA_SLC_BLOCK) % SEL_HALF_BLOCKS) == col
    o_ref[:, LANE:] = jnp.where(hot, 1.0, 0.0).astype(o_ref.dtype)


def nsa_sel_keys(proj, gain, tm=2048):
    s = proj.shape[0]
    tm = min(tm, s)
    cb0 = C_NKS // LANE
    return pl.pallas_call(
        _nsa_ksel_kernel,
        grid=(s // tm, NSA_KV_HEADS),
        in_specs=[pl.BlockSpec((tm, LANE), lambda i, c: (i, cb0 + c)),
                  pl.BlockSpec((1, LANE), lambda i, c: (0, 0))],
        out_specs=pl.BlockSpec((tm, 2 * LANE), lambda i, c: (i, c)),
        out_shape=jax.ShapeDtypeStruct((s, NSA_KV_HEADS * 2 * LANE), BF16),
        compiler_params=_cp(("parallel", "parallel")),
        name="nsa_sel_keys",
    )(proj, gain.reshape(1, LANE).astype(F32))


def _rope_tables(pos_ref, inv_ref, sign_ref):
    ang = pos_ref[...].astype(F32) * inv_ref[...]
    return jnp.cos(ang), jnp.sin(ang) * sign_ref[...]


def _rope_apply(x, cosv, sin_signed):
    lane = lax.broadcasted_iota(jnp.int32, x.shape, 1)
    half = MLA_ROPE // 2
    swapped = jnp.where(lane < half, pltpu.roll(x, LANE - half, 1), pltpu.roll(x, half, 1))
    return x * cosv + swapped * sin_signed


def _latent_norm(c_ref, g_ref):
    c = c_ref[...].astype(F32)
    ms = jnp.mean(c * c, axis=-1, keepdims=True)
    return (c * lax.rsqrt(ms + EPS) * g_ref[...]).astype(BF16)


def _mla_q_kernel(c_ref, pos_ref, gq_ref, w_ref, gn_ref, gr_ref, inv_ref, sign_ref, o_ref, *, scale):
    u = jnp.dot(_latent_norm(c_ref, gq_ref), w_ref[...], preferred_element_type=F32)
    cosv, sinv = _rope_tables(pos_ref, inv_ref, sign_ref)
    for h in range(MLA_HEADS):
        b = 2 * LANE * h
        nope = _group_norm(u[:, b:b + LANE], LANE) * (gn_ref[...] * scale)
        rp = u[:, b + LANE:b + 2 * LANE]
        rp = rp * lax.rsqrt(jnp.sum(rp * rp, axis=-1, keepdims=True) / MLA_ROPE + EPS) * gr_ref[...]
        rp = _rope_apply(rp, cosv, sinv) * scale
        o_ref[:, b:b + LANE] = nope.astype(o_ref.dtype)
        o_ref[:, b + LANE:b + 2 * LANE] = rp.astype(o_ref.dtype)


def _mla_kv_kernel(c_ref, kr_ref, pos_ref, gkv_ref, w_ref, gn_ref, gr_ref, inv_ref, sign_ref, k_ref, v_ref):
    kv = jnp.dot(_latent_norm(c_ref, gkv_ref), w_ref[...], preferred_element_type=F32)
    cosv, sinv = _rope_tables(pos_ref, inv_ref, sign_ref)
    kr = kr_ref[...].astype(F32)
    kr = kr * lax.rsqrt(jnp.sum(kr * kr, axis=-1, keepdims=True) / MLA_ROPE + EPS) * gr_ref[...]
    kr = _rope_apply(kr, cosv, sinv).astype(k_ref.dtype)
    nv = MLA_HEADS * LANE
    for h in range(MLA_HEADS):
        kn = _group_norm(kv[:, h * LANE:(h + 1) * LANE], LANE) * gn_ref[...]
        k_ref[:, 2 * LANE * h:2 * LANE * h + LANE] = kn.astype(k_ref.dtype)
        k_ref[:, 2 * LANE * h + LANE:2 * LANE * (h + 1)] = kr
    v_ref[...] = kv[:, nv:].astype(v_ref.dtype)


def _rope_consts():
    half = MLA_ROPE // 2
    inv = ROPE_THETA ** (-jnp.arange(half, dtype=F32) / half)
    inv = jnp.concatenate([inv, inv, jnp.zeros((LANE - MLA_ROPE,), F32)]).reshape(1, LANE)
    sign = np.zeros((1, LANE), np.float32)
    sign[0, :half] = -1.0
    sign[0, half:MLA_ROPE] = 1.0
    return inv, jnp.asarray(sign)


def _pad_lanes(v, n=LANE):
    return jnp.concatenate([v.astype(F32), jnp.zeros((n - v.shape[0],), F32)]).reshape(1, n)


def mla_q_proj(proj, pos, q_norm, w_uq, nope_gain, rope_gain, tm=512):
    s = proj.shape[0]
    tm = min(tm, s)
    inv, sign = _rope_consts()
    w = jnp.pad(w_uq.reshape(MLA_Q_RANK, MLA_HEADS, MLA_NOPE + MLA_ROPE),
                ((0, 0), (0, 0), (0, LANE - MLA_ROPE))).reshape(MLA_Q_RANK, MLA_HEADS * 2 * LANE).astype(BF16)
    wq = MLA_HEADS * 2 * LANE
    const = lambda i: (0, 0)
    return pl.pallas_call(
        functools.partial(_mla_q_kernel, scale=(MLA_NOPE + MLA_ROPE) ** -0.5),
        grid=(s // tm,),
        in_specs=[pl.BlockSpec((tm, MLA_Q_RANK), lambda i: (i, C_CQ // MLA_Q_RANK)),
                  pl.BlockSpec((tm, 1), lambda i: (i, 0)),
                  pl.BlockSpec((1, MLA_Q_RANK), const),
                  pl.BlockSpec((MLA_Q_RANK, wq), const),
                  pl.BlockSpec((1, LANE), const), pl.BlockSpec((1, LANE), const),
                  pl.BlockSpec((1, LANE), const), pl.BlockSpec((1, LANE), const)],
        out_specs=pl.BlockSpec((tm, wq), lambda i: (i, 0)),
        out_shape=jax.ShapeDtypeStruct((s, wq), BF16),
        compiler_params=_cp(("parallel",)),
        name="mla_q_proj",
    )(proj, pos, q_norm.reshape(1, -1).astype(F32), w, nope_gain.reshape(1, LANE).astype(F32),
      _pad_lanes(rope_gain), inv, sign)


def mla_kv_proj(proj, pos, kv_norm, w_ukv, nope_gain, rope_gain, tm=512):
    s = proj.shape[0]
    tm = min(tm, s)
    inv, sign = _rope_consts()
    w = jnp.transpose(w_ukv.reshape(MLA_KV_RANK, MLA_HEADS, 2, LANE), (0, 2, 1, 3))
    w = w.reshape(MLA_KV_RANK, 2 * MLA_HEADS * LANE).astype(BF16)
    wk = MLA_HEADS * 2 * LANE
    wv = MLA_HEADS * MLA_V
    const = lambda i: (0, 0)
    return pl.pallas_call(
        _mla_kv_kernel,
        grid=(s // tm,),
        in_specs=[pl.BlockSpec((tm, MLA_KV_RANK), lambda i: (i, C_CKV // MLA_KV_RANK)),
                  pl.BlockSpec((tm, LANE), lambda i: (i, C_KR // LANE)),
                  pl.BlockSpec((tm, 1), lambda i: (i, 0)),
                  pl.BlockSpec((1, MLA_KV_RANK), const),
                  pl.BlockSpec((MLA_KV_RANK, 2 * MLA_HEADS * LANE), const),
                  pl.BlockSpec((1, LANE), const), pl.BlockSpec((1, LANE), const),
                  pl.BlockSpec((1, LANE), const), pl.BlockSpec((1, LANE), const)],
        out_specs=[pl.BlockSpec((tm, wk), lambda i: (i, 0)),
                   pl.BlockSpec((tm, wv), lambda i: (i, 0))],
        out_shape=[jax.ShapeDtypeStruct((s, wk), BF16), jax.ShapeDtypeStruct((s, wv), BF16)],
        compiler_params=_cp(("parallel",)),
        name="mla_kv_proj",
    )(proj, proj, pos, kv_norm.reshape(1, -1).astype(F32), w, nope_gain.reshape(1, LANE).astype(F32),
      _pad_lanes(rope_gain), inv, sign)


def _rel_bucket_np(dist):
    n = np.maximum(dist, 0)
    exact = REL_BUCKETS // 2
    ratio = np.log(np.maximum(n, 1).astype(np.float32) / np.float32(exact)) / np.float32(math.log(REL_MAX_DIST / exact))
    large = np.minimum(exact + (ratio.astype(np.float32) * (REL_BUCKETS - exact)).astype(np.int32), REL_BUCKETS - 1)
    return np.where(n < exact, n, large).astype(np.int32)


def _strip_x0(tk):
    return -(-(REL_MAX_DIST - 1 + tk) // LANE) * LANE


def bias_strips(table, tq, tk, tkb, dist_scale):
    x0 = _strip_x0(tkb)
    width = x0 + max(tq, tk)
    period = -(-(width + tq) // LANE) * LANE
    t = np.arange(period)
    t = np.where(t < width, t, t - period)
    u = table.astype(F32)[_rel_bucket_np((x0 - t) * dist_scale)].T
    h = u.shape[0]
    return pl.pallas_call(
        functools.partial(_toeplitz_kernel, tq=tq, width=width),
        grid=(h,),
        in_specs=[pl.BlockSpec((None, 1, period), lambda n: (n, 0, 0))],
        out_specs=pl.BlockSpec((None, tq, width), lambda n: (n, 0, 0)),
        out_shape=jax.ShapeDtypeStruct((h, tq, width), F32),
        compiler_params=_cp(("parallel",)),
        name="bias_strips",
    )(u.reshape(h, 1, period))


def _toeplitz_kernel(u_ref, o_ref, *, tq, width):
    rows = jnp.broadcast_to(u_ref[...], (tq, u_ref.shape[1]))
    o_ref[...] = pltpu.roll(rows, 0, 1, stride=1, stride_axis=0)[:, :width]


def _attn_kernel(*refs, mode, tq, tk, tkb, G, dq, dv, span, has_bias, far_zero, x0, n_halves, want_lse, out_scale):
    refs = list(refs)
    q_ref = refs.pop(0)
    sel_ref = refs.pop(0) if mode == "nsa_sel" else None
    k_ref = refs.pop(0)
    v_ref = refs.pop(0)
    bias_ref = refs.pop(0) if has_bias else None
    lam_ref = gout_ref = None
    if mode == "diff":
        lam_ref = refs.pop(0)
        gout_ref = refs.pop(0)
    o_ref = refs.pop(0)
    lse_ref = refs.pop(0) if want_lse else None
    m_sc, l_sc, acc_sc = refs

    qi = pl.program_id(1)
    q0 = qi * tq
    n_tiles = k_ref.shape[0] // tk

    if mode == "diff":
        q = q_ref[...]
        lane = lax.broadcasted_iota(jnp.int32, q.shape, 1)
        zero = jnp.zeros_like(q)
        lhs = [jnp.concatenate([jnp.where(lane < DIFF_QK, q, zero), jnp.where(lane >= DIFF_QK, q, zero)], axis=0)]
    elif mode == "nsa_sel":
        lhs = []
        for hf in range(n_halves):
            sel = sel_ref[hf]
            lhs.append(jnp.concatenate(
                [jnp.concatenate([q_ref[:, g * dq:(g + 1) * dq], sel], axis=1) for g in range(G)], axis=0))
    else:
        lhs = [jnp.concatenate([q_ref[:, g * dq:(g + 1) * dq] for g in range(G)], axis=0) if G > 1 else q_ref[...]]

    m_sc[...] = jnp.full_like(m_sc, NEG)
    l_sc[...] = jnp.zeros_like(l_sc)
    acc_sc[...] = jnp.zeros_like(acc_sc)

    def logits(k0, w, Q, masked, biased):
        kt = k_ref[pl.ds(k0, w), :]
        s = lax.dot_general(Q, kt, (((1,), (1,)), ((), ())), preferred_element_type=F32)
        has_bias = biased
        if has_bias or masked:
            if masked:
                rel = (q0 - k0) + lax.broadcasted_iota(jnp.int32, (tq, w), 0) - lax.broadcasted_iota(jnp.int32, (tq, w), 1)
                vis = rel >= 0
                if span is not None:
                    vis = jnp.logical_and(vis, rel <= span)
            if has_bias:
                start = pl.multiple_of(jnp.maximum(x0 - (q0 - k0), 0), LANE)
            parts = []
            for g in range(G):
                sg = s[g * tq:(g + 1) * tq]
                if has_bias:
                    sg = sg + bias_ref[0 if mode == "diff" else g, :, pl.ds(start, w)]
                if masked:
                    sg = jnp.where(vis, sg, NEG)
                parts.append(sg)
            s = jnp.concatenate(parts, axis=0) if G > 1 else parts[0]
        return s

    def update(s, k0, w):
        vt = v_ref[pl.ds(k0, w), :]
        m_prev = m_sc[...]
        m_new = jnp.maximum(m_prev, jnp.max(s, axis=-1, keepdims=True))
        alpha = jnp.exp(m_prev - m_new)
        p = jnp.exp(s - jnp.tile(m_new, (1, w // LANE)))
        l_sc[...] = alpha * l_sc[...] + jnp.sum(p, axis=-1, keepdims=True)
        acc_sc[...] = alpha * acc_sc[...] + jnp.dot(p.astype(vt.dtype), vt, preferred_element_type=F32)
        m_sc[...] = m_new

    def tile(k0, w, Q, masked, biased):
        if w == tkb and tkb > tk:
            h = w // 2
            sa = logits(k0, h, Q, masked, biased)
            sb = logits(k0 + h, h, Q, masked, biased)
            update(sa, k0, h)
            update(sb, k0 + h, h)
        else:
            update(logits(k0, w, Q, masked, biased), k0, w)

    def run(lo, hi, w, Q, masked, biased=has_bias):
        def body(j, c):
            tile(pl.multiple_of(j * w, w), w, Q, masked, biased)
            return c
        lax.fori_loop(lo, hi, body, 0)

    def run_visible(lo, hi, Q):
        if tkb == tk:
            run(lo, hi, tk, Q, False)
            return
        r = tkb // tk
        up = jnp.minimum(-(-lo // r) * r, hi)
        dn = jnp.maximum((hi // r) * r, up)
        if not (isinstance(lo, int) and lo % r == 0):
            run(lo, up, tk, Q, False)
        if has_bias and far_zero:
            far = jnp.clip(jnp.maximum(q0 - (REL_MAX_DIST - 1), 0) // tkb, up // r, dn // r)
            run(up // r, far, tkb, Q, False, False)
            run(far, dn // r, tkb, Q, False)
        else:
            run(up // r, dn // r, tkb, Q, False)
        run(dn, hi, tk, Q, False)

    j_hi = jnp.minimum((q0 + tq - 1) // tk, n_tiles - 1)
    b = (q0 + 1) // tk
    if span is None:
        j_lo = 0
        a = 0
    else:
        j_lo = jnp.maximum(q0 - span, 0) // tk
        a = jnp.maximum(q0 + tq - 1 - span + tk - 1, 0) // tk
        a = jnp.minimum(jnp.maximum(a, j_lo), b)
    tph = (SEL_HALF_BLOCKS * NSA_SLC_BLOCK) // tk
    for hf in range(n_halves):
        Q = lhs[hf]
        if mode == "nsa_sel":
            clip = lambda lo, hi: (jnp.maximum(lo, hf * tph) if hf else lo, jnp.minimum(hi, (hf + 1) * tph))
        else:
            clip = lambda lo, hi: (lo, hi)
        if span is not None:
            run(*clip(j_lo, a), tk, Q, True)
        run_visible(*clip(a, b), Q)
        run(*clip(b, j_hi + 1), tk, Q, True)

    l = l_sc[...]
    acc = acc_sc[...] / l
    if mode == "diff":
        o = acc[:tq] - lam_ref[...] * acc[tq:]
        o = o * lax.rsqrt(jnp.mean(o * o, axis=-1, keepdims=True) + EPS) * (gout_ref[...] * out_scale)
        o_ref[...] = o.astype(o_ref.dtype)
    else:
        for g in range(G):
            o_ref[:, g * dv:(g + 1) * dv] = acc[g * tq:(g + 1) * tq].astype(o_ref.dtype)
        if want_lse:
            lse = m_sc[...] + jnp.log(l)
            for g in range(G):
                lse_ref[:, g * LANE:(g + 1) * LANE] = lse[g * tq:(g + 1) * tq]


def attention(q, k, v, *, n_streams, seq, out_cols, qcol, kcol, vcol, ocol, mode="plain", tq, tk, tkb=None, G=1,
              dq, dk, dv=LANE, span=None, bias=None, bcol=None, sel=None, lam=None, gout=None, out_scale=1.0,
              want_lse=False, bias_far_zero=False, name="attention"):
    tq = min(tq, seq)
    tk = min(tk, seq)
    tkb = tk if tkb is None else min(tkb, seq)
    assert seq % tq == 0 and seq % tkb == 0 and tkb % tk == 0 and tq % LANE == 0 and tk % LANE == 0
    resident = dict(pipeline_mode=pl.Buffered(1))
    R = 2 * tq if mode == "diff" else G * tq
    Gk = 2 if mode == "diff" else G
    wq = dq if mode == "diff" else G * dq
    in_specs = [pl.BlockSpec((tq, wq), lambda n, i: (i, qcol(n)))]
    args = [q]
    n_halves = 1
    if mode == "nsa_sel":
        n_halves = sel.shape[0]
        in_specs.append(pl.BlockSpec((n_halves, tq, LANE), lambda n, i: (0, i, n)))
        args.append(sel)
    in_specs += [pl.BlockSpec((seq, dk), lambda n, i: (0, kcol(n)), **resident),
                 pl.BlockSpec((seq, dv), lambda n, i: (0, vcol(n)), **resident)]
    args += [k, v]
    x0 = 0
    if bias is not None:
        x0 = _strip_x0(tkb)
        gb = 1 if mode == "diff" else G
        in_specs.append(pl.BlockSpec((gb, tq, x0 + max(tq, tk)), lambda n, i: (bcol(n), 0, 0), **resident))
        args.append(bias)
    if mode == "diff":
        in_specs += [pl.BlockSpec((1, LANE), lambda n, i: (0, 0)), pl.BlockSpec((1, LANE), lambda n, i: (0, 0))]
        args += [lam, gout]
    wo = dv if mode == "diff" else G * dv
    out_specs = [pl.BlockSpec((tq, wo), lambda n, i: (i, ocol(n)))]
    out_shape = [jax.ShapeDtypeStruct((seq, out_cols), BF16)]
    if want_lse:
        out_specs.append(pl.BlockSpec((tq, G * LANE), lambda n, i: (i, ocol(n))))
        out_shape.append(jax.ShapeDtypeStruct((seq, out_cols), F32))
    kern = functools.partial(_attn_kernel, mode=mode, tq=tq, tk=tk, tkb=tkb, G=Gk, dq=dq, dv=dv, span=span,
                             has_bias=bias is not None, far_zero=bias_far_zero, x0=x0, n_halves=n_halves,
                             want_lse=want_lse,
                             out_scale=out_scale)
    res = pl.pallas_call(
        kern,
        grid=(n_streams, seq // tq),
        in_specs=in_specs,
        out_specs=out_specs,
        out_shape=out_shape,
        scratch_shapes=[pltpu.VMEM((R, LANE), F32), pltpu.VMEM((R, LANE), F32), pltpu.VMEM((R, dv), F32)],
        compiler_params=_cp(("parallel", "parallel")),
        name=name,
    )(*args)
    return res if want_lse else res[0]


def _dil_combine_kernel(o0, o1, o2, l0, l1, l2, out_ref):
    os_ = (o0, o1, o2)
    ls_ = (l0, l1, l2)
    for j in range(DIL_HP):
        sl = slice(j * LANE, (j + 1) * LANE)
        lse = [r[:, sl] for r in ls_]
        m = jnp.maximum(jnp.maximum(lse[0], lse[1]), lse[2])
        e = [jnp.exp(x - m) for x in lse]
        inv = 1.0 / (e[0] + e[1] + e[2])
        for g in range(len(DIL_PATTERNS)):
            c = (g * DIL_HP + j) * LANE
            out_ref[:, c:c + LANE] = (os_[g][:, sl].astype(F32) * (e[g] * inv)).astype(out_ref.dtype)


def dil_combine(outs, lses, tm=512):
    s = outs[0].shape[0]
    tm = min(tm, s)
    w = DIL_HP * LANE
    spec = pl.BlockSpec((tm, w), lambda i: (i, 0))
    return pl.pallas_call(
        _dil_combine_kernel,
        grid=(s // tm,),
        in_specs=[spec] * 6,
        out_specs=pl.BlockSpec((tm, DIL_HEADS * LANE), lambda i: (i, 0)),
        out_shape=jax.ShapeDtypeStruct((s, DIL_HEADS * LANE), BF16),
        compiler_params=_cp(("parallel",)),
        name="dil_combine",
    )(*outs, *lses)


def _gelu_tanh(y):
    return 0.5 * y * (1.0 + jnp.tanh(0.7978845608028654 * (y + 0.044715 * y * y * y)))


def _compress_kernel(a_ref, pe_ref, w1_ref, w2_ref, g_ref, o_ref, *, norm, transpose):
    a = a_ref[...].astype(F32)
    nc = a.shape[0]
    lo = jnp.dot((a + pe_ref[0]).astype(BF16), w1_ref[0], preferred_element_type=F32)
    hi = jnp.dot((a + pe_ref[1]).astype(BF16), w1_ref[1], preferred_element_type=F32)
    y = lo + pltpu.roll(hi, nc - 1, 0)
    z = jnp.dot(_gelu_tanh(y).astype(BF16), w2_ref[...], preferred_element_type=F32)
    if norm:
        z = _group_norm(z, LANE) * g_ref[...]
    o_ref[...] = (z.T if transpose else z).astype(o_ref.dtype)


def nsa_compress(chunks, pe, w1, w2, gain, *, norm, transpose):
    kvh, nc, w = chunks.shape
    half = NSA_CMP_STRIDE * LANE
    pe2 = pe.astype(F32).reshape(2, 1, half)
    w1s = w1.reshape(2, half, LANE).astype(BF16)
    oshape = (kvh, LANE, nc) if transpose else (kvh, nc, LANE)
    oblock = (None, LANE, nc) if transpose else (None, nc, LANE)
    return pl.pallas_call(
        functools.partial(_compress_kernel, norm=norm, transpose=transpose),
        grid=(kvh,),
        in_specs=[pl.BlockSpec((None, nc, w), lambda h: (h, 0, 0)),
                  pl.BlockSpec((2, 1, half), lambda h: (0, 0, 0)),
                  pl.BlockSpec((2, half, LANE), lambda h: (0, 0, 0)),
                  pl.BlockSpec((LANE, LANE), lambda h: (0, 0)),
                  pl.BlockSpec((1, LANE), lambda h: (0, 0))],
        out_specs=pl.BlockSpec(oblock, lambda h: (h, 0, 0)),
        out_shape=jax.ShapeDtypeStruct(oshape, BF16),
        compiler_params=_cp(("parallel",)),
        name="nsa_compress",
    )(chunks, pe2, w1s, w2.astype(BF16), gain.reshape(1, LANE).astype(F32))


def _cmp_topk_kernel(q_ref, kc_ref, vt_ref, ov_ref, strip_ref, o_ref, sel_ref, *, tq, y0, k_sel, n_halves):
    qi = pl.program_id(1)
    q0 = qi * tq
    G = NSA_GROUP
    nc = kc_ref.shape[0]
    qs = jnp.concatenate([q_ref[:, g * LANE:(g + 1) * LANE] for g in range(G)], axis=0)
    st = lax.dot_general(kc_ref[...], qs, (((1,), (1,)), ((), ())), preferred_element_type=F32)
    n_io = lax.broadcasted_iota(jnp.int32, (nc, tq), 0)
    i_io = lax.broadcasted_iota(jnp.int32, (nc, tq), 1)
    vis = (q0 + i_io - NSA_CMP_STRIDE * n_io - (NSA_CMP_BLOCK - 1)) >= 0
    start = pl.multiple_of(y0 - qi * (tq // NSA_CMP_STRIDE), 8)
    psum = jnp.zeros((nc, tq), F32)
    pts = []
    for g in range(G):
        s = st[:, g * tq:(g + 1) * tq] + strip_ref[g, pl.ds(start, nc), :]
        s = jnp.where(vis, s, NEG)
        m = jnp.max(s, axis=0, keepdims=True)
        e = jnp.where(vis, jnp.exp(s - m), 0.0)
        ssum = jnp.sum(e, axis=0, keepdims=True)
        p = e / jnp.where(ssum > 0, ssum, 1.0)
        psum = psum + p
        pts.append(p.astype(BF16))
    pt = jnp.concatenate(pts, axis=1)
    ot = jnp.dot(vt_ref[...], pt, preferred_element_type=F32)
    for g in range(G):
        o_ref[:, g * LANE:(g + 1) * LANE] = ot[:, g * tq:(g + 1) * tq].T.astype(o_ref.dtype)
    p_hi = psum.astype(BF16)
    p_lo = (psum - p_hi.astype(F32)).astype(BF16)
    imp = (jnp.dot(ov_ref[...], p_hi, preferred_element_type=F32)
           + jnp.dot(ov_ref[...], p_lo, preferred_element_type=F32))
    nb = imp.shape[0]
    b_io = lax.broadcasted_iota(jnp.int32, (nb, tq), 0)
    qblk = (q0 + lax.broadcasted_iota(jnp.int32, (nb, tq), 1)) // NSA_SLC_BLOCK
    forced = (b_io == 0) | (b_io == qblk) | (b_io == qblk - 1)
    val = jnp.where(forced, 3e38, jnp.where(b_io <= qblk, imp, -1.0))
    seln = jnp.full((nb, tq), SEL_NEG, F32)
    for _ in range(k_sel):
        mx = jnp.max(val, axis=0, keepdims=True)
        idx = jnp.min(jnp.where(val == mx, b_io, nb), axis=0, keepdims=True)
        pick = (b_io == idx) & (mx >= 0.0)
        seln = jnp.where(pick, 0.0, seln)
        val = jnp.where(pick, -2.0, val)
    sel = seln.T
    for hf in range(n_halves):
        sel_ref[hf] = sel[:, hf * LANE:(hf + 1) * LANE].astype(sel_ref.dtype)


def _cmp_y0(s):
    return (LANE // NSA_CMP_STRIDE) * (s // LANE - 1)


def cmp_bias_strips(table, s):
    nc = s // NSA_CMP_STRIDE
    y0 = _cmp_y0(s)
    far_rows = -(-(REL_MAX_DIST + NSA_CMP_BLOCK) // NSA_CMP_STRIDE)
    ya = max(y0 - far_rows, 0)
    yb = min(y0 + LANE // NSA_CMP_STRIDE, y0 + nc)
    y = np.arange(ya, yb)[:, None]
    i = np.arange(LANE)[None, :]
    band = jnp.transpose(table.astype(F32)[_rel_bucket_np(NSA_CMP_STRIDE * (y0 - y) + i - (NSA_CMP_BLOCK - 1))], (2, 0, 1))
    h = table.shape[1]
    far = jnp.broadcast_to(table.astype(F32)[REL_BUCKETS - 1][:, None, None], (h, ya, LANE))
    return jnp.concatenate([far, band, jnp.zeros((h, y0 + nc - yb, LANE), F32)], axis=1)


def nsa_cmp_topk(nqn, k_cmp, v_cmp_t, strips, s):
    tq = LANE
    nc = s // NSA_CMP_STRIDE
    n_cmp = nc - NSA_CMP_BLOCK // NSA_CMP_STRIDE + 1
    n_slc = s // NSA_SLC_BLOCK
    nbp = -(-n_slc // LANE) * LANE
    n_halves = nbp // LANE
    cs = np.arange(nc) * NSA_CMP_STRIDE
    ce = cs + NSA_CMP_BLOCK - 1
    ss = np.arange(nbp) * NSA_SLC_BLOCK
    ov = ((cs[None, :] < ss[:, None] + NSA_SLC_BLOCK) & (ce[None, :] >= ss[:, None])).astype(np.float32)
    ov[:, n_cmp:] = 0.0
    ov[n_slc:, :] = 0.0
    y0 = _cmp_y0(s)
    k_sel = min(NSA_TOPK, n_slc)
    return pl.pallas_call(
        functools.partial(_cmp_topk_kernel, tq=tq, y0=y0, k_sel=k_sel, n_halves=n_halves),
        grid=(NSA_KV_HEADS, s // tq),
        in_specs=[pl.BlockSpec((tq, NSA_GROUP * LANE), lambda h, i: (i, h)),
                  pl.BlockSpec((None, nc, LANE), lambda h, i: (h, 0, 0)),
                  pl.BlockSpec((None, LANE, nc), lambda h, i: (h, 0, 0)),
                  pl.BlockSpec((nbp, nc), lambda h, i: (0, 0)),
                  pl.BlockSpec((NSA_GROUP, y0 + nc, tq), lambda h, i: (h, 0, 0))],
        out_specs=[pl.BlockSpec((tq, NSA_GROUP * LANE), lambda h, i: (i, h)),
                   pl.BlockSpec((n_halves, tq, LANE), lambda h, i: (0, i, h))],
        out_shape=[jax.ShapeDtypeStruct((s, NSA_HEADS * LANE), BF16),
                   jax.ShapeDtypeStruct((n_halves, s, NSA_KV_HEADS * LANE), BF16)],
        compiler_params=_cp(("parallel", "parallel")),
        name="nsa_cmp_topk",
    )(nqn, k_cmp, v_cmp_t, jnp.asarray(ov, BF16), strips)


def _nsa_combine_kernel(oc_ref, os_ref, ow_ref, g_ref, o_ref):
    gate = jax.nn.sigmoid(g_ref[...].astype(F32))
    for h in range(NSA_HEADS):
        sl = slice(h * LANE, (h + 1) * LANE)
        o = (gate[:, 3 * h:3 * h + 1] * oc_ref[:, sl].astype(F32)
             + gate[:, 3 * h + 1:3 * h + 2] * os_ref[:, sl].astype(F32)
             + gate[:, 3 * h + 2:3 * h + 3] * ow_ref[:, sl].astype(F32))
        o_ref[:, sl] = o.astype(o_ref.dtype)


def nsa_combine(o_c, o_s, o_w, proj, tm=512):
    s = o_c.shape[0]
    tm = min(tm, s)
    w = NSA_HEADS * LANE
    spec = pl.BlockSpec((tm, w), lambda i: (i, 0))
    return pl.pallas_call(
        _nsa_combine_kernel,
        grid=(s // tm,),
        in_specs=[spec, spec, spec, pl.BlockSpec((tm, LANE), lambda i: (i, C_NG // LANE))],
        out_specs=spec,
        out_shape=jax.ShapeDtypeStruct((s, w), BF16),
        compiler_params=_cp(("parallel",)),
        name="nsa_combine",
    )(o_c, o_s, o_w, proj)


def _xattn_kernel(q_ref, kv_ref, gq_ref, gk_ref, o_ref, *, scale):
    nk = XATTN_HEADS * LANE
    for h in range(XATTN_HEADS):
        sl = slice(h * LANE, (h + 1) * LANE)
        qn = (_group_norm(q_ref[:, sl].astype(F32), LANE) * (gq_ref[...] * scale)).astype(BF16)
        kn = (_group_norm(kv_ref[:, sl].astype(F32), LANE) * gk_ref[...]).astype(BF16)
        s = lax.dot_general(qn, kn, (((1,), (1,)), ((), ())), preferred_element_type=F32)
        m = jnp.max(s, axis=-1, keepdims=True)
        e = jnp.exp(s - m)
        p = e / jnp.sum(e, axis=-1, keepdims=True)
        v = kv_ref[:, nk + h * LANE:nk + (h + 1) * LANE]
        o_ref[:, sl] = jnp.dot(p.astype(v.dtype), v, preferred_element_type=F32).astype(o_ref.dtype)


def xattn_core(qx, kv, qk_norm, tq=512):
    s, w = qx.shape
    tq = min(tq, s)
    m = kv.shape[0]
    return pl.pallas_call(
        functools.partial(_xattn_kernel, scale=HEAD_DIM ** -0.5),
        grid=(s // tq,),
        in_specs=[pl.BlockSpec((tq, w), lambda i: (i, 0)),
                  pl.BlockSpec((m, 2 * w), lambda i: (0, 0)),
                  pl.BlockSpec((1, LANE), lambda i: (0, 0)),
                  pl.BlockSpec((1, LANE), lambda i: (0, 0))],
        out_specs=pl.BlockSpec((tq, w), lambda i: (i, 0)),
        out_shape=jax.ShapeDtypeStruct((s, w), BF16),
        compiler_params=_cp(("parallel",)),
        name="xattn_core",
    )(qx, kv, qk_norm[0].reshape(1, LANE).astype(F32), qk_norm[1].reshape(1, LANE).astype(F32))


def _pack_w_in(w):
    d = w.shape[0]
    z = lambda n: jnp.zeros((d, n), w.dtype)
    parts = [w[:, 0:896], w[:, 1408:1472], z(64), w[:, 896:1408], w[:, 1472:7488],
             w[:, 7488:7512], z(LANE - 3 * NSA_HEADS), w[:, 7512:10200], z(NP_COLS - 10368)]
    return jnp.concatenate(parts, axis=1).astype(BF16)


def _mla_mixer(proj, pos, l, p):
    s = proj.shape[0]
    qm = mla_q_proj(proj, pos, p["mla_q_norm"][l], p["mla_w_uq"][l], p["mla_nope_norm"][l, 0], p["mla_rope_norm"][l, 0])
    km, vm = mla_kv_proj(proj, pos, p["mla_kv_norm"][l], p["mla_w_ukv"][l], p["mla_nope_norm"][l, 1],
                         p["mla_rope_norm"][l, 1])
    ident = lambda n: n
    return attention(qm, km, vm, n_streams=MLA_HEADS, seq=s, out_cols=MLA_HEADS * MLA_V, qcol=ident, kcol=ident,
                     vcol=ident, ocol=ident, tq=BIG_ROWS, tk=ATT_TK, tkb=ATT_TKB, dq=2 * LANE, dk=2 * LANE,
                     name="mla_attention")


def bias_tables(rel_bias, s):
    tabs = {"dil": []}
    for g, (_, dil) in enumerate(DIL_PATTERNS):
        tq = min(DIL_TQ, s // dil)
        tabs["dil"].append(bias_strips(rel_bias[:, BIAS_DIL + g * DIL_HP:BIAS_DIL + (g + 1) * DIL_HP], tq, tq, tq, dil))
    far_zero = lambda t: t - t[REL_BUCKETS - 1:REL_BUCKETS]
    nsa = rel_bias[:, BIAS_NSA:BIAS_NSA + NSA_HEADS].astype(F32)
    tk, tkb = min(ATT_TK, s), min(ATT_TKB, s)
    tabs["nsa"] = bias_strips(far_zero(nsa), ATT_ROWS // NSA_GROUP, tk, tkb, 1)
    tabs["nsa_cmp"] = cmp_bias_strips(nsa, s)
    diff = rel_bias[:, BIAS_DIFF:BIAS_DIFF + DIFF_HEADS].astype(F32)
    tabs["diff"] = bias_strips(far_zero(diff), min(BIG_ROWS // 2, s), tk, tkb, 1)
    return tabs


def _dilated_mixer(proj, l, p, tabs):
    s = proj.shape[0]
    w = DIL_HP * LANE
    dqn = head_rmsnorm(proj, C_DQ, DIL_HEADS, p["dil_qk_norm"][l, 0], LANE, HEAD_DIM ** -0.5)
    dkn = head_rmsnorm(proj, C_DK, DIL_HEADS, p["dil_qk_norm"][l, 1], LANE, 1.0)
    outs, lses = [], []
    ident = lambda n: n
    for g, (window, dil) in enumerate(DIL_PATTERNS):
        n_sub = s // dil
        tq = min(DIL_TQ, n_sub)
        fold = lambda t, c0: t[:, c0 + g * w:c0 + (g + 1) * w].reshape(n_sub, dil * w)
        o, lse = attention(fold(dqn, 0), fold(dkn, 0), fold(proj, C_DV), n_streams=dil * DIL_HP, seq=n_sub,
                           out_cols=dil * w, qcol=ident, kcol=ident, vcol=ident, ocol=ident, tq=tq, tk=tq, dq=LANE,
                           dk=LANE, span=window // dil, bias=tabs["dil"][g], bcol=lambda n: n % DIL_HP,
                           want_lse=True, name=f"dil_attention_{g}")
        outs.append(o.reshape(s, w))
        lses.append(lse.reshape(s, w))
    return dil_combine(outs, lses)


def _nsa_mixer(proj, l, p, tabs):
    s = proj.shape[0]
    gains = p["nsa_qk_norm"][l]
    nqn = head_rmsnorm(proj, C_NQ, NSA_HEADS, gains[0], LANE, HEAD_DIM ** -0.5)
    kwn = head_rmsnorm(proj, C_NKW, NSA_KV_HEADS, gains[3], LANE, 1.0)
    ksel = nsa_sel_keys(proj, gains[2])
    nc = s // NSA_CMP_STRIDE

    def chunks(c0):
        t = proj[:, c0:c0 + NSA_KV_HEADS * LANE].reshape(nc, NSA_CMP_STRIDE, NSA_KV_HEADS, LANE)
        return jnp.transpose(t, (2, 0, 1, 3)).reshape(NSA_KV_HEADS, nc, NSA_CMP_STRIDE * LANE)

    k_cmp = nsa_compress(chunks(C_NKC), p["nsa_cmp_pe"][l, 0], p["nsa_cmp_w1"][l, 0], p["nsa_cmp_w2"][l, 0],
                         gains[1], norm=True, transpose=False)
    v_cmp_t = nsa_compress(chunks(C_NVC), p["nsa_cmp_pe"][l, 1], p["nsa_cmp_w1"][l, 1], p["nsa_cmp_w2"][l, 1],
                           gains[1], norm=False, transpose=True)
    o_c, sel = nsa_cmp_topk(nqn, k_cmp, v_cmp_t, tabs["nsa_cmp"], s)
    ident = lambda n: n
    common = dict(n_streams=NSA_KV_HEADS, seq=s, out_cols=NSA_HEADS * LANE, qcol=ident, ocol=ident,
                  tq=ATT_ROWS // NSA_GROUP, tk=ATT_TK, tkb=ATT_TKB, G=NSA_GROUP, dq=LANE, bias=tabs["nsa"], bcol=ident,
                  bias_far_zero=True)
    o_s = attention(nqn, ksel, proj, kcol=ident, vcol=lambda n: C_NVS // LANE + n, mode="nsa_sel", dk=2 * LANE,
                    sel=sel, name="nsa_sel_attention", **common)
    o_w = attention(nqn, kwn, proj, kcol=ident, vcol=lambda n: C_NVW // LANE + n, dk=LANE, span=NSA_WINDOW - 1,
                    name="nsa_win_attention", **common)
    return nsa_combine(o_c, o_s, o_w, proj)


def _diff_mixer(proj, l, p, tabs):
    s = proj.shape[0]
    fqn = head_rmsnorm(proj, C_FQ, DIFF_HEADS, p["diff_qk_norm"][l, 0], DIFF_QK, DIFF_QK ** -0.5)
    fkn = head_rmsnorm(proj, C_FK, DIFF_HEADS, p["diff_qk_norm"][l, 1], DIFF_QK, 1.0)
    lam_init = 0.8 - 0.6 * math.exp(-0.3 * l)
    lv = p["diff_lambda"][l].astype(F32)
    lam = jnp.exp(jnp.sum(lv[0] * lv[1])) - jnp.exp(jnp.sum(lv[2] * lv[3])) + lam_init
    ident = lambda n: n
    return attention(fqn, fkn, proj, n_streams=DIFF_HEADS, seq=s, out_cols=DIFF_HEADS * DIFF_V, qcol=ident,
                     kcol=ident, vcol=lambda n: C_FV // LANE + n, ocol=ident, mode="diff", tq=BIG_ROWS // 2,
                     tk=ATT_TK, tkb=ATT_TKB, dq=LANE, dk=LANE, bias=tabs["diff"], bcol=ident, bias_far_zero=True,
                     lam=jnp.full((1, LANE), lam, F32),
                     gout=p["diff_out_norm"][l].reshape(1, LANE).astype(F32), out_scale=1.0 - lam_init,
                     name="diff_attention")


def kernel(x, mem, positions, rel_bias, norm_mix, w_in, mla_q_norm, mla_kv_norm, mla_w_uq, mla_w_ukv, mla_nope_norm, mla_rope_norm, dil_qk_norm, nsa_qk_norm, nsa_cmp_pe, nsa_cmp_w1, nsa_cmp_w2, diff_qk_norm, diff_lambda, diff_out_norm, w_out, norm_xattn, norm_mem, xattn_wq, xattn_wkv, xattn_qk_norm, xattn_wo, norm_ffn, ffn_w_gate, ffn_w_up, ffn_conv_w, ffn_conv_b, ffn_w_down):
    p = dict(mla_q_norm=mla_q_norm, mla_kv_norm=mla_kv_norm, mla_w_uq=mla_w_uq, mla_w_ukv=mla_w_ukv,
             mla_nope_norm=mla_nope_norm, mla_rope_norm=mla_rope_norm, dil_qk_norm=dil_qk_norm,
             nsa_qk_norm=nsa_qk_norm, nsa_cmp_pe=nsa_cmp_pe, nsa_cmp_w1=nsa_cmp_w1, nsa_cmp_w2=nsa_cmp_w2,
             diff_qk_norm=diff_qk_norm, diff_lambda=diff_lambda, diff_out_norm=diff_out_norm)
    b, s, d = x.shape
    assert b == 1
    xs = x.reshape(s, d).astype(F32)
    mems = mem.reshape(mem.shape[1], d).astype(F32)
    pos = positions.reshape(s, 1).astype(jnp.int32)
    tabs = bias_tables(rel_bias, s)
    for l in range(DEPTH):
        h = rmsnorm_rows(xs, norm_mix[l])
        proj = matmul(h, _pack_w_in(w_in[l]), tm=512, tn=1536, out_dtype=BF16, name="in_proj")
        mix = jnp.concatenate([_mla_mixer(proj, pos, l, p), _dilated_mixer(proj, l, p, tabs),
                               _nsa_mixer(proj, l, p, tabs), _diff_mixer(proj, l, p, tabs)], axis=1)
        xs = matmul(mix, w_out[l].astype(BF16), tm=512, tn=1024, out_dtype=F32, residual=xs, name="out_proj")
        hx = rmsnorm_rows(xs, norm_xattn[l])
        qx = matmul(hx, xattn_wq[l].astype(BF16), tm=1024, tn=512, out_dtype=BF16, name="xattn_q")
        kvm = matmul(rmsnorm_rows(mems, norm_mem[l]), xattn_wkv[l].astype(BF16), tm=256, tn=1024, out_dtype=BF16,
                     name="xattn_kv")
        ox = xattn_core(qx, kvm, xattn_qk_norm[l])
        xs = matmul(ox, xattn_wo[l].astype(BF16), tm=1024, tn=1024, out_dtype=F32, residual=xs, name="xattn_o")
        hf = rmsnorm_rows(xs, norm_ffn[l])
        act = ffn_gate_up(hf, ffn_w_gate[l].astype(BF16), ffn_w_up[l].astype(BF16), ffn_conv_w[l], ffn_conv_b[l])
        xs = matmul(act, ffn_w_down[l].astype(BF16), tm=512, tn=512, out_dtype=F32, residual=xs, name="ffn_down")
    return xs.reshape(b, s, d)
```

```python
import functools
import math

import numpy as np
import jax
import jax.numpy as jnp
from jax import lax
from jax.experimental import pallas as pl
from jax.experimental.pallas import tpu as pltpu

F32 = jnp.float32
BF16 = jnp.bfloat16

D_MODEL = 4096
DEPTH = 2
HEAD_DIM = 128
EPS = 1e-6
MLA_HEADS = 8
MLA_Q_RANK = 896
MLA_KV_RANK = 512
MLA_NOPE = 128
MLA_ROPE = 64
MLA_V = 128
ROPE_THETA = 10000.0
DIL_PATTERNS = ((128, 1), (512, 4), (2048, 16))
DIL_HP = 3
DIL_HEADS = DIL_HP * len(DIL_PATTERNS)
NSA_HEADS = 8
NSA_KV_HEADS = 2
NSA_GROUP = NSA_HEADS // NSA_KV_HEADS
NSA_CMP_BLOCK = 32
NSA_CMP_STRIDE = 16
NSA_SLC_BLOCK = 64
NSA_TOPK = 16
NSA_WINDOW = 512
DIFF_HEADS = 7
DIFF_QK = 64
DIFF_V = 128
XATTN_HEADS = 4
FFN_DIM = 11008
REL_BUCKETS = 32
REL_MAX_DIST = 2048
BIAS_DIL = 0
BIAS_NSA = DIL_HEADS
BIAS_DIFF = DIL_HEADS + NSA_HEADS

LANE = 128
VMEM_LIMIT = 56 * 1024 * 1024
NEG = -1e30
SEL_NEG = -32768.0
SEL_HALF_BLOCKS = 128
ATT_ROWS = 1024
ATT_TK = 1024
ATT_TKB = 2048
DIL_TQ = 512

C_CQ = 0
C_KR = 896
C_CKV = 1024
C_DQ = 1536
C_DK = 2688
C_DV = 3840
C_NQ = 4992
C_NKC = 6016
C_NVC = 6272
C_NKS = 6528
C_NVS = 6784
C_NKW = 7040
C_NVW = 7296
C_NG = 7552
C_FQ = 7680
C_FK = 8576
C_FV = 9472
NP_COLS = 10752


def _cp(sem):
    return pltpu.CompilerParams(dimension_semantics=sem, vmem_limit_bytes=VMEM_LIMIT)


def _rmsnorm_rows_kernel(x_ref, g_ref, o_ref):
    x = x_ref[...].astype(F32)
    ms = jnp.mean(x * x, axis=-1, keepdims=True)
    o_ref[...] = (x * lax.rsqrt(ms + EPS) * g_ref[...]).astype(o_ref.dtype)


def rmsnorm_rows(x, g, tm=256):
    m, d = x.shape
    tm = min(tm, m)
    return pl.pallas_call(
        _rmsnorm_rows_kernel,
        grid=(m // tm,),
        in_specs=[pl.BlockSpec((tm, d), lambda i: (i, 0)),
                  pl.BlockSpec((1, d), lambda i: (0, 0))],
        out_specs=pl.BlockSpec((tm, d), lambda i: (i, 0)),
        out_shape=jax.ShapeDtypeStruct((m, d), BF16),
        compiler_params=_cp(("parallel",)),
        name="rmsnorm_rows",
    )(x, g.reshape(1, d).astype(F32))


def _matmul_kernel(*refs, has_res):
    if has_res:
        a_ref, w_ref, r_ref, o_ref = refs
    else:
        a_ref, w_ref, o_ref = refs
    acc = jnp.dot(a_ref[...], w_ref[...], preferred_element_type=F32)
    if has_res:
        acc = acc + r_ref[...]
    o_ref[...] = acc.astype(o_ref.dtype)


def matmul(a, w, *, tm, tn, out_dtype, residual=None, name="matmul"):
    m, k = a.shape
    n = w.shape[1]
    tm = min(tm, m)
    tn = min(tn, n)
    assert m % tm == 0 and n % tn == 0
    in_specs = [pl.BlockSpec((tm, k), lambda j, i: (i, 0)),
                pl.BlockSpec((k, tn), lambda j, i: (0, j))]
    args = [a, w]
    if residual is not None:
        in_specs.append(pl.BlockSpec((tm, tn), lambda j, i: (i, j)))
        args.append(residual)
    return pl.pallas_call(
        functools.partial(_matmul_kernel, has_res=residual is not None),
        grid=(n // tn, m // tm),
        in_specs=in_specs,
        out_specs=pl.BlockSpec((tm, tn), lambda j, i: (i, j)),
        out_shape=jax.ShapeDtypeStruct((m, n), out_dtype),
        compiler_params=_cp(("parallel", "parallel")),
        name=name,
    )(*args)


def _ffn_gate_up_kernel(h_ref, wg_ref, wu_ref, cw_ref, cb_ref, o_ref, carry_ref):
    i = pl.program_id(1)

    @pl.when(i == 0)
    def _():
        carry_ref[...] = jnp.zeros_like(carry_ref)

    h = h_ref[...]
    u = jnp.dot(h, wg_ref[...], preferred_element_type=F32)
    up = jnp.dot(h, wu_ref[...], preferred_element_type=F32)
    tm = u.shape[0]
    prev = carry_ref[...]
    row = lax.broadcasted_iota(jnp.int32, u.shape, 0)
    u1 = jnp.where(row == 0, prev[7:8, :], pltpu.roll(u, 1, 0))
    u2 = pltpu.roll(u, 2, 0)
    u2 = jnp.where(row == 0, prev[6:7, :], jnp.where(row == 1, prev[7:8, :], u2))
    cw = cw_ref[...]
    g = cw[0:1, :] * u2 + cw[1:2, :] * u1 + cw[2:3, :] * u + cb_ref[...]
    o_ref[...] = (g * jax.nn.sigmoid(g) * up).astype(o_ref.dtype)
    carry_ref[...] = u[tm - 8:, :]


def ffn_gate_up(h, wg, wu, conv_w, conv_b, tm=1024, tf=256):
    m, d = h.shape
    f = wg.shape[1]
    tm = min(tm, m)
    assert f % tf == 0 and m % tm == 0
    return pl.pallas_call(
        _ffn_gate_up_kernel,
        grid=(f // tf, m // tm),
        in_specs=[pl.BlockSpec((tm, d), lambda j, i: (i, 0)),
                  pl.BlockSpec((d, tf), lambda j, i: (0, j)),
                  pl.BlockSpec((d, tf), lambda j, i: (0, j)),
                  pl.BlockSpec((3, tf), lambda j, i: (0, j)),
                  pl.BlockSpec((1, tf), lambda j, i: (0, j))],
        out_specs=pl.BlockSpec((tm, tf), lambda j, i: (i, j)),
        out_shape=jax.ShapeDtypeStruct((m, f), BF16),
        scratch_shapes=[pltpu.VMEM((8, tf), F32)],
        compiler_params=_cp(("arbitrary", "arbitrary")),
        name="ffn_gate_up",
    )(h, wg, wu, conv_w.astype(F32), conv_b.reshape(1, f).astype(F32))


def _group_norm(x, d):
    sq = x * x
    tot = jnp.sum(sq, axis=-1, keepdims=True)
    if d == LANE:
        ms = tot / d
    else:
        lane = lax.broadcasted_iota(jnp.int32, x.shape, 1)
        lo = jnp.sum(jnp.where(lane < d, sq, 0.0), axis=-1, keepdims=True)
        ms = jnp.where(lane < d, lo, tot - lo) / d
    return x * lax.rsqrt(ms + EPS)


def _head_norm_kernel(x_ref, g_ref, o_ref, *, d, scale):
    x = x_ref[...].astype(F32)
    o_ref[...] = (_group_norm(x, d) * (g_ref[...] * scale)).astype(o_ref.dtype)


def head_rmsnorm(src, col0, ncb, gain, d, scale, tm=2048):
    s = src.shape[0]
    tm = min(tm, s)
    cb0 = col0 // LANE
    g = jnp.tile(gain.astype(F32), LANE // d).reshape(1, LANE)
    return pl.pallas_call(
        functools.partial(_head_norm_kernel, d=d, scale=scale),
        grid=(s // tm, ncb),
        in_specs=[pl.BlockSpec((tm, LANE), lambda i, c: (i, cb0 + c)),
                  pl.BlockSpec((1, LANE), lambda i, c: (0, 0))],
        out_specs=pl.BlockSpec((tm, LANE), lambda i, c: (i, c)),
        out_shape=jax.ShapeDtypeStruct((s, ncb * LANE), BF16),
        compiler_params=_cp(("parallel", "parallel")),
        name="head_rmsnorm",
    )(src, g)


def _nsa_ksel_kernel(x_ref, g_ref, o_ref):
    tm = x_ref.shape[0]
    x = x_ref[...].astype(F32)
    o_ref[:, :LANE] = (_group_norm(x, LANE) * g_ref[...]).astype(o_ref.dtype)
    row = pl.program_id(0) * tm + lax.broadcasted_iota(jnp.int32, (tm, LANE), 0)
    col = lax.broadcasted_iota(jnp.int32, (tm, LANE), 1)
    hot = ((row // NSA_SLC_BLOCK) % SEL_HALF_BLOCKS) == col
    o_ref[:, LANE:] = jnp.where(hot, 1.0, 0.0).astype(o_ref.dtype)


def nsa_sel_keys(proj, gain, tm=2048):
    s = proj.shape[0]
    tm = min(tm, s)
    cb0 = C_NKS // LANE
    return pl.pallas_call(
        _nsa_ksel_kernel,
        grid=(s // tm, NSA_KV_HEADS),
        in_specs=[pl.BlockSpec((tm, LANE), lambda i, c: (i, cb0 + c)),
                  pl.BlockSpec((1, LANE), lambda i, c: (0, 0))],
        out_specs=pl.BlockSpec((tm, 2 * LANE), lambda i, c: (i, c)),
        out_shape=jax.ShapeDtypeStruct((s, NSA_KV_HEADS * 2 * LANE), BF16),
        compiler_params=_cp(("parallel", "parallel")),
        name="nsa_sel_keys",
    )(proj, gain.reshape(1, LANE).astype(F32))


def _rope_tables(pos_ref, inv_ref, sign_ref):
    ang = pos_ref[...].astype(F32) * inv_ref[...]
    return jnp.cos(ang), jnp.sin(ang) * sign_ref[...]


def _rope_apply(x, cosv, sin_signed):
    lane = lax.broadcasted_iota(jnp.int32, x.shape, 1)
    half = MLA_ROPE // 2
    swapped = jnp.where(lane < half, pltpu.roll(x, LANE - half, 1), pltpu.roll(x, half, 1))
    return x * cosv + swapped * sin_signed


def _latent_norm(c_ref, g_ref):
    c = c_ref[...].astype(F32)
    ms = jnp.mean(c * c, axis=-1, keepdims=True)
    return (c * lax.rsqrt(ms + EPS) * g_ref[...]).astype(BF16)


def _mla_q_kernel(c_ref, pos_ref, gq_ref, w_ref, gn_ref, gr_ref, inv_ref, sign_ref, o_ref, *, scale):
    u = jnp.dot(_latent_norm(c_ref, gq_ref), w_ref[...], preferred_element_type=F32)
    cosv, sinv = _rope_tables(pos_ref, inv_ref, sign_ref)
    for h in range(MLA_HEADS):
        b = 2 * LANE * h
        nope = _group_norm(u[:, b:b + LANE], LANE) * (gn_ref[...] * scale)
        rp = u[:, b + LANE:b + 2 * LANE]
        rp = rp * lax.rsqrt(jnp.sum(rp * rp, axis=-1, keepdims=True) / MLA_ROPE + EPS) * gr_ref[...]
        rp = _rope_apply(rp, cosv, sinv) * scale
        o_ref[:, b:b + LANE] = nope.astype(o_ref.dtype)
        o_ref[:, b + LANE:b + 2 * LANE] = rp.astype(o_ref.dtype)


def _mla_kv_kernel(c_ref, kr_ref, pos_ref, gkv_ref, w_ref, gn_ref, gr_ref, inv_ref, sign_ref, k_ref, v_ref):
    kv = jnp.dot(_latent_norm(c_ref, gkv_ref), w_ref[...], preferred_element_type=F32)
    cosv, sinv = _rope_tables(pos_ref, inv_ref, sign_ref)
    kr = kr_ref[...].astype(F32)
    kr = kr * lax.rsqrt(jnp.sum(kr * kr, axis=-1, keepdims=True) / MLA_ROPE + EPS) * gr_ref[...]
    kr = _rope_apply(kr, cosv, sinv).astype(k_ref.dtype)
    nv = MLA_HEADS * LANE
    for h in range(MLA_HEADS):
        kn = _group_norm(kv[:, h * LANE:(h + 1) * LANE], LANE) * gn_ref[...]
        k_ref[:, 2 * LANE * h:2 * LANE * h + LANE] = kn.astype(k_ref.dtype)
        k_ref[:, 2 * LANE * h + LANE:2 * LANE * (h + 1)] = kr
    v_ref[...] = kv[:, nv:].astype(v_ref.dtype)


def _rope_consts():
    half = MLA_ROPE // 2
    inv = ROPE_THETA ** (-jnp.arange(half, dtype=F32) / half)
    inv = jnp.concatenate([inv, inv, jnp.zeros((LANE - MLA_ROPE,), F32)]).reshape(1, LANE)
    sign = np.zeros((1, LANE), np.float32)
    sign[0, :half] = -1.0
    sign[0, half:MLA_ROPE] = 1.0
    return inv, jnp.asarray(sign)


def _pad_lanes(v, n=LANE):
    return jnp.concatenate([v.astype(F32), jnp.zeros((n - v.shape[0],), F32)]).reshape(1, n)


def mla_q_proj(proj, pos, q_norm, w_uq, nope_gain, rope_gain, tm=512):
    s = proj.shape[0]
    tm = min(tm, s)
    inv, sign = _rope_consts()
    w = jnp.pad(w_uq.reshape(MLA_Q_RANK, MLA_HEADS, MLA_NOPE + MLA_ROPE),
                ((0, 0), (0, 0), (0, LANE - MLA_ROPE))).reshape(MLA_Q_RANK, MLA_HEADS * 2 * LANE).astype(BF16)
    wq = MLA_HEADS * 2 * LANE
    const = lambda i: (0, 0)
    return pl.pallas_call(
        functools.partial(_mla_q_kernel, scale=(MLA_NOPE + MLA_ROPE) ** -0.5),
        grid=(s // tm,),
        in_specs=[pl.BlockSpec((tm, MLA_Q_RANK), lambda i: (i, C_CQ // MLA_Q_RANK)),
                  pl.BlockSpec((tm, 1), lambda i: (i, 0)),
                  pl.BlockSpec((1, MLA_Q_RANK), const),
                  pl.BlockSpec((MLA_Q_RANK, wq), const),
                  pl.BlockSpec((1, LANE), const), pl.BlockSpec((1, LANE), const),
                  pl.BlockSpec((1, LANE), const), pl.BlockSpec((1, LANE), const)],
        out_specs=pl.BlockSpec((tm, wq), lambda i: (i, 0)),
        out_shape=jax.ShapeDtypeStruct((s, wq), BF16),
        compiler_params=_cp(("parallel",)),
        name="mla_q_proj",
    )(proj, pos, q_norm.reshape(1, -1).astype(F32), w, nope_gain.reshape(1, LANE).astype(F32),
      _pad_lanes(rope_gain), inv, sign)


def mla_kv_proj(proj, pos, kv_norm, w_ukv, nope_gain, rope_gain, tm=512):
    s = proj.shape[0]
    tm = min(tm, s)
    inv, sign = _rope_consts()
    w = jnp.transpose(w_ukv.reshape(MLA_KV_RANK, MLA_HEADS, 2, LANE), (0, 2, 1, 3))
    w = w.reshape(MLA_KV_RANK, 2 * MLA_HEADS * LANE).astype(BF16)
    wk = MLA_HEADS * 2 * LANE
    wv = MLA_HEADS * MLA_V
    const = lambda i: (0, 0)
    return pl.pallas_call(
        _mla_kv_kernel,
        grid=(s // tm,),
        in_specs=[pl.BlockSpec((tm, MLA_KV_RANK), lambda i: (i, C_CKV // MLA_KV_RANK)),
                  pl.BlockSpec((tm, LANE), lambda i: (i, C_KR // LANE)),
                  pl.BlockSpec((tm, 1), lambda i: (i, 0)),
                  pl.BlockSpec((1, MLA_KV_RANK), const),
                  pl.BlockSpec((MLA_KV_RANK, 2 * MLA_HEADS * LANE), const),
                  pl.BlockSpec((1, LANE), const), pl.BlockSpec((1, LANE), const),
                  pl.BlockSpec((1, LANE), const), pl.BlockSpec((1, LANE), const)],
        out_specs=[pl.BlockSpec((tm, wk), lambda i: (i, 0)),
                   pl.BlockSpec((tm, wv), lambda i: (i, 0))],
        out_shape=[jax.ShapeDtypeStruct((s, wk), BF16), jax.ShapeDtypeStruct((s, wv), BF16)],
        compiler_params=_cp(("parallel",)),
        name="mla_kv_proj",
    )(proj, proj, pos, kv_norm.reshape(1, -1).astype(F32), w, nope_gain.reshape(1, LANE).astype(F32),
      _pad_lanes(rope_gain), inv, sign)


def _rel_bucket_np(dist):
    n = np.maximum(dist, 0)
    exact = REL_BUCKETS // 2
    ratio = np.log(np.maximum(n, 1).astype(np.float32) / np.float32(exact)) / np.float32(math.log(REL_MAX_DIST / exact))
    large = np.minimum(exact + (ratio.astype(np.float32) * (REL_BUCKETS - exact)).astype(np.int32), REL_BUCKETS - 1)
    return np.where(n < exact, n, large).astype(np.int32)


def _strip_x0(tk):
    return -(-(REL_MAX_DIST - 1 + tk) // LANE) * LANE


def bias_strips(table, tq, tk, tkb, dist_scale):
    x0 = _strip_x0(tkb)
    width = x0 + max(tq, tk)
    period = -(-(width + tq) // LANE) * LANE
    t = np.arange(period)
    t = np.where(t < width, t, t - period)
    u = table.astype(F32)[_rel_bucket_np((x0 - t) * dist_scale)].T
    h = u.shape[0]
    return pl.pallas_call(
        functools.partial(_toeplitz_kernel, tq=tq, width=width),
        grid=(h,),
        in_specs=[pl.BlockSpec((None, 1, period), lambda n: (n, 0, 0))],
        out_specs=pl.BlockSpec((None, tq, width), lambda n: (n, 0, 0)),
        out_shape=jax.ShapeDtypeStruct((h, tq, width), F32),
        compiler_params=_cp(("parallel",)),
        name="bias_strips",
    )(u.reshape(h, 1, period))


def _toeplitz_kernel(u_ref, o_ref, *, tq, width):
    rows = jnp.broadcast_to(u_ref[...], (tq, u_ref.shape[1]))
    o_ref[...] = pltpu.roll(rows, 0, 1, stride=1, stride_axis=0)[:, :width]


def _attn_kernel(*refs, mode, tq, tk, tkb, G, dq, dv, span, has_bias, far_zero, x0, n_halves, want_lse, out_scale):
    refs = list(refs)
    q_ref = refs.pop(0)
    sel_ref = refs.pop(0) if mode == "nsa_sel" else None
    k_ref = refs.pop(0)
    v_ref = refs.pop(0)
    bias_ref = refs.pop(0) if has_bias else None
    lam_ref = gout_ref = None
    if mode == "diff":
        lam_ref = refs.pop(0)
        gout_ref = refs.pop(0)
    o_ref = refs.pop(0)
    lse_ref = refs.pop(0) if want_lse else None
    m_sc, l_sc, acc_sc = refs

    qi = pl.program_id(1)
    q0 = qi * tq
    n_tiles = k_ref.shape[0] // tk

    if mode == "diff":
        q = q_ref[...]
        lane = lax.broadcasted_iota(jnp.int32, q.shape, 1)
        zero = jnp.zeros_like(q)
        lhs = [jnp.concatenate([jnp.where(lane < DIFF_QK, q, zero), jnp.where(lane >= DIFF_QK, q, zero)], axis=0)]
    elif mode == "nsa_sel":
        lhs = []
        for hf in range(n_halves):
            sel = sel_ref[hf]
            lhs.append(jnp.concatenate(
                [jnp.concatenate([q_ref[:, g * dq:(g + 1) * dq], sel], axis=1) for g in range(G)], axis=0))
    else:
        lhs = [jnp.concatenate([q_ref[:, g * dq:(g + 1) * dq] for g in range(G)], axis=0) if G > 1 else q_ref[...]]

    m_sc[...] = jnp.full_like(m_sc, NEG)
    l_sc[...] = jnp.zeros_like(l_sc)
    acc_sc[...] = jnp.zeros_like(acc_sc)

    def logits(k0, w, Q, masked, biased):
        kt = k_ref[pl.ds(k0, w), :]
        s = lax.dot_general(Q, kt, (((1,), (1,)), ((), ())), preferred_element_type=F32)
        has_bias = biased
        if has_bias or masked:
            if masked:
                rel = (q0 - k0) + lax.broadcasted_iota(jnp.int32, (tq, w), 0) - lax.broadcasted_iota(jnp.int32, (tq, w), 1)
                vis = rel >= 0
                if span is not None:
                    vis = jnp.logical_and(vis, rel <= span)
            if has_bias:
                start = pl.multiple_of(jnp.maximum(x0 - (q0 - k0), 0), LANE)
            parts = []
            for g in range(G):
                sg = s[g * tq:(g + 1) * tq]
                if has_bias:
                    sg = sg + bias_ref[0 if mode == "diff" else g, :, pl.ds(start, w)]
                if masked:
                    sg = jnp.where(vis, sg, NEG)
                parts.append(sg)
            s = jnp.concatenate(parts, axis=0) if G > 1 else parts[0]
        return s

    def update(s, k0, w):
        vt = v_ref[pl.ds(k0, w), :]
        m_prev = m_sc[...]
        m_new = jnp.maximum(m_prev, jnp.max(s, axis=-1, keepdims=True))
        alpha = jnp.exp(m_prev - m_new)
        p = jnp.exp(s - jnp.tile(m_new, (1, w // LANE)))
        l_sc[...] = alpha * l_sc[...] + jnp.sum(p, axis=-1, keepdims=True)
        acc_sc[...] = alpha * acc_sc[...] + jnp.dot(p.astype(vt.dtype), vt, preferred_element_type=F32)
        m_sc[...] = m_new

    def tile(k0, w, Q, masked, biased):
        if w == tkb and tkb > tk:
            h = w // 2
            sa = logits(k0, h, Q, masked, biased)
            sb = logits(k0 + h, h, Q, masked, biased)
            update(sa, k0, h)
            update(sb, k0 + h, h)
        else:
            update(logits(k0, w, Q, masked, biased), k0, w)

    def run(lo, hi, w, Q, masked, biased=has_bias):
        def body(j, c):
            tile(pl.multiple_of(j * w, w), w, Q, masked, biased)
            return c
        lax.fori_loop(lo, hi, body, 0)

    def run_visible(lo, hi, Q):
        if tkb == tk:
            run(lo, hi, tk, Q, False)
            return
        r = tkb // tk
        up = jnp.minimum(-(-lo // r) * r, hi)
        dn = jnp.maximum((hi // r) * r, up)
        if not (isinstance(lo, int) and lo % r == 0):
            run(lo, up, tk, Q, False)
        if has_bias and far_zero:
            far = jnp.clip(jnp.maximum(q0 - (REL_MAX_DIST - 1), 0) // tkb, up // r, dn // r)
            run(up // r, far, tkb, Q, False, False)
            run(far, dn // r, tkb, Q, False)
        else:
            run(up // r, dn // r, tkb, Q, False)
        run(dn, hi, tk, Q, False)

    j_hi = jnp.minimum((q0 + tq - 1) // tk, n_tiles - 1)
    b = (q0 + 1) // tk
    if span is None:
        j_lo = 0
        a = 0
    else:
        j_lo = jnp.maximum(q0 - span, 0) // tk
        a = jnp.maximum(q0 + tq - 1 - span + tk - 1, 0) // tk
        a = jnp.minimum(jnp.maximum(a, j_lo), b)
    tph = (SEL_HALF_BLOCKS * NSA_SLC_BLOCK) // tk
    for hf in range(n_halves):
        Q = lhs[hf]
        if mode == "nsa_sel":
            clip = lambda lo, hi: (jnp.maximum(lo, hf * tph) if hf else lo, jnp.minimum(hi, (hf + 1) * tph))
        else:
            clip = lambda lo, hi: (lo, hi)
        if span is not None:
            run(*clip(j_lo, a), tk, Q, True)
        run_visible(*clip(a, b), Q)
        run(*clip(b, j_hi + 1), tk, Q, True)

    l = l_sc[...]
    acc = acc_sc[...] / l
    if mode == "diff":
        o = acc[:tq] - lam_ref[...] * acc[tq:]
        o = o * lax.rsqrt(jnp.mean(o * o, axis=-1, keepdims=True) + EPS) * (gout_ref[...] * out_scale)
        o_ref[...] = o.astype(o_ref.dtype)
    else:
        for g in range(G):
            o_ref[:, g * dv:(g + 1) * dv] = acc[g * tq:(g + 1) * tq].astype(o_ref.dtype)
        if want_lse:
            lse = m_sc[...] + jnp.log(l)
            for g in range(G):
                lse_ref[:, g * LANE:(g + 1) * LANE] = lse[g * tq:(g + 1) * tq]


def attention(q, k, v, *, n_streams, seq, out_cols, qcol, kcol, vcol, ocol, mode="plain", tq, tk, tkb=None, G=1,
              dq, dk, dv=LANE, span=None, bias=None, bcol=None, sel=None, lam=None, gout=None, out_scale=1.0,
              want_lse=False, bias_far_zero=False, name="attention"):
    tq = min(tq, seq)
    tk = min(tk, seq)
    tkb = tk if tkb is None else min(tkb, seq)
    assert seq % tq == 0 and seq % tkb == 0 and tkb % tk == 0 and tq % LANE == 0 and tk % LANE == 0
    resident = dict(pipeline_mode=pl.Buffered(1))
    R = 2 * tq if mode == "diff" else G * tq
    Gk = 2 if mode == "diff" else G
    wq = dq if mode == "diff" else G * dq
    in_specs = [pl.BlockSpec((tq, wq), lambda n, i: (i, qcol(n)))]
    args = [q]
    n_halves = 1
    if mode == "nsa_sel":
        n_halves = sel.shape[0]
        in_specs.append(pl.BlockSpec((n_halves, tq, LANE), lambda n, i: (0, i, n)))
        args.append(sel)
    in_specs += [pl.BlockSpec((seq, dk), lambda n, i: (0, kcol(n)), **resident),
                 pl.BlockSpec((seq, dv), lambda n, i: (0, vcol(n)), **resident)]
    args += [k, v]
    x0 = 0
    if bias is not None:
        x0 = _strip_x0(tkb)
        gb = 1 if mode == "diff" else G
        in_specs.append(pl.BlockSpec((gb, tq, x0 + max(tq, tk)), lambda n, i: (bcol(n), 0, 0), **resident))
        args.append(bias)
    if mode == "diff":
        in_specs += [pl.BlockSpec((1, LANE), lambda n, i: (0, 0)), pl.BlockSpec((1, LANE), lambda n, i: (0, 0))]
        args += [lam, gout]
    wo = dv if mode == "diff" else G * dv
    out_specs = [pl.BlockSpec((tq, wo), lambda n, i: (i, ocol(n)))]
    out_shape = [jax.ShapeDtypeStruct((seq, out_cols), BF16)]
    if want_lse:
        out_specs.append(pl.BlockSpec((tq, G * LANE), lambda n, i: (i, ocol(n))))
        out_shape.append(jax.ShapeDtypeStruct((seq, out_cols), F32))
    kern = functools.partial(_attn_kernel, mode=mode, tq=tq, tk=tk, tkb=tkb, G=Gk, dq=dq, dv=dv, span=span,
                             has_bias=bias is not None, far_zero=bias_far_zero, x0=x0, n_halves=n_halves,
                             want_lse=want_lse,
                             out_scale=out_scale)
    res = pl.pallas_call(
        kern,
        grid=(n_streams, seq // tq),
        in_specs=in_specs,
        out_specs=out_specs,
        out_shape=out_shape,
        scratch_shapes=[pltpu.VMEM((R, LANE), F32), pltpu.VMEM((R, LANE), F32), pltpu.VMEM((R, dv), F32)],
        compiler_params=_cp(("parallel", "parallel")),
        name=name,
    )(*args)
    return res if want_lse else res[0]


def _dil_combine_kernel(o0, o1, o2, l0, l1, l2, out_ref):
    os_ = (o0, o1, o2)
    ls_ = (l0, l1, l2)
    for j in range(DIL_HP):
        sl = slice(j * LANE, (j + 1) * LANE)
        lse = [r[:, sl] for r in ls_]
        m = jnp.maximum(jnp.maximum(lse[0], lse[1]), lse[2])
        e = [jnp.exp(x - m) for x in lse]
        inv = 1.0 / (e[0] + e[1] + e[2])
        for g in range(len(DIL_PATTERNS)):
            c = (g * DIL_HP + j) * LANE
            out_ref[:, c:c + LANE] = (os_[g][:, sl].astype(F32) * (e[g] * inv)).astype(out_ref.dtype)


def dil_combine(outs, lses, tm=512):
    s = outs[0].shape[0]
    tm = min(tm, s)
    w = DIL_HP * LANE
    spec = pl.BlockSpec((tm, w), lambda i: (i, 0))
    return pl.pallas_call(
        _dil_combine_kernel,
        grid=(s // tm,),
        in_specs=[spec] * 6,
        out_specs=pl.BlockSpec((tm, DIL_HEADS * LANE), lambda i: (i, 0)),
        out_shape=jax.ShapeDtypeStruct((s, DIL_HEADS * LANE), BF16),
        compiler_params=_cp(("parallel",)),
        name="dil_combine",
    )(*outs, *lses)


def _gelu_tanh(y):
    return 0.5 * y * (1.0 + jnp.tanh(0.7978845608028654 * (y + 0.044715 * y * y * y)))


def _compress_kernel(a_ref, pe_ref, w1_ref, w2_ref, g_ref, o_ref, *, norm, transpose):
    a = a_ref[...].astype(F32)
    nc = a.shape[0]
    lo = jnp.dot((a + pe_ref[0]).astype(BF16), w1_ref[0], preferred_element_type=F32)
    hi = jnp.dot((a + pe_ref[1]).astype(BF16), w1_ref[1], preferred_element_type=F32)
    y = lo + pltpu.roll(hi, nc - 1, 0)
    z = jnp.dot(_gelu_tanh(y).astype(BF16), w2_ref[...], preferred_element_type=F32)
    if norm:
        z = _group_norm(z, LANE) * g_ref[...]
    o_ref[...] = (z.T if transpose else z).astype(o_ref.dtype)


def nsa_compress(chunks, pe, w1, w2, gain, *, norm, transpose):
    kvh, nc, w = chunks.shape
    half = NSA_CMP_STRIDE * LANE
    pe2 = pe.astype(F32).reshape(2, 1, half)
    w1s = w1.reshape(2, half, LANE).astype(BF16)
    oshape = (kvh, LANE, nc) if transpose else (kvh, nc, LANE)
    oblock = (None, LANE, nc) if transpose else (None, nc, LANE)
    return pl.pallas_call(
        functools.partial(_compress_kernel, norm=norm, transpose=transpose),
        grid=(kvh,),
        in_specs=[pl.BlockSpec((None, nc, w), lambda h: (h, 0, 0)),
                  pl.BlockSpec((2, 1, half), lambda h: (0, 0, 0)),
                  pl.BlockSpec((2, half, LANE), lambda h: (0, 0, 0)),
                  pl.BlockSpec((LANE, LANE), lambda h: (0, 0)),
                  pl.BlockSpec((1, LANE), lambda h: (0, 0))],
        out_specs=pl.BlockSpec(oblock, lambda h: (h, 0, 0)),
        out_shape=jax.ShapeDtypeStruct(oshape, BF16),
        compiler_params=_cp(("parallel",)),
        name="nsa_compress",
    )(chunks, pe2, w1s, w2.astype(BF16), gain.reshape(1, LANE).astype(F32))


def _cmp_topk_kernel(q_ref, kc_ref, vt_ref, ov_ref, strip_ref, o_ref, sel_ref, *, tq, y0, k_sel, n_halves):
    qi = pl.program_id(1)
    q0 = qi * tq
    G = NSA_GROUP
    nc = kc_ref.shape[0]
    qs = jnp.concatenate([q_ref[:, g * LANE:(g + 1) * LANE] for g in range(G)], axis=0)
    st = lax.dot_general(kc_ref[...], qs, (((1,), (1,)), ((), ())), preferred_element_type=F32)
    n_io = lax.broadcasted_iota(jnp.int32, (nc, tq), 0)
    i_io = lax.broadcasted_iota(jnp.int32, (nc, tq), 1)
    vis = (q0 + i_io - NSA_CMP_STRIDE * n_io - (NSA_CMP_BLOCK - 1)) >= 0
    start = pl.multiple_of(y0 - qi * (tq // NSA_CMP_STRIDE), 8)
    psum = jnp.zeros((nc, tq), F32)
    pts = []
    for g in range(G):
        s = st[:, g * tq:(g + 1) * tq] + strip_ref[g, pl.ds(start, nc), :]
        s = jnp.where(vis, s, NEG)
        m = jnp.max(s, axis=0, keepdims=True)
        e = jnp.where(vis, jnp.exp(s - m), 0.0)
        ssum = jnp.sum(e, axis=0, keepdims=True)
        p = e / jnp.where(ssum > 0, ssum, 1.0)
        psum = psum + p
        pts.append(p.astype(BF16))
    pt = jnp.concatenate(pts, axis=1)
    ot = jnp.dot(vt_ref[...], pt, preferred_element_type=F32)
    for g in range(G):
        o_ref[:, g * LANE:(g + 1) * LANE] = ot[:, g * tq:(g + 1) * tq].T.astype(o_ref.dtype)
    p_hi = psum.astype(BF16)
    p_lo = (psum - p_hi.astype(F32)).astype(BF16)
    imp = (jnp.dot(ov_ref[...], p_hi, preferred_element_type=F32)
           + jnp.dot(ov_ref[...], p_lo, preferred_element_type=F32))
    nb = imp.shape[0]
    b_io = lax.broadcasted_iota(jnp.int32, (nb, tq), 0)
    qblk = (q0 + lax.broadcasted_iota(jnp.int32, (nb, tq), 1)) // NSA_SLC_BLOCK
    forced = (b_io == 0) | (b_io == qblk) | (b_io == qblk - 1)
    val = jnp.where(forced, 3e38, jnp.where(b_io <= qblk, imp, -1.0))
    seln = jnp.full((nb, tq), SEL_NEG, F32)
    for _ in range(k_sel):
        mx = jnp.max(val, axis=0, keepdims=True)
        idx = jnp.min(jnp.where(val == mx, b_io, nb), axis=0, keepdims=True)
        pick = (b_io == idx) & (mx >= 0.0)
        seln = jnp.where(pick, 0.0, seln)
        val = jnp.where(pick, -2.0, val)
    sel = seln.T
    for hf in range(n_halves):
        sel_ref[hf] = sel[:, hf * LANE:(hf + 1) * LANE].astype(sel_ref.dtype)


def _cmp_y0(s):
    return (LANE // NSA_CMP_STRIDE) * (s // LANE - 1)


def cmp_bias_strips(table, s):
    nc = s // NSA_CMP_STRIDE
    y0 = _cmp_y0(s)
    far_rows = -(-(REL_MAX_DIST + NSA_CMP_BLOCK) // NSA_CMP_STRIDE)
    ya = max(y0 - far_rows, 0)
    yb = min(y0 + LANE // NSA_CMP_STRIDE, y0 + nc)
    y = np.arange(ya, yb)[:, None]
    i = np.arange(LANE)[None, :]
    band = jnp.transpose(table.astype(F32)[_rel_bucket_np(NSA_CMP_STRIDE * (y0 - y) + i - (NSA_CMP_BLOCK - 1))], (2, 0, 1))
    h = table.shape[1]
    far = jnp.broadcast_to(table.astype(F32)[REL_BUCKETS - 1][:, None, None], (h, ya, LANE))
    return jnp.concatenate([far, band, jnp.zeros((h, y0 + nc - yb, LANE), F32)], axis=1)


def nsa_cmp_topk(nqn, k_cmp, v_cmp_t, strips, s):
    tq = LANE
    nc = s // NSA_CMP_STRIDE
    n_cmp = nc - NSA_CMP_BLOCK // NSA_CMP_STRIDE + 1
    n_slc = s // NSA_SLC_BLOCK
    nbp = -(-n_slc // LANE) * LANE
    n_halves = nbp // LANE
    cs = np.arange(nc) * NSA_CMP_STRIDE
    ce = cs + NSA_CMP_BLOCK - 1
    ss = np.arange(nbp) * NSA_SLC_BLOCK
    ov = ((cs[None, :] < ss[:, None] + NSA_SLC_BLOCK) & (ce[None, :] >= ss[:, None])).astype(np.float32)
    ov[:, n_cmp:] = 0.0
    ov[n_slc:, :] = 0.0
    y0 = _cmp_y0(s)
    k_sel = min(NSA_TOPK, n_slc)
    return pl.pallas_call(
        functools.partial(_cmp_topk_kernel, tq=tq, y0=y0, k_sel=k_sel, n_halves=n_halves),
        grid=(NSA_KV_HEADS, s // tq),
        in_specs=[pl.BlockSpec((tq, NSA_GROUP * LANE), lambda h, i: (i, h)),
                  pl.BlockSpec((None, nc, LANE), lambda h, i: (h, 0, 0)),
                  pl.BlockSpec((None, LANE, nc), lambda h, i: (h, 0, 0)),
                  pl.BlockSpec((nbp, nc), lambda h, i: (0, 0)),
                  pl.BlockSpec((NSA_GROUP, y0 + nc, tq), lambda h, i: (h, 0, 0))],
        out_specs=[pl.BlockSpec((tq, NSA_GROUP * LANE), lambda h, i: (i, h)),
                   pl.BlockSpec((n_halves, tq, LANE), lambda h, i: (0, i, h))],
        out_shape=[jax.ShapeDtypeStruct((s, NSA_HEADS * LANE), BF16),
                   jax.ShapeDtypeStruct((n_halves, s, NSA_KV_HEADS * LANE), BF16)],
        compiler_params=_cp(("parallel", "parallel")),
        name="nsa_cmp_topk",
    )(nqn, k_cmp, v_cmp_t, jnp.asarray(ov, BF16), strips)


def _nsa_combine_kernel(oc_ref, os_ref, ow_ref, g_ref, o_ref):
    gate = jax.nn.sigmoid(g_ref[...].astype(F32))
    for h in range(NSA_HEADS):
        sl = slice(h * LANE, (h + 1) * LANE)
        o = (gate[:, 3 * h:3 * h + 1] * oc_ref[:, sl].astype(F32)
             + gate[:, 3 * h + 1:3 * h + 2] * os_ref[:, sl].astype(F32)
             + gate[:, 3 * h + 2:3 * h + 3] * ow_ref[:, sl].astype(F32))
        o_ref[:, sl] = o.astype(o_ref.dtype)


def nsa_combine(o_c, o_s, o_w, proj, tm=512):
    s = o_c.shape[0]
    tm = min(tm, s)
    w = NSA_HEADS * LANE
    spec = pl.BlockSpec((tm, w), lambda i: (i, 0))
    return pl.pallas_call(
        _nsa_combine_kernel,
        grid=(s // tm,),
        in_specs=[spec, spec, spec, pl.BlockSpec((tm, LANE), lambda i: (i, C_NG // LANE))],
        out_specs=spec,
        out_shape=jax.ShapeDtypeStruct((s, w), BF16),
        compiler_params=_cp(("parallel",)),
        name="nsa_combine",
    )(o_c, o_s, o_w, proj)


def _xattn_kernel(q_ref, kv_ref, gq_ref, gk_ref, o_ref, *, scale):
    nk = XATTN_HEADS * LANE
    for h in range(XATTN_HEADS):
        sl = slice(h * LANE, (h + 1) * LANE)
        qn = (_group_norm(q_ref[:, sl].astype(F32), LANE) * (gq_ref[...] * scale)).astype(BF16)
        kn = (_group_norm(kv_ref[:, sl].astype(F32), LANE) * gk_ref[...]).astype(BF16)
        s = lax.dot_general(qn, kn, (((1,), (1,)), ((), ())), preferred_element_type=F32)
        m = jnp.max(s, axis=-1, keepdims=True)
        e = jnp.exp(s - m)
        p = e / jnp.sum(e, axis=-1, keepdims=True)
        v = kv_ref[:, nk + h * LANE:nk + (h + 1) * LANE]
        o_ref[:, sl] = jnp.dot(p.astype(v.dtype), v, preferred_element_type=F32).astype(o_ref.dtype)


def xattn_core(qx, kv, qk_norm, tq=512):
    s, w = qx.shape
    tq = min(tq, s)
    m = kv.shape[0]
    return pl.pallas_call(
        functools.partial(_xattn_kernel, scale=HEAD_DIM ** -0.5),
        grid=(s // tq,),
        in_specs=[pl.BlockSpec((tq, w), lambda i: (i, 0)),
                  pl.BlockSpec((m, 2 * w), lambda i: (0, 0)),
                  pl.BlockSpec((1, LANE), lambda i: (0, 0)),
                  pl.BlockSpec((1, LANE), lambda i: (0, 0))],
        out_specs=pl.BlockSpec((tq, w), lambda i: (i, 0)),
        out_shape=jax.ShapeDtypeStruct((s, w), BF16),
        compiler_params=_cp(("parallel",)),
        name="xattn_core",
    )(qx, kv, qk_norm[0].reshape(1, LANE).astype(F32), qk_norm[1].reshape(1, LANE).astype(F32))


def _pack_w_in(w):
    d = w.shape[0]
    z = lambda n: jnp.zeros((d, n), w.dtype)
    parts = [w[:, 0:896], w[:, 1408:1472], z(64), w[:, 896:1408], w[:, 1472:7488],
             w[:, 7488:7512], z(LANE - 3 * NSA_HEADS), w[:, 7512:10200], z(NP_COLS - 10368)]
    return jnp.concatenate(parts, axis=1).astype(BF16)


def _mla_mixer(proj, pos, l, p):
    s = proj.shape[0]
    qm = mla_q_proj(proj, pos, p["mla_q_norm"][l], p["mla_w_uq"][l], p["mla_nope_norm"][l, 0], p["mla_rope_norm"][l, 0])
    km, vm = mla_kv_proj(proj, pos, p["mla_kv_norm"][l], p["mla_w_ukv"][l], p["mla_nope_norm"][l, 1],
                         p["mla_rope_norm"][l, 1])
    ident = lambda n: n
    return attention(qm, km, vm, n_streams=MLA_HEADS, seq=s, out_cols=MLA_HEADS * MLA_V, qcol=ident, kcol=ident,
                     vcol=ident, ocol=ident, tq=ATT_ROWS, tk=ATT_TK, tkb=ATT_TKB, dq=2 * LANE, dk=2 * LANE,
                     name="mla_attention")


def bias_tables(rel_bias, s):
    tabs = {"dil": []}
    for g, (_, dil) in enumerate(DIL_PATTERNS):
        tq = min(DIL_TQ, s // dil)
        tabs["dil"].append(bias_strips(rel_bias[:, BIAS_DIL + g * DIL_HP:BIAS_DIL + (g + 1) * DIL_HP], tq, tq, tq, dil))
    far_zero = lambda t: t - t[REL_BUCKETS - 1:REL_BUCKETS]
    nsa = rel_bias[:, BIAS_NSA:BIAS_NSA + NSA_HEADS].astype(F32)
    tk, tkb = min(ATT_TK, s), min(ATT_TKB, s)
    tabs["nsa"] = bias_strips(far_zero(nsa), ATT_ROWS // NSA_GROUP, tk, tkb, 1)
    tabs["nsa_cmp"] = cmp_bias_strips(nsa, s)
    diff = rel_bias[:, BIAS_DIFF:BIAS_DIFF + DIFF_HEADS].astype(F32)
    tabs["diff"] = bias_strips(far_zero(diff), min(ATT_ROWS // 2, s), tk, tkb, 1)
    return tabs


def _dilated_mixer(proj, l, p, tabs):
    s = proj.shape[0]
    w = DIL_HP * LANE
    dqn = head_rmsnorm(proj, C_DQ, DIL_HEADS, p["dil_qk_norm"][l, 0], LANE, HEAD_DIM ** -0.5)
    dkn = head_rmsnorm(proj, C_DK, DIL_HEADS, p["dil_qk_norm"][l, 1], LANE, 1.0)
    outs, lses = [], []
    ident = lambda n: n
    for g, (window, dil) in enumerate(DIL_PATTERNS):
        n_sub = s // dil
        tq = min(DIL_TQ, n_sub)
        fold = lambda t, c0: t[:, c0 + g * w:c0 + (g + 1) * w].reshape(n_sub, dil * w)
        o, lse = attention(fold(dqn, 0), fold(dkn, 0), fold(proj, C_DV), n_streams=dil * DIL_HP, seq=n_sub,
                           out_cols=dil * w, qcol=ident, kcol=ident, vcol=ident, ocol=ident, tq=tq, tk=tq, dq=LANE,
                           dk=LANE, span=window // dil, bias=tabs["dil"][g], bcol=lambda n: n % DIL_HP,
                           want_lse=True, name=f"dil_attention_{g}")
        outs.append(o.reshape(s, w))
        lses.append(lse.reshape(s, w))
    return dil_combine(outs, lses)


def _nsa_mixer(proj, l, p, tabs):
    s = proj.shape[0]
    gains = p["nsa_qk_norm"][l]
    nqn = head_rmsnorm(proj, C_NQ, NSA_HEADS, gains[0], LANE, HEAD_DIM ** -0.5)
    kwn = head_rmsnorm(proj, C_NKW, NSA_KV_HEADS, gains[3], LANE, 1.0)
    ksel = nsa_sel_keys(proj, gains[2])
    nc = s // NSA_CMP_STRIDE

    def chunks(c0):
        t = proj[:, c0:c0 + NSA_KV_HEADS * LANE].reshape(nc, NSA_CMP_STRIDE, NSA_KV_HEADS, LANE)
        return jnp.transpose(t, (2, 0, 1, 3)).reshape(NSA_KV_HEADS, nc, NSA_CMP_STRIDE * LANE)

    k_cmp = nsa_compress(chunks(C_NKC), p["nsa_cmp_pe"][l, 0], p["nsa_cmp_w1"][l, 0], p["nsa_cmp_w2"][l, 0],
                         gains[1], norm=True, transpose=False)
    v_cmp_t = nsa_compress(chunks(C_NVC), p["nsa_cmp_pe"][l, 1], p["nsa_cmp_w1"][l, 1], p["nsa_cmp_w2"][l, 1],
                           gains[1], norm=False, transpose=True)
    o_c, sel = nsa_cmp_topk(nqn, k_cmp, v_cmp_t, tabs["nsa_cmp"], s)
    ident = lambda n: n
    common = dict(n_streams=NSA_KV_HEADS, seq=s, out_cols=NSA_HEADS * LANE, qcol=ident, ocol=ident,
                  tq=ATT_ROWS // NSA_GROUP, tk=ATT_TK, tkb=ATT_TKB, G=NSA_GROUP, dq=LANE, bias=tabs["nsa"], bcol=ident,
                  bias_far_zero=True)
    o_s = attention(nqn, ksel, proj, kcol=ident, vcol=lambda n: C_NVS // LANE + n, mode="nsa_sel", dk=2 * LANE,
                    sel=sel, name="nsa_sel_attention", **common)
    o_w = attention(nqn, kwn, proj, kcol=ident, vcol=lambda n: C_NVW // LANE + n, dk=LANE, span=NSA_WINDOW - 1,
                    name="nsa_win_attention", **common)
    return nsa_combine(o_c, o_s, o_w, proj)


def _diff_mixer(proj, l, p, tabs):
    s = proj.shape[0]
    fqn = head_rmsnorm(proj, C_FQ, DIFF_HEADS, p["diff_qk_norm"][l, 0], DIFF_QK, DIFF_QK ** -0.5)
    fkn = head_rmsnorm(proj, C_FK, DIFF_HEADS, p["diff_qk_norm"][l, 1], DIFF_QK, 1.0)
    lam_init = 0.8 - 0.6 * math.exp(-0.3 * l)
    lv = p["diff_lambda"][l].astype(F32)
    lam = jnp.exp(jnp.sum(lv[0] * lv[1])) - jnp.exp(jnp.sum(lv[2] * lv[3])) + lam_init
    ident = lambda n: n
    return attention(fqn, fkn, proj, n_streams=DIFF_HEADS, seq=s, out_cols=DIFF_HEADS * DIFF_V, qcol=ident,
                     kcol=ident, vcol=lambda n: C_FV // LANE + n, ocol=ident, mode="diff", tq=ATT_ROWS // 2,
                     tk=ATT_TK, tkb=ATT_TKB, dq=LANE, dk=LANE, bias=tabs["diff"], bcol=ident, bias_far_zero=True,
                     lam=jnp.full((1, LANE), lam, F32),
                     gout=p["diff_out_norm"][l].reshape(1, LANE).astype(F32), out_scale=1.0 - lam_init,
                     name="diff_attention")


def kernel(x, mem, positions, rel_bias, norm_mix, w_in, mla_q_norm, mla_kv_norm, mla_w_uq, mla_w_ukv, mla_nope_norm, mla_rope_norm, dil_qk_norm, nsa_qk_norm, nsa_cmp_pe, nsa_cmp_w1, nsa_cmp_w2, diff_qk_norm, diff_lambda, diff_out_norm, w_out, norm_xattn, norm_mem, xattn_wq, xattn_wkv, xattn_qk_norm, xattn_wo, norm_ffn, ffn_w_gate, ffn_w_up, ffn_conv_w, ffn_conv_b, ffn_w_down):
    p = dict(mla_q_norm=mla_q_norm, mla_kv_norm=mla_kv_norm, mla_w_uq=mla_w_uq, mla_w_ukv=mla_w_ukv,
             mla_nope_norm=mla_nope_norm, mla_rope_norm=mla_rope_norm, dil_qk_norm=dil_qk_norm,
             nsa_qk_norm=nsa_qk_norm, nsa_cmp_pe=nsa_cmp_pe, nsa_cmp_w1=nsa_cmp_w1, nsa_cmp_w2=nsa_cmp_w2,
             diff_qk_norm=diff_qk_norm, diff_lambda=diff_lambda, diff_out_norm=diff_out_norm)
    b, s, d = x.shape
    assert b == 1
    xs = x.reshape(s, d).astype(F32)
    mems = mem.reshape(mem.shape[1], d).astype(F32)
    pos = positions.reshape(s, 1).astype(jnp.int32)
    tabs = bias_tables(rel_bias, s)
    for l in range(DEPTH):
        h = rmsnorm_rows(xs, norm_mix[l])
        proj = matmul(h, _pack_w_in(w_in[l]), tm=512, tn=1536, out_dtype=BF16, name="in_proj")
        mix = jnp.concatenate([_mla_mixer(proj, pos, l, p), _dilated_mixer(proj, l, p, tabs),
                               _nsa_mixer(proj, l, p, tabs), _diff_mixer(proj, l, p, tabs)], axis=1)
        xs = matmul(mix, w_out[l].astype(BF16), tm=512, tn=1024, out_dtype=F32, residual=xs, name="out_proj")
        hx = rmsnorm_rows(xs, norm_xattn[l])
        qx = matmul(hx, xattn_wq[l].astype(BF16), tm=1024, tn=512, out_dtype=BF16, name="xattn_q")
        kvm = matmul(rmsnorm_rows(mems, norm_mem[l]), xattn_wkv[l].astype(BF16), tm=256, tn=1024, out_dtype=BF16,
                     name="xattn_kv")
        ox = xattn_core(qx, kvm, xattn_qk_norm[l])
        xs = matmul(ox, xattn_wo[l].astype(BF16), tm=1024, tn=1024, out_dtype=F32, residual=xs, name="xattn_o")
        hf = rmsnorm_rows(xs, norm_ffn[l])
        act = ffn_gate_up(hf, ffn_w_gate[l].astype(BF16), ffn_w_up[l].astype(BF16), ffn_conv_w[l], ffn_conv_b[l])
        xs = matmul(act, ffn_w_down[l].astype(BF16), tm=512, tn=512, out_dtype=F32, residual=xs, name="ffn_down")
    return xs.reshape(b, s, d)
```

```python
import functools
import math

import numpy as np
import jax
import jax.numpy as jnp
from jax import lax
from jax.experimental import pallas as pl
from jax.experimental.pallas import tpu as pltpu

F32 = jnp.float32
BF16 = jnp.bfloat16

D_MODEL = 4096
DEPTH = 2
HEAD_DIM = 128
EPS = 1e-6
MLA_HEADS = 8
MLA_Q_RANK = 896
MLA_KV_RANK = 512
MLA_NOPE = 128
MLA_ROPE = 64
MLA_V = 128
ROPE_THETA = 10000.0
DIL_PATTERNS = ((128, 1), (512, 4), (2048, 16))
DIL_HP = 3
DIL_HEADS = DIL_HP * len(DIL_PATTERNS)
NSA_HEADS = 8
NSA_KV_HEADS = 2
NSA_GROUP = NSA_HEADS // NSA_KV_HEADS
NSA_CMP_BLOCK = 32
NSA_CMP_STRIDE = 16
NSA_SLC_BLOCK = 64
NSA_TOPK = 16
NSA_WINDOW = 512
DIFF_HEADS = 7
DIFF_QK = 64
DIFF_V = 128
XATTN_HEADS = 4
FFN_DIM = 11008
REL_BUCKETS = 32
REL_MAX_DIST = 2048
BIAS_DIL = 0
BIAS_NSA = DIL_HEADS
BIAS_DIFF = DIL_HEADS + NSA_HEADS

LANE = 128
VMEM_LIMIT = 56 * 1024 * 1024
NEG = -1e30
SEL_NEG = -float(2 ** 30)
SEL_HALF_BLOCKS = 128
ATT_ROWS = 1024
ATT_TK = 1024
ATT_TKB = 2048
DIL_TQ = 512

C_CQ = 0
C_KR = 896
C_CKV = 1024
C_DQ = 1536
C_DK = 2688
C_DV = 3840
C_NQ = 4992
C_NKC = 6016
C_NVC = 6272
C_NKS = 6528
C_NVS = 6784
C_NKW = 7040
C_NVW = 7296
C_NG = 7552
C_FQ = 7680
C_FK = 8576
C_FV = 9472
NP_COLS = 10752


def _cp(sem):
    return pltpu.CompilerParams(dimension_semantics=sem, vmem_limit_bytes=VMEM_LIMIT)


def _rmsnorm_rows_kernel(x_ref, g_ref, o_ref):
    x = x_ref[...].astype(F32)
    ms = jnp.mean(x * x, axis=-1, keepdims=True)
    o_ref[...] = (x * lax.rsqrt(ms + EPS) * g_ref[...]).astype(o_ref.dtype)


def rmsnorm_rows(x, g, tm=256):
    m, d = x.shape
    tm = min(tm, m)
    return pl.pallas_call(
        _rmsnorm_rows_kernel,
        grid=(m // tm,),
        in_specs=[pl.BlockSpec((tm, d), lambda i: (i, 0)),
                  pl.BlockSpec((1, d), lambda i: (0, 0))],
        out_specs=pl.BlockSpec((tm, d), lambda i: (i, 0)),
        out_shape=jax.ShapeDtypeStruct((m, d), BF16),
        compiler_params=_cp(("parallel",)),
        name="rmsnorm_rows",
    )(x, g.reshape(1, d).astype(F32))


def _matmul_kernel(*refs, has_res):
    if has_res:
        a_ref, w_ref, r_ref, o_ref = refs
    else:
        a_ref, w_ref, o_ref = refs
    acc = jnp.dot(a_ref[...], w_ref[...], preferred_element_type=F32)
    if has_res:
        acc = acc + r_ref[...]
    o_ref[...] = acc.astype(o_ref.dtype)


def _matmul_cat_kernel(*refs, offs):
    a_refs = refs[:len(offs)]
    w_ref, r_ref, o_ref = refs[len(offs):]
    acc = r_ref[...]
    for a_ref, (k0, k1) in zip(a_refs, offs):
        acc = acc + jnp.dot(a_ref[...], w_ref[k0:k1, :], preferred_element_type=F32)
    o_ref[...] = acc.astype(o_ref.dtype)


def matmul_cat(a_list, w, residual, *, tm, tn, name="matmul_cat"):
    m = a_list[0].shape[0]
    k, n = w.shape
    offs, k0 = [], 0
    for a in a_list:
        offs.append((k0, k0 + a.shape[1]))
        k0 += a.shape[1]
    tm = min(tm, m)
    assert k0 == k and m % tm == 0 and n % tn == 0
    in_specs = [pl.BlockSpec((tm, a.shape[1]), lambda j, i: (i, 0)) for a in a_list]
    in_specs += [pl.BlockSpec((k, tn), lambda j, i: (0, j)), pl.BlockSpec((tm, tn), lambda j, i: (i, j))]
    return pl.pallas_call(
        functools.partial(_matmul_cat_kernel, offs=tuple(offs)),
        grid=(n // tn, m // tm),
        in_specs=in_specs,
        out_specs=pl.BlockSpec((tm, tn), lambda j, i: (i, j)),
        out_shape=jax.ShapeDtypeStruct((m, n), F32),
        compiler_params=_cp(("parallel", "parallel")),
        name=name,
    )(*a_list, w, residual)


def matmul(a, w, *, tm, tn, out_dtype, residual=None, name="matmul"):
    m, k = a.shape
    n = w.shape[1]
    tm = min(tm, m)
    tn = min(tn, n)
    assert m % tm == 0 and n % tn == 0
    in_specs = [pl.BlockSpec((tm, k), lambda j, i: (i, 0)),
                pl.BlockSpec((k, tn), lambda j, i: (0, j))]
    args = [a, w]
    if residual is not None:
        in_specs.append(pl.BlockSpec((tm, tn), lambda j, i: (i, j)))
        args.append(residual)
    return pl.pallas_call(
        functools.partial(_matmul_kernel, has_res=residual is not None),
        grid=(n // tn, m // tm),
        in_specs=in_specs,
        out_specs=pl.BlockSpec((tm, tn), lambda j, i: (i, j)),
        out_shape=jax.ShapeDtypeStruct((m, n), out_dtype),
        compiler_params=_cp(("parallel", "parallel")),
        name=name,
    )(*args)


def _ffn_gate_up_kernel(h_ref, wg_ref, wu_ref, cw_ref, cb_ref, o_ref, carry_ref):
    i = pl.program_id(1)

    @pl.when(i == 0)
    def _():
        carry_ref[...] = jnp.zeros_like(carry_ref)

    h = h_ref[...]
    u = jnp.dot(h, wg_ref[...], preferred_element_type=F32)
    up = jnp.dot(h, wu_ref[...], preferred_element_type=F32)
    tm = u.shape[0]
    prev = carry_ref[...]
    row = lax.broadcasted_iota(jnp.int32, u.shape, 0)
    u1 = jnp.where(row == 0, prev[7:8, :], pltpu.roll(u, 1, 0))
    u2 = pltpu.roll(u, 2, 0)
    u2 = jnp.where(row == 0, prev[6:7, :], jnp.where(row == 1, prev[7:8, :], u2))
    cw = cw_ref[...]
    g = cw[0:1, :] * u2 + cw[1:2, :] * u1 + cw[2:3, :] * u + cb_ref[...]
    o_ref[...] = (g * jax.nn.sigmoid(g) * up).astype(o_ref.dtype)
    carry_ref[...] = u[tm - 8:, :]


def ffn_gate_up(h, wg, wu, conv_w, conv_b, tm=1024, tf=256):
    m, d = h.shape
    f = wg.shape[1]
    tm = min(tm, m)
    assert f % tf == 0 and m % tm == 0
    return pl.pallas_call(
        _ffn_gate_up_kernel,
        grid=(f // tf, m // tm),
        in_specs=[pl.BlockSpec((tm, d), lambda j, i: (i, 0)),
                  pl.BlockSpec((d, tf), lambda j, i: (0, j)),
                  pl.BlockSpec((d, tf), lambda j, i: (0, j)),
                  pl.BlockSpec((3, tf), lambda j, i: (0, j)),
                  pl.BlockSpec((1, tf), lambda j, i: (0, j))],
        out_specs=pl.BlockSpec((tm, tf), lambda j, i: (i, j)),
        out_shape=jax.ShapeDtypeStruct((m, f), BF16),
        scratch_shapes=[pltpu.VMEM((8, tf), F32)],
        compiler_params=_cp(("arbitrary", "arbitrary")),
        name="ffn_gate_up",
    )(h, wg, wu, conv_w.astype(F32), conv_b.reshape(1, f).astype(F32))


def _group_norm(x, d):
    sq = x * x
    tot = jnp.sum(sq, axis=-1, keepdims=True)
    if d == LANE:
        ms = tot / d
    else:
        lane = lax.broadcasted_iota(jnp.int32, x.shape, 1)
        lo = jnp.sum(jnp.where(lane < d, sq, 0.0), axis=-1, keepdims=True)
        ms = jnp.where(lane < d, lo, tot - lo) / d
    return x * lax.rsqrt(ms + EPS)


def _head_norm_kernel(x_ref, g_ref, o_ref, *, d, scale):
    x = x_ref[...].astype(F32)
    o_ref[...] = (_group_norm(x, d) * (g_ref[...] * scale)).astype(o_ref.dtype)


def head_rmsnorm(src, col0, ncb, gain, d, scale, tm=2048):
    s = src.shape[0]
    tm = min(tm, s)
    cb0 = col0 // LANE
    g = jnp.tile(gain.astype(F32), LANE // d).reshape(1, LANE)
    return pl.pallas_call(
        functools.partial(_head_norm_kernel, d=d, scale=scale),
        grid=(s // tm, ncb),
        in_specs=[pl.BlockSpec((tm, LANE), lambda i, c: (i, cb0 + c)),
                  pl.BlockSpec((1, LANE), lambda i, c: (0, 0))],
        out_specs=pl.BlockSpec((tm, LANE), lambda i, c: (i, c)),
        out_shape=jax.ShapeDtypeStruct((s, ncb * LANE), BF16),
        compiler_params=_cp(("parallel", "parallel")),
        name="head_rmsnorm",
    )(src, g)


def _nsa_ksel_kernel(x_ref, g_ref, o_ref):
    tm = x_ref.shape[0]
    x = x_ref[...].astype(F32)
    o_ref[:, :LANE] = (_group_norm(x, LANE) * g_ref[...]).astype(o_ref.dtype)
    row = pl.program_id(0) * tm + lax.broadcasted_iota(jnp.int32, (tm, LANE), 0)
    col = lax.broadcasted_iota(jnp.int32, (tm, LANE), 1)
    hot = ((row // NSA_SLC_BLOCK) % SEL_HALF_BLOCKS) == col
    o_ref[:, LANE:] = jnp.where(hot, 1.0, 0.0).astype(o_ref.dtype)


def nsa_sel_keys(proj, gain, tm=2048):
    s = proj.shape[0]
    tm = min(tm, s)
    cb0 = C_NKS // LANE
    return pl.pallas_call(
        _nsa_ksel_kernel,
        grid=(s // tm, NSA_KV_HEADS),
        in_specs=[pl.BlockSpec((tm, LANE), lambda i, c: (i, cb0 + c)),
                  pl.BlockSpec((1, LANE), lambda i, c: (0, 0))],
        out_specs=pl.BlockSpec((tm, 2 * LANE), lambda i, c: (i, c)),
        out_shape=jax.ShapeDtypeStruct((s, NSA_KV_HEADS * 2 * LANE), BF16),
        compiler_params=_cp(("parallel", "parallel")),
        name="nsa_sel_keys",
    )(proj, gain.reshape(1, LANE).astype(F32))


def _rope_tables(pos_ref, inv_ref, sign_ref):
    ang = pos_ref[...].astype(F32) * inv_ref[...]
    return jnp.cos(ang), jnp.sin(ang) * sign_ref[...]


def _rope_apply(x, cosv, sin_signed):
    lane = lax.broadcasted_iota(jnp.int32, x.shape, 1)
    half = MLA_ROPE // 2
    swapped = jnp.where(lane < half, pltpu.roll(x, LANE - half, 1), pltpu.roll(x, half, 1))
    return x * cosv + swapped * sin_signed


def _latent_norm(c_ref, g_ref):
    c = c_ref[...].astype(F32)
    ms = jnp.mean(c * c, axis=-1, keepdims=True)
    return (c * lax.rsqrt(ms + EPS) * g_ref[...]).astype(BF16)


def _mla_q_kernel(c_ref, pos_ref, gq_ref, w_ref, gn_ref, gr_ref, inv_ref, sign_ref, o_ref, *, scale):
    u = jnp.dot(_latent_norm(c_ref, gq_ref), w_ref[...], preferred_element_type=F32)
    cosv, sinv = _rope_tables(pos_ref, inv_ref, sign_ref)
    for h in range(MLA_HEADS):
        b = 2 * LANE * h
        nope = _group_norm(u[:, b:b + LANE], LANE) * (gn_ref[...] * scale)
        rp = u[:, b + LANE:b + 2 * LANE]
        rp = rp * lax.rsqrt(jnp.sum(rp * rp, axis=-1, keepdims=True) / MLA_ROPE + EPS) * gr_ref[...]
        rp = _rope_apply(rp, cosv, sinv) * scale
        o_ref[:, b:b + LANE] = nope.astype(o_ref.dtype)
        o_ref[:, b + LANE:b + 2 * LANE] = rp.astype(o_ref.dtype)


def _mla_kv_kernel(c_ref, kr_ref, pos_ref, gkv_ref, w_ref, gn_ref, gr_ref, inv_ref, sign_ref, k_ref, v_ref):
    kv = jnp.dot(_latent_norm(c_ref, gkv_ref), w_ref[...], preferred_element_type=F32)
    cosv, sinv = _rope_tables(pos_ref, inv_ref, sign_ref)
    kr = kr_ref[...].astype(F32)
    kr = kr * lax.rsqrt(jnp.sum(kr * kr, axis=-1, keepdims=True) / MLA_ROPE + EPS) * gr_ref[...]
    kr = _rope_apply(kr, cosv, sinv).astype(k_ref.dtype)
    nv = MLA_HEADS * LANE
    for h in range(MLA_HEADS):
        kn = _group_norm(kv[:, h * LANE:(h + 1) * LANE], LANE) * gn_ref[...]
        k_ref[:, 2 * LANE * h:2 * LANE * h + LANE] = kn.astype(k_ref.dtype)
        k_ref[:, 2 * LANE * h + LANE:2 * LANE * (h + 1)] = kr
    v_ref[...] = kv[:, nv:].astype(v_ref.dtype)


def _rope_consts():
    half = MLA_ROPE // 2
    inv = ROPE_THETA ** (-jnp.arange(half, dtype=F32) / half)
    inv = jnp.concatenate([inv, inv, jnp.zeros((LANE - MLA_ROPE,), F32)]).reshape(1, LANE)
    sign = np.zeros((1, LANE), np.float32)
    sign[0, :half] = -1.0
    sign[0, half:MLA_ROPE] = 1.0
    return inv, jnp.asarray(sign)


def _pad_lanes(v, n=LANE):
    return jnp.concatenate([v.astype(F32), jnp.zeros((n - v.shape[0],), F32)]).reshape(1, n)


def mla_q_proj(proj, pos, q_norm, w_uq, nope_gain, rope_gain, tm=512):
    s = proj.shape[0]
    tm = min(tm, s)
    inv, sign = _rope_consts()
    w = jnp.pad(w_uq.reshape(MLA_Q_RANK, MLA_HEADS, MLA_NOPE + MLA_ROPE),
                ((0, 0), (0, 0), (0, LANE - MLA_ROPE))).reshape(MLA_Q_RANK, MLA_HEADS * 2 * LANE).astype(BF16)
    wq = MLA_HEADS * 2 * LANE
    const = lambda i: (0, 0)
    return pl.pallas_call(
        functools.partial(_mla_q_kernel, scale=(MLA_NOPE + MLA_ROPE) ** -0.5),
        grid=(s // tm,),
        in_specs=[pl.BlockSpec((tm, MLA_Q_RANK), lambda i: (i, C_CQ // MLA_Q_RANK)),
                  pl.BlockSpec((tm, 1), lambda i: (i, 0)),
                  pl.BlockSpec((1, MLA_Q_RANK), const),
                  pl.BlockSpec((MLA_Q_RANK, wq), const),
                  pl.BlockSpec((1, LANE), const), pl.BlockSpec((1, LANE), const),
                  pl.BlockSpec((1, LANE), const), pl.BlockSpec((1, LANE), const)],
        out_specs=pl.BlockSpec((tm, wq), lambda i: (i, 0)),
        out_shape=jax.ShapeDtypeStruct((s, wq), BF16),
        compiler_params=_cp(("parallel",)),
        name="mla_q_proj",
    )(proj, pos, q_norm.reshape(1, -1).astype(F32), w, nope_gain.reshape(1, LANE).astype(F32),
      _pad_lanes(rope_gain), inv, sign)


def mla_kv_proj(proj, pos, kv_norm, w_ukv, nope_gain, rope_gain, tm=512):
    s = proj.shape[0]
    tm = min(tm, s)
    inv, sign = _rope_consts()
    w = jnp.transpose(w_ukv.reshape(MLA_KV_RANK, MLA_HEADS, 2, LANE), (0, 2, 1, 3))
    w = w.reshape(MLA_KV_RANK, 2 * MLA_HEADS * LANE).astype(BF16)
    wk = MLA_HEADS * 2 * LANE
    wv = MLA_HEADS * MLA_V
    const = lambda i: (0, 0)
    return pl.pallas_call(
        _mla_kv_kernel,
        grid=(s // tm,),
        in_specs=[pl.BlockSpec((tm, MLA_KV_RANK), lambda i: (i, C_CKV // MLA_KV_RANK)),
                  pl.BlockSpec((tm, LANE), lambda i: (i, C_KR // LANE)),
                  pl.BlockSpec((tm, 1), lambda i: (i, 0)),
                  pl.BlockSpec((1, MLA_KV_RANK), const),
                  pl.BlockSpec((MLA_KV_RANK, 2 * MLA_HEADS * LANE), const),
                  pl.BlockSpec((1, LANE), const), pl.BlockSpec((1, LANE), const),
                  pl.BlockSpec((1, LANE), const), pl.BlockSpec((1, LANE), const)],
        out_specs=[pl.BlockSpec((tm, wk), lambda i: (i, 0)),
                   pl.BlockSpec((tm, wv), lambda i: (i, 0))],
        out_shape=[jax.ShapeDtypeStruct((s, wk), BF16), jax.ShapeDtypeStruct((s, wv), BF16)],
        compiler_params=_cp(("parallel",)),
        name="mla_kv_proj",
    )(proj, proj, pos, kv_norm.reshape(1, -1).astype(F32), w, nope_gain.reshape(1, LANE).astype(F32),
      _pad_lanes(rope_gain), inv, sign)


def _rel_bucket_np(dist):
    n = np.maximum(dist, 0)
    exact = REL_BUCKETS // 2
    ratio = np.log(np.maximum(n, 1).astype(np.float32) / np.float32(exact)) / np.float32(math.log(REL_MAX_DIST / exact))
    large = np.minimum(exact + (ratio.astype(np.float32) * (REL_BUCKETS - exact)).astype(np.int32), REL_BUCKETS - 1)
    return np.where(n < exact, n, large).astype(np.int32)


def _strip_x0(tk):
    return -(-(REL_MAX_DIST - 1 + tk) // LANE) * LANE


def bias_strips(table, tq, tk, tkb, dist_scale):
    x0 = _strip_x0(tkb)
    width = x0 + max(tq, tk)
    period = -(-(width + tq) // LANE) * LANE
    t = np.arange(period)
    t = np.where(t < width, t, t - period)
    u = table.astype(F32)[_rel_bucket_np((x0 - t) * dist_scale)].T
    h = u.shape[0]
    return pl.pallas_call(
        functools.partial(_toeplitz_kernel, tq=tq, width=width),
        grid=(h,),
        in_specs=[pl.BlockSpec((None, 1, period), lambda n: (n, 0, 0))],
        out_specs=pl.BlockSpec((None, tq, width), lambda n: (n, 0, 0)),
        out_shape=jax.ShapeDtypeStruct((h, tq, width), F32),
        compiler_params=_cp(("parallel",)),
        name="bias_strips",
    )(u.reshape(h, 1, period))


def _toeplitz_kernel(u_ref, o_ref, *, tq, width):
    rows = jnp.broadcast_to(u_ref[...], (tq, u_ref.shape[1]))
    o_ref[...] = pltpu.roll(rows, 0, 1, stride=1, stride_axis=0)[:, :width]


def _attn_kernel(*refs, mode, tq, tk, tkb, G, dq, dv, span, has_bias, far_zero, x0, n_halves, want_lse, out_scale):
    refs = list(refs)
    q_ref = refs.pop(0)
    sel_ref = refs.pop(0) if mode == "nsa_sel" else None
    k_ref = refs.pop(0)
    v_ref = refs.pop(0)
    bias_ref = refs.pop(0) if has_bias else None
    lam_ref = gout_ref = None
    if mode == "diff":
        lam_ref = refs.pop(0)
        gout_ref = refs.pop(0)
    o_ref = refs.pop(0)
    lse_ref = refs.pop(0) if want_lse else None
    m_sc, l_sc, acc_sc = refs

    qi = pl.program_id(1)
    q0 = qi * tq
    n_tiles = k_ref.shape[0] // tk

    if mode == "diff":
        q = q_ref[...]
        lane = lax.broadcasted_iota(jnp.int32, q.shape, 1)
        zero = jnp.zeros_like(q)
        lhs = [jnp.concatenate([jnp.where(lane < DIFF_QK, q, zero), jnp.where(lane >= DIFF_QK, q, zero)], axis=0)]
    elif mode == "nsa_sel":
        lhs = []
        for hf in range(n_halves):
            sel = sel_ref[hf]
            lhs.append(jnp.concatenate(
                [jnp.concatenate([q_ref[:, g * dq:(g + 1) * dq], sel], axis=1) for g in range(G)], axis=0))
    else:
        lhs = [jnp.concatenate([q_ref[:, g * dq:(g + 1) * dq] for g in range(G)], axis=0) if G > 1 else q_ref[...]]

    m_sc[...] = jnp.full_like(m_sc, NEG)
    l_sc[...] = jnp.zeros_like(l_sc)
    acc_sc[...] = jnp.zeros_like(acc_sc)

    def logits(k0, w, Q, masked, biased):
        kt = k_ref[pl.ds(k0, w), :]
        s = lax.dot_general(Q, kt, (((1,), (1,)), ((), ())), preferred_element_type=F32)
        has_bias = biased
        if has_bias or masked:
            if masked:
                rel = (q0 - k0) + lax.broadcasted_iota(jnp.int32, (tq, w), 0) - lax.broadcasted_iota(jnp.int32, (tq, w), 1)
                vis = rel >= 0
                if span is not None:
                    vis = jnp.logical_and(vis, rel <= span)
            if has_bias:
                start = pl.multiple_of(jnp.maximum(x0 - (q0 - k0), 0), LANE)
            parts = []
            for g in range(G):
                sg = s[g * tq:(g + 1) * tq]
                if has_bias:
                    sg = sg + bias_ref[0 if mode == "diff" else g, :, pl.ds(start, w)]
                if masked:
                    sg = jnp.where(vis, sg, NEG)
                parts.append(sg)
            s = jnp.concatenate(parts, axis=0) if G > 1 else parts[0]
        return s

    def update(s, k0, w):
        vt = v_ref[pl.ds(k0, w), :]
        m_prev = m_sc[...]
        m_new = jnp.maximum(m_prev, jnp.max(s, axis=-1, keepdims=True))
        alpha = jnp.exp(m_prev - m_new)
        p = jnp.exp(s - jnp.tile(m_new, (1, w // LANE)))
        l_sc[...] = alpha * l_sc[...] + jnp.sum(p, axis=-1, keepdims=True)
        acc_sc[...] = alpha * acc_sc[...] + jnp.dot(p.astype(vt.dtype), vt, preferred_element_type=F32)
        m_sc[...] = m_new

    def tile(k0, w, Q, masked, biased):
        if w == tkb and tkb > tk:
            h = w // 2
            sa = logits(k0, h, Q, masked, biased)
            sb = logits(k0 + h, h, Q, masked, biased)
            update(sa, k0, h)
            update(sb, k0 + h, h)
        else:
            update(logits(k0, w, Q, masked, biased), k0, w)

    def run(lo, hi, w, Q, masked, biased=has_bias):
        def body(j, c):
            tile(pl.multiple_of(j * w, w), w, Q, masked, biased)
            return c
        lax.fori_loop(lo, hi, body, 0)

    def run_visible(lo, hi, Q):
        if tkb == tk:
            run(lo, hi, tk, Q, False)
            return
        r = tkb // tk
        up = jnp.minimum(-(-lo // r) * r, hi)
        dn = jnp.maximum((hi // r) * r, up)
        if not (isinstance(lo, int) and lo % r == 0):
            run(lo, up, tk, Q, False)
        if has_bias and far_zero:
            far = jnp.clip(jnp.maximum(q0 - (REL_MAX_DIST - 1), 0) // tkb, up // r, dn // r)
            run(up // r, far, tkb, Q, False, False)
            run(far, dn // r, tkb, Q, False)
        else:
            run(up // r, dn // r, tkb, Q, False)
        run(dn, hi, tk, Q, False)

    j_hi = jnp.minimum((q0 + tq - 1) // tk, n_tiles - 1)
    b = (q0 + 1) // tk
    if span is None:
        j_lo = 0
        a = 0
    else:
        j_lo = jnp.maximum(q0 - span, 0) // tk
        a = jnp.maximum(q0 + tq - 1 - span + tk - 1, 0) // tk
        a = jnp.minimum(jnp.maximum(a, j_lo), b)
    tph = (SEL_HALF_BLOCKS * NSA_SLC_BLOCK) // tk
    for hf in range(n_halves):
        Q = lhs[hf]
        if mode == "nsa_sel":
            clip = lambda lo, hi: (jnp.maximum(lo, hf * tph) if hf else lo, jnp.minimum(hi, (hf + 1) * tph))
        else:
            clip = lambda lo, hi: (lo, hi)
        if span is not None:
            run(*clip(j_lo, a), tk, Q, True)
        run_visible(*clip(a, b), Q)
        run(*clip(b, j_hi + 1), tk, Q, True)

    l = l_sc[...]
    acc = acc_sc[...] / l
    if mode == "diff":
        o = acc[:tq] - lam_ref[...] * acc[tq:]
        o = o * lax.rsqrt(jnp.mean(o * o, axis=-1, keepdims=True) + EPS) * (gout_ref[...] * out_scale)
        o_ref[...] = o.astype(o_ref.dtype)
    else:
        for g in range(G):
            o_ref[:, g * dv:(g + 1) * dv] = acc[g * tq:(g + 1) * tq].astype(o_ref.dtype)
        if want_lse:
            lse = m_sc[...] + jnp.log(l)
            for g in range(G):
                lse_ref[:, g * LANE:(g + 1) * LANE] = lse[g * tq:(g + 1) * tq]


def attention(q, k, v, *, n_streams, seq, out_cols, qcol, kcol, vcol, ocol, mode="plain", tq, tk, tkb=None, G=1,
              dq, dk, dv=LANE, span=None, bias=None, bcol=None, sel=None, lam=None, gout=None, out_scale=1.0,
              want_lse=False, bias_far_zero=False, name="attention"):
    tq = min(tq, seq)
    tk = min(tk, seq)
    tkb = tk if tkb is None else min(tkb, seq)
    assert seq % tq == 0 and seq % tkb == 0 and tkb % tk == 0 and tq % LANE == 0 and tk % LANE == 0
    resident = dict(pipeline_mode=pl.Buffered(1))
    R = 2 * tq if mode == "diff" else G * tq
    Gk = 2 if mode == "diff" else G
    wq = dq if mode == "diff" else G * dq
    in_specs = [pl.BlockSpec((tq, wq), lambda n, i: (i, qcol(n)))]
    args = [q]
    n_halves = 1
    if mode == "nsa_sel":
        n_halves = sel.shape[0]
        in_specs.append(pl.BlockSpec((n_halves, tq, LANE), lambda n, i: (0, i, n)))
        args.append(sel)
    in_specs += [pl.BlockSpec((seq, dk), lambda n, i: (0, kcol(n)), **resident),
                 pl.BlockSpec((seq, dv), lambda n, i: (0, vcol(n)), **resident)]
    args += [k, v]
    x0 = 0
    if bias is not None:
        x0 = _strip_x0(tkb)
        gb = 1 if mode == "diff" else G
        in_specs.append(pl.BlockSpec((gb, tq, x0 + max(tq, tk)), lambda n, i: (bcol(n), 0, 0), **resident))
        args.append(bias)
    if mode == "diff":
        in_specs += [pl.BlockSpec((1, LANE), lambda n, i: (0, 0)), pl.BlockSpec((1, LANE), lambda n, i: (0, 0))]
        args += [lam, gout]
    wo = dv if mode == "diff" else G * dv
    out_specs = [pl.BlockSpec((tq, wo), lambda n, i: (i, ocol(n)))]
    out_shape = [jax.ShapeDtypeStruct((seq, out_cols), BF16)]
    if want_lse:
        out_specs.append(pl.BlockSpec((tq, G * LANE), lambda n, i: (i, ocol(n))))
        out_shape.append(jax.ShapeDtypeStruct((seq, out_cols), F32))
    kern = functools.partial(_attn_kernel, mode=mode, tq=tq, tk=tk, tkb=tkb, G=Gk, dq=dq, dv=dv, span=span,
                             has_bias=bias is not None, far_zero=bias_far_zero, x0=x0, n_halves=n_halves,
                             want_lse=want_lse,
                             out_scale=out_scale)
    res = pl.pallas_call(
        kern,
        grid=(n_streams, seq // tq),
        in_specs=in_specs,
        out_specs=out_specs,
        out_shape=out_shape,
        scratch_shapes=[pltpu.VMEM((R, LANE), F32), pltpu.VMEM((R, LANE), F32), pltpu.VMEM((R, dv), F32)],
        compiler_params=_cp(("parallel", "parallel")),
        name=name,
    )(*args)
    return res if want_lse else res[0]


def _banded_kernel(q_ref, k_ref, v_ref, bias_ref, o_ref, lse_ref, m_sc, l_sc, acc_sc, *, heads, tq, span, x0):
    q0 = pl.program_id(1) * tq
    tk = tq
    m_sc[...] = jnp.full_like(m_sc, NEG)
    l_sc[...] = jnp.zeros_like(l_sc)
    acc_sc[...] = jnp.zeros_like(acc_sc)

    def body(j, c):
        k0 = pl.multiple_of(j * tk, tk)
        rel = (q0 - k0) + lax.broadcasted_iota(jnp.int32, (tq, tk), 0) - lax.broadcasted_iota(jnp.int32, (tq, tk), 1)
        vis = jnp.logical_and(rel >= 0, rel <= span)
        start = pl.multiple_of(jnp.maximum(x0 - (q0 - k0), 0), LANE)
        for h in range(heads):
            cs = slice(h * LANE, (h + 1) * LANE)
            s = lax.dot_general(q_ref[:, cs], k_ref[pl.ds(k0, tk), cs], (((1,), (1,)), ((), ())),
                                preferred_element_type=F32)
            s = jnp.where(vis, s + bias_ref[h, :, pl.ds(start, tk)], NEG)
            m_prev = m_sc[h]
            m_new = jnp.maximum(m_prev, jnp.max(s, axis=-1, keepdims=True))
            alpha = jnp.exp(m_prev - m_new)
            p = jnp.exp(s - jnp.tile(m_new, (1, tk // LANE)))
            l_sc[h] = alpha * l_sc[h] + jnp.sum(p, axis=-1, keepdims=True)
            vt = v_ref[pl.ds(k0, tk), cs]
            acc_sc[h] = alpha * acc_sc[h] + jnp.dot(p.astype(vt.dtype), vt, preferred_element_type=F32)
            m_sc[h] = m_new
        return c

    lax.fori_loop(jnp.maximum(q0 - span, 0) // tk, (q0 + tq - 1) // tk + 1, body, 0)
    for h in range(heads):
        cs = slice(h * LANE, (h + 1) * LANE)
        l = l_sc[h]
        o_ref[:, cs] = (acc_sc[h] / l).astype(o_ref.dtype)
        lse_ref[:, cs] = m_sc[h] + jnp.log(l)


def banded_attention(q, k, v, bias, *, streams, seq, tq, span, qcol, vcol, name):
    heads = bias.shape[0]
    width = heads * LANE
    tq = min(tq, seq)
    assert seq % tq == 0
    x0 = _strip_x0(tq)
    resident = dict(pipeline_mode=pl.Buffered(1))
    return pl.pallas_call(
        functools.partial(_banded_kernel, heads=heads, tq=tq, span=span, x0=x0),
        grid=(streams, seq // tq),
        in_specs=[pl.BlockSpec((tq, width), lambda r, i: (i, qcol(r))),
                  pl.BlockSpec((seq, width), lambda r, i: (0, qcol(r)), **resident),
                  pl.BlockSpec((seq, width), lambda r, i: (0, vcol(r)), **resident),
                  pl.BlockSpec((heads, tq, x0 + tq), lambda r, i: (0, 0, 0), **resident)],
        out_specs=[pl.BlockSpec((tq, width), lambda r, i: (i, r)),
                   pl.BlockSpec((tq, width), lambda r, i: (i, r))],
        out_shape=[jax.ShapeDtypeStruct((seq, streams * width), BF16),
                   jax.ShapeDtypeStruct((seq, streams * width), F32)],
        scratch_shapes=[pltpu.VMEM((heads, tq, LANE), F32), pltpu.VMEM((heads, tq, LANE), F32),
                        pltpu.VMEM((heads, tq, LANE), F32)],
        compiler_params=_cp(("parallel", "parallel")),
        name=name,
    )(q, k, v, bias)


def _dil_combine_kernel(o0, o1, o2, l0, l1, l2, out_ref):
    os_ = (o0, o1, o2)
    ls_ = (l0, l1, l2)
    for j in range(DIL_HP):
        sl = slice(j * LANE, (j + 1) * LANE)
        lse = [r[:, sl] for r in ls_]
        m = jnp.maximum(jnp.maximum(lse[0], lse[1]), lse[2])
        e = [jnp.exp(x - m) for x in lse]
        inv = 1.0 / (e[0] + e[1] + e[2])
        for g in range(len(DIL_PATTERNS)):
            c = (g * DIL_HP + j) * LANE
            out_ref[:, c:c + LANE] = (os_[g][:, sl].astype(F32) * (e[g] * inv)).astype(out_ref.dtype)


def dil_combine(outs, lses, tm=512):
    s = outs[0].shape[0]
    tm = min(tm, s)
    w = DIL_HP * LANE
    spec = pl.BlockSpec((tm, w), lambda i: (i, 0))
    return pl.pallas_call(
        _dil_combine_kernel,
        grid=(s // tm,),
        in_specs=[spec] * 6,
        out_specs=pl.BlockSpec((tm, DIL_HEADS * LANE), lambda i: (i, 0)),
        out_shape=jax.ShapeDtypeStruct((s, DIL_HEADS * LANE), BF16),
        compiler_params=_cp(("parallel",)),
        name="dil_combine",
    )(*outs, *lses)


def _gelu_tanh(y):
    return 0.5 * y * (1.0 + jnp.tanh(0.7978845608028654 * (y + 0.044715 * y * y * y)))


def _compress_kernel(a_ref, pe_ref, w1_ref, w2_ref, g_ref, o_ref, *, norm, transpose):
    a = a_ref[...].astype(F32)
    nc = a.shape[0]
    lo = jnp.dot((a + pe_ref[0]).astype(BF16), w1_ref[0], preferred_element_type=F32)
    hi = jnp.dot((a + pe_ref[1]).astype(BF16), w1_ref[1], preferred_element_type=F32)
    y = lo + pltpu.roll(hi, nc - 1, 0)
    z = jnp.dot(_gelu_tanh(y).astype(BF16), w2_ref[...], preferred_element_type=F32)
    if norm:
        z = _group_norm(z, LANE) * g_ref[...]
    o_ref[...] = (z.T if transpose else z).astype(o_ref.dtype)


def nsa_compress(chunks, pe, w1, w2, gain, *, norm, transpose):
    kvh, nc, w = chunks.shape
    half = NSA_CMP_STRIDE * LANE
    pe2 = pe.astype(F32).reshape(2, 1, half)
    w1s = w1.reshape(2, half, LANE).astype(BF16)
    oshape = (kvh, LANE, nc) if transpose else (kvh, nc, LANE)
    oblock = (None, LANE, nc) if transpose else (None, nc, LANE)
    return pl.pallas_call(
        functools.partial(_compress_kernel, norm=norm, transpose=transpose),
        grid=(kvh,),
        in_specs=[pl.BlockSpec((None, nc, w), lambda h: (h, 0, 0)),
                  pl.BlockSpec((2, 1, half), lambda h: (0, 0, 0)),
                  pl.BlockSpec((2, half, LANE), lambda h: (0, 0, 0)),
                  pl.BlockSpec((LANE, LANE), lambda h: (0, 0)),
                  pl.BlockSpec((1, LANE), lambda h: (0, 0))],
        out_specs=pl.BlockSpec(oblock, lambda h: (h, 0, 0)),
        out_shape=jax.ShapeDtypeStruct(oshape, BF16),
        compiler_params=_cp(("parallel",)),
        name="nsa_compress",
    )(chunks, pe2, w1s, w2.astype(BF16), gain.reshape(1, LANE).astype(F32))


def _cmp_topk_kernel(q_ref, kc_ref, vt_ref, ov_ref, strip_ref, o_ref, sel_ref, *, tq, y0, k_sel, n_halves):
    qi = pl.program_id(1)
    q0 = qi * tq
    G = NSA_GROUP
    nc = kc_ref.shape[0]
    qs = jnp.concatenate([q_ref[:, g * LANE:(g + 1) * LANE] for g in range(G)], axis=0)
    st = lax.dot_general(kc_ref[...], qs, (((1,), (1,)), ((), ())), preferred_element_type=F32)
    n_io = lax.broadcasted_iota(jnp.int32, (nc, tq), 0)
    i_io = lax.broadcasted_iota(jnp.int32, (nc, tq), 1)
    vis = (q0 + i_io - NSA_CMP_STRIDE * n_io - (NSA_CMP_BLOCK - 1)) >= 0
    start = pl.multiple_of(y0 - qi * (tq // NSA_CMP_STRIDE), 8)
    psum = jnp.zeros((nc, tq), F32)
    pts = []
    for g in range(G):
        s = st[:, g * tq:(g + 1) * tq] + strip_ref[g, pl.ds(start, nc), :]
        s = jnp.where(vis, s, NEG)
        m = jnp.max(s, axis=0, keepdims=True)
        e = jnp.where(vis, jnp.exp(s - m), 0.0)
        ssum = jnp.sum(e, axis=0, keepdims=True)
        p = e / jnp.where(ssum > 0, ssum, 1.0)
        psum = psum + p
        pts.append(p.astype(BF16))
    pt = jnp.concatenate(pts, axis=1)
    ot = jnp.dot(vt_ref[...], pt, preferred_element_type=F32)
    for g in range(G):
        o_ref[:, g * LANE:(g + 1) * LANE] = ot[:, g * tq:(g + 1) * tq].T.astype(o_ref.dtype)
    p_hi = psum.astype(BF16)
    p_lo = (psum - p_hi.astype(F32)).astype(BF16)
    imp = (jnp.dot(ov_ref[...], p_hi, preferred_element_type=F32)
           + jnp.dot(ov_ref[...], p_lo, preferred_element_type=F32))
    nb = imp.shape[0]
    b_io = lax.broadcasted_iota(jnp.int32, (nb, tq), 0)
    qblk = (q0 + lax.broadcasted_iota(jnp.int32, (nb, tq), 1)) // NSA_SLC_BLOCK
    forced = (b_io == 0) | (b_io == qblk) | (b_io == qblk - 1)
    val = jnp.where(forced, 3e38, jnp.where(b_io <= qblk, imp, -1.0))
    seln = jnp.full((nb, tq), SEL_NEG, F32)
    for _ in range(k_sel):
        mx = jnp.max(val, axis=0, keepdims=True)
        idx = jnp.min(jnp.where(val == mx, b_io, nb), axis=0, keepdims=True)
        pick = (b_io == idx) & (mx >= 0.0)
        seln = jnp.where(pick, 0.0, seln)
        val = jnp.where(pick, -2.0, val)
    sel = seln.T
    for hf in range(n_halves):
        sel_ref[hf] = sel[:, hf * LANE:(hf + 1) * LANE].astype(sel_ref.dtype)


def _cmp_y0(s):
    return (LANE // NSA_CMP_STRIDE) * (s // LANE - 1)


def cmp_bias_strips(table, s):
    nc = s // NSA_CMP_STRIDE
    y0 = _cmp_y0(s)
    far_rows = -(-(REL_MAX_DIST + NSA_CMP_BLOCK) // NSA_CMP_STRIDE)
    ya = max(y0 - far_rows, 0)
    yb = min(y0 + LANE // NSA_CMP_STRIDE, y0 + nc)
    y = np.arange(ya, yb)[:, None]
    i = np.arange(LANE)[None, :]
    band = jnp.transpose(table.astype(F32)[_rel_bucket_np(NSA_CMP_STRIDE * (y0 - y) + i - (NSA_CMP_BLOCK - 1))], (2, 0, 1))
    h = table.shape[1]
    far = jnp.broadcast_to(table.astype(F32)[REL_BUCKETS - 1][:, None, None], (h, ya, LANE))
    return jnp.concatenate([far, band, jnp.zeros((h, y0 + nc - yb, LANE), F32)], axis=1)


def nsa_cmp_topk(nqn, k_cmp, v_cmp_t, strips, s):
    tq = LANE
    nc = s // NSA_CMP_STRIDE
    n_cmp = nc - NSA_CMP_BLOCK // NSA_CMP_STRIDE + 1
    n_slc = s // NSA_SLC_BLOCK
    nbp = -(-n_slc // LANE) * LANE
    n_halves = nbp // LANE
    cs = np.arange(nc) * NSA_CMP_STRIDE
    ce = cs + NSA_CMP_BLOCK - 1
    ss = np.arange(nbp) * NSA_SLC_BLOCK
    ov = ((cs[None, :] < ss[:, None] + NSA_SLC_BLOCK) & (ce[None, :] >= ss[:, None])).astype(np.float32)
    ov[:, n_cmp:] = 0.0
    ov[n_slc:, :] = 0.0
    y0 = _cmp_y0(s)
    k_sel = min(NSA_TOPK, n_slc)
    return pl.pallas_call(
        functools.partial(_cmp_topk_kernel, tq=tq, y0=y0, k_sel=k_sel, n_halves=n_halves),
        grid=(NSA_KV_HEADS, s // tq),
        in_specs=[pl.BlockSpec((tq, NSA_GROUP * LANE), lambda h, i: (i, h)),
                  pl.BlockSpec((None, nc, LANE), lambda h, i: (h, 0, 0)),
                  pl.BlockSpec((None, LANE, nc), lambda h, i: (h, 0, 0)),
                  pl.BlockSpec((nbp, nc), lambda h, i: (0, 0)),
                  pl.BlockSpec((NSA_GROUP, y0 + nc, tq), lambda h, i: (h, 0, 0))],
        out_specs=[pl.BlockSpec((tq, NSA_GROUP * LANE), lambda h, i: (i, h)),
                   pl.BlockSpec((n_halves, tq, LANE), lambda h, i: (0, i, h))],
        out_shape=[jax.ShapeDtypeStruct((s, NSA_HEADS * LANE), BF16),
                   jax.ShapeDtypeStruct((n_halves, s, NSA_KV_HEADS * LANE), BF16)],
        compiler_params=_cp(("parallel", "parallel")),
        name="nsa_cmp_topk",
    )(nqn, k_cmp, v_cmp_t, jnp.asarray(ov, BF16), strips)


def _nsa_combine_kernel(oc_ref, os_ref, ow_ref, g_ref, o_ref):
    gate = jax.nn.sigmoid(g_ref[...].astype(F32))
    for h in range(NSA_HEADS):
        sl = slice(h * LANE, (h + 1) * LANE)
        o = (gate[:, 3 * h:3 * h + 1] * oc_ref[:, sl].astype(F32)
             + gate[:, 3 * h + 1:3 * h + 2] * os_ref[:, sl].astype(F32)
             + gate[:, 3 * h + 2:3 * h + 3] * ow_ref[:, sl].astype(F32))
        o_ref[:, sl] = o.astype(o_ref.dtype)


def nsa_combine(o_c, o_s, o_w, proj, tm=512):
    s = o_c.shape[0]
    tm = min(tm, s)
    w = NSA_HEADS * LANE
    spec = pl.BlockSpec((tm, w), lambda i: (i, 0))
    return pl.pallas_call(
        _nsa_combine_kernel,
        grid=(s // tm,),
        in_specs=[spec, spec, spec, pl.BlockSpec((tm, LANE), lambda i: (i, C_NG // LANE))],
        out_specs=spec,
        out_shape=jax.ShapeDtypeStruct((s, w), BF16),
        compiler_params=_cp(("parallel",)),
        name="nsa_combine",
    )(o_c, o_s, o_w, proj)


def _xattn_kernel(q_ref, kv_ref, gq_ref, gk_ref, o_ref, *, scale):
    nk = XATTN_HEADS * LANE
    for h in range(XATTN_HEADS):
        sl = slice(h * LANE, (h + 1) * LANE)
        qn = (_group_norm(q_ref[:, sl].astype(F32), LANE) * (gq_ref[...] * scale)).astype(BF16)
        kn = (_group_norm(kv_ref[:, sl].astype(F32), LANE) * gk_ref[...]).astype(BF16)
        s = lax.dot_general(qn, kn, (((1,), (1,)), ((), ())), preferred_element_type=F32)
        m = jnp.max(s, axis=-1, keepdims=True)
        e = jnp.exp(s - m)
        p = e / jnp.sum(e, axis=-1, keepdims=True)
        v = kv_ref[:, nk + h * LANE:nk + (h + 1) * LANE]
        o_ref[:, sl] = jnp.dot(p.astype(v.dtype), v, preferred_element_type=F32).astype(o_ref.dtype)


def xattn_core(qx, kv, qk_norm, tq=512):
    s, w = qx.shape
    tq = min(tq, s)
    m = kv.shape[0]
    return pl.pallas_call(
        functools.partial(_xattn_kernel, scale=HEAD_DIM ** -0.5),
        grid=(s // tq,),
        in_specs=[pl.BlockSpec((tq, w), lambda i: (i, 0)),
                  pl.BlockSpec((m, 2 * w), lambda i: (0, 0)),
                  pl.BlockSpec((1, LANE), lambda i: (0, 0)),
                  pl.BlockSpec((1, LANE), lambda i: (0, 0))],
        out_specs=pl.BlockSpec((tq, w), lambda i: (i, 0)),
        out_shape=jax.ShapeDtypeStruct((s, w), BF16),
        compiler_params=_cp(("parallel",)),
        name="xattn_core",
    )(qx, kv, qk_norm[0].reshape(1, LANE).astype(F32), qk_norm[1].reshape(1, LANE).astype(F32))


def _pack_w_in(w):
    d = w.shape[0]
    z = lambda n: jnp.zeros((d, n), w.dtype)
    parts = [w[:, 0:896], w[:, 1408:1472], z(64), w[:, 896:1408], w[:, 1472:7488],
             w[:, 7488:7512], z(LANE - 3 * NSA_HEADS), w[:, 7512:10200], z(NP_COLS - 10368)]
    return jnp.concatenate(parts, axis=1).astype(BF16)


def _mla_mixer(proj, pos, l, p):
    s = proj.shape[0]
    qm = mla_q_proj(proj, pos, p["mla_q_norm"][l], p["mla_w_uq"][l], p["mla_nope_norm"][l, 0], p["mla_rope_norm"][l, 0])
    km, vm = mla_kv_proj(proj, pos, p["mla_kv_norm"][l], p["mla_w_ukv"][l], p["mla_nope_norm"][l, 1],
                         p["mla_rope_norm"][l, 1])
    ident = lambda n: n
    return attention(qm, km, vm, n_streams=MLA_HEADS, seq=s, out_cols=MLA_HEADS * MLA_V, qcol=ident, kcol=ident,
                     vcol=ident, ocol=ident, tq=ATT_ROWS, tk=ATT_TK, tkb=ATT_TKB, dq=2 * LANE, dk=2 * LANE,
                     name="mla_attention")


def bias_tables(rel_bias, s):
    tabs = {"dil": []}
    for g, (_, dil) in enumerate(DIL_PATTERNS):
        tq = min(DIL_TQ, s // dil)
        tabs["dil"].append(bias_strips(rel_bias[:, BIAS_DIL + g * DIL_HP:BIAS_DIL + (g + 1) * DIL_HP], tq, tq, tq, dil))
    far_zero = lambda t: t - t[REL_BUCKETS - 1:REL_BUCKETS]
    nsa = rel_bias[:, BIAS_NSA:BIAS_NSA + NSA_HEADS].astype(F32)
    tk, tkb = min(ATT_TK, s), min(ATT_TKB, s)
    tabs["nsa"] = bias_strips(far_zero(nsa), ATT_ROWS // NSA_GROUP, tk, tkb, 1)
    tabs["nsa_cmp"] = cmp_bias_strips(nsa, s)
    diff = rel_bias[:, BIAS_DIFF:BIAS_DIFF + DIFF_HEADS].astype(F32)
    tabs["diff"] = bias_strips(far_zero(diff), min(ATT_ROWS // 2, s), tk, tkb, 1)
    return tabs


def _dilated_mixer(proj, l, p, tabs):
    s = proj.shape[0]
    w = DIL_HP * LANE
    dqn = head_rmsnorm(proj, C_DQ, DIL_HEADS, p["dil_qk_norm"][l, 0], LANE, HEAD_DIM ** -0.5)
    dkn = head_rmsnorm(proj, C_DK, DIL_HEADS, p["dil_qk_norm"][l, 1], LANE, 1.0)
    outs, lses = [], []
    ident = lambda n: n
    for g, (window, dil) in enumerate(DIL_PATTERNS):
        n_sub = s // dil
        tq = min(DIL_TQ, n_sub)
        if dil == 1:
            qkv = (dqn, dkn, proj)
            qcol = lambda r, g=g: g
            vcol = lambda r, g=g: C_DV // w + g
        else:
            fold = lambda t, c0: t[:, c0 + g * w:c0 + (g + 1) * w].reshape(n_sub, dil * w)
            qkv = (fold(dqn, 0), fold(dkn, 0), fold(proj, C_DV))
            qcol = vcol = ident
        o, lse = banded_attention(*qkv, tabs["dil"][g], streams=dil, seq=n_sub, tq=tq, span=window // dil,
                                  qcol=qcol, vcol=vcol, name=f"dil_attention_{g}")
        outs.append(o.reshape(s, w))
        lses.append(lse.reshape(s, w))
    return dil_combine(outs, lses)


def _nsa_mixer(proj, l, p, tabs):
    s = proj.shape[0]
    gains = p["nsa_qk_norm"][l]
    nqn = head_rmsnorm(proj, C_NQ, NSA_HEADS, gains[0], LANE, HEAD_DIM ** -0.5)
    kwn = head_rmsnorm(proj, C_NKW, NSA_KV_HEADS, gains[3], LANE, 1.0)
    ksel = nsa_sel_keys(proj, gains[2])
    nc = s // NSA_CMP_STRIDE

    def chunks(c0):
        t = proj[:, c0:c0 + NSA_KV_HEADS * LANE].reshape(nc, NSA_CMP_STRIDE, NSA_KV_HEADS, LANE)
        return jnp.transpose(t, (2, 0, 1, 3)).reshape(NSA_KV_HEADS, nc, NSA_CMP_STRIDE * LANE)

    k_cmp = nsa_compress(chunks(C_NKC), p["nsa_cmp_pe"][l, 0], p["nsa_cmp_w1"][l, 0], p["nsa_cmp_w2"][l, 0],
                         gains[1], norm=True, transpose=False)
    v_cmp_t = nsa_compress(chunks(C_NVC), p["nsa_cmp_pe"][l, 1], p["nsa_cmp_w1"][l, 1], p["nsa_cmp_w2"][l, 1],
                           gains[1], norm=False, transpose=True)
    o_c, sel = nsa_cmp_topk(nqn, k_cmp, v_cmp_t, tabs["nsa_cmp"], s)
    ident = lambda n: n
    common = dict(n_streams=NSA_KV_HEADS, seq=s, out_cols=NSA_HEADS * LANE, qcol=ident, ocol=ident,
                  tq=ATT_ROWS // NSA_GROUP, tk=ATT_TK, tkb=ATT_TKB, G=NSA_GROUP, dq=LANE, bias=tabs["nsa"], bcol=ident,
                  bias_far_zero=True)
    o_s = attention(nqn, ksel, proj, kcol=ident, vcol=lambda n: C_NVS // LANE + n, mode="nsa_sel", dk=2 * LANE,
                    sel=sel, name="nsa_sel_attention", **common)
    o_w = attention(nqn, kwn, proj, kcol=ident, vcol=lambda n: C_NVW // LANE + n, dk=LANE, span=NSA_WINDOW - 1,
                    name="nsa_win_attention", **common)
    return nsa_combine(o_c, o_s, o_w, proj)


def _diff_mixer(proj, l, p, tabs):
    s = proj.shape[0]
    fqn = head_rmsnorm(proj, C_FQ, DIFF_HEADS, p["diff_qk_norm"][l, 0], DIFF_QK, DIFF_QK ** -0.5)
    fkn = head_rmsnorm(proj, C_FK, DIFF_HEADS, p["diff_qk_norm"][l, 1], DIFF_QK, 1.0)
    lam_init = 0.8 - 0.6 * math.exp(-0.3 * l)
    lv = p["diff_lambda"][l].astype(F32)
    lam = jnp.exp(jnp.sum(lv[0] * lv[1])) - jnp.exp(jnp.sum(lv[2] * lv[3])) + lam_init
    ident = lambda n: n
    return attention(fqn, fkn, proj, n_streams=DIFF_HEADS, seq=s, out_cols=DIFF_HEADS * DIFF_V, qcol=ident,
                     kcol=ident, vcol=lambda n: C_FV // LANE + n, ocol=ident, mode="diff", tq=ATT_ROWS // 2,
                     tk=ATT_TK, tkb=ATT_TKB, dq=LANE, dk=LANE, bias=tabs["diff"], bcol=ident, bias_far_zero=True,
                     lam=jnp.full((1, LANE), lam, F32),
                     gout=p["diff_out_norm"][l].reshape(1, LANE).astype(F32), out_scale=1.0 - lam_init,
                     name="diff_attention")


def kernel(x, mem, positions, rel_bias, norm_mix, w_in, mla_q_norm, mla_kv_norm, mla_w_uq, mla_w_ukv, mla_nope_norm, mla_rope_norm, dil_qk_norm, nsa_qk_norm, nsa_cmp_pe, nsa_cmp_w1, nsa_cmp_w2, diff_qk_norm, diff_lambda, diff_out_norm, w_out, norm_xattn, norm_mem, xattn_wq, xattn_wkv, xattn_qk_norm, xattn_wo, norm_ffn, ffn_w_gate, ffn_w_up, ffn_conv_w, ffn_conv_b, ffn_w_down):
    p = dict(mla_q_norm=mla_q_norm, mla_kv_norm=mla_kv_norm, mla_w_uq=mla_w_uq, mla_w_ukv=mla_w_ukv,
             mla_nope_norm=mla_nope_norm, mla_rope_norm=mla_rope_norm, dil_qk_norm=dil_qk_norm,
             nsa_qk_norm=nsa_qk_norm, nsa_cmp_pe=nsa_cmp_pe, nsa_cmp_w1=nsa_cmp_w1, nsa_cmp_w2=nsa_cmp_w2,
             diff_qk_norm=diff_qk_norm, diff_lambda=diff_lambda, diff_out_norm=diff_out_norm)
    b, s, d = x.shape
    assert b == 1
    xs = x.reshape(s, d).astype(F32)
    mems = mem.reshape(mem.shape[1], d).astype(F32)
    pos = positions.reshape(s, 1).astype(jnp.int32)
    tabs = bias_tables(rel_bias, s)
    for l in range(DEPTH):
        h = rmsnorm_rows(xs, norm_mix[l])
        proj = matmul(h, _pack_w_in(w_in[l]), tm=512, tn=1536, out_dtype=BF16, name="in_proj")
        mix = [_mla_mixer(proj, pos, l, p), _dilated_mixer(proj, l, p, tabs), _nsa_mixer(proj, l, p, tabs),
               _diff_mixer(proj, l, p, tabs)]
        xs = matmul_cat(mix, w_out[l].astype(BF16), xs, tm=512, tn=1024, name="out_proj")
        hx = rmsnorm_rows(xs, norm_xattn[l])
        qx = matmul(hx, xattn_wq[l].astype(BF16), tm=1024, tn=512, out_dtype=BF16, name="xattn_q")
        kvm = matmul(rmsnorm_rows(mems, norm_mem[l]), xattn_wkv[l].astype(BF16), tm=256, tn=1024, out_dtype=BF16,
                     name="xattn_kv")
        ox = xattn_core(qx, kvm, xattn_qk_norm[l])
        xs = matmul(ox, xattn_wo[l].astype(BF16), tm=1024, tn=1024, out_dtype=F32, residual=xs, name="xattn_o")
        hf = rmsnorm_rows(xs, norm_ffn[l])
        act = ffn_gate_up(hf, ffn_w_gate[l].astype(BF16), ffn_w_up[l].astype(BF16), ffn_conv_w[l], ffn_conv_b[l])
        xs = matmul(act, ffn_w_down[l].astype(BF16), tm=512, tn=512, out_dtype=F32, residual=xs, name="ffn_down")
    return xs.reshape(b, s, d)
```

```python
import functools
import math

import numpy as np
import jax
import jax.numpy as jnp
from jax import lax
from jax.experimental import pallas as pl
from jax.experimental.pallas import tpu as pltpu

F32 = jnp.float32
BF16 = jnp.bfloat16

D_MODEL = 4096
DEPTH = 2
HEAD_DIM = 128
EPS = 1e-6
MLA_HEADS = 8
MLA_Q_RANK = 896
MLA_KV_RANK = 512
MLA_NOPE = 128
MLA_ROPE = 64
MLA_V = 128
ROPE_THETA = 10000.0
DIL_PATTERNS = ((128, 1), (512, 4), (2048, 16))
DIL_HP = 3
DIL_HEADS = DIL_HP * len(DIL_PATTERNS)
NSA_HEADS = 8
NSA_KV_HEADS = 2
NSA_GROUP = NSA_HEADS // NSA_KV_HEADS
NSA_CMP_BLOCK = 32
NSA_CMP_STRIDE = 16
NSA_SLC_BLOCK = 64
NSA_TOPK = 16
NSA_WINDOW = 512
DIFF_HEADS = 7
DIFF_QK = 64
DIFF_V = 128
XATTN_HEADS = 4
FFN_DIM = 11008
REL_BUCKETS = 32
REL_MAX_DIST = 2048
BIAS_DIL = 0
BIAS_NSA = DIL_HEADS
BIAS_DIFF = DIL_HEADS + NSA_HEADS

LANE = 128
VMEM_LIMIT = 56 * 1024 * 1024
NEG = -1e30
SEL_NEG = -float(2 ** 30)
SEL_HALF_BLOCKS = 128
ATT_ROWS = 1024
ATT_TK = 1024
ATT_TKB = 2048
DIL_TQ = 512

C_CQ = 0
C_KR = 896
C_CKV = 1024
C_DQ = 1536
C_DK = 2688
C_DV = 3840
C_NQ = 4992
C_NKC = 6016
C_NVC = 6272
C_NKS = 6528
C_NVS = 6784
C_NKW = 7040
C_NVW = 7296
C_NG = 7552
C_FQ = 7680
C_FK = 8576
C_FV = 9472
NP_COLS = 10752


def _cp(sem):
    return pltpu.CompilerParams(dimension_semantics=sem, vmem_limit_bytes=VMEM_LIMIT)


def _rmsnorm_rows_kernel(x_ref, g_ref, o_ref):
    x = x_ref[...].astype(F32)
    ms = jnp.mean(x * x, axis=-1, keepdims=True)
    o_ref[...] = (x * lax.rsqrt(ms + EPS) * g_ref[...]).astype(o_ref.dtype)


def rmsnorm_rows(x, g, tm=256):
    m, d = x.shape
    tm = min(tm, m)
    return pl.pallas_call(
        _rmsnorm_rows_kernel,
        grid=(m // tm,),
        in_specs=[pl.BlockSpec((tm, d), lambda i: (i, 0)),
                  pl.BlockSpec((1, d), lambda i: (0, 0))],
        out_specs=pl.BlockSpec((tm, d), lambda i: (i, 0)),
        out_shape=jax.ShapeDtypeStruct((m, d), BF16),
        compiler_params=_cp(("parallel",)),
        name="rmsnorm_rows",
    )(x, g.reshape(1, d).astype(F32))


def _matmul_kernel(*refs, has_res):
    if has_res:
        a_ref, w_ref, r_ref, o_ref = refs
    else:
        a_ref, w_ref, o_ref = refs
    acc = jnp.dot(a_ref[...], w_ref[...], preferred_element_type=F32)
    if has_res:
        acc = acc + r_ref[...]
    o_ref[...] = acc.astype(o_ref.dtype)


def _matmul_cat_kernel(*refs, offs):
    a_refs = refs[:len(offs)]
    w_ref, r_ref, o_ref = refs[len(offs):]
    acc = r_ref[...]
    for a_ref, (k0, k1) in zip(a_refs, offs):
        acc = acc + jnp.dot(a_ref[...], w_ref[k0:k1, :], preferred_element_type=F32)
    o_ref[...] = acc.astype(o_ref.dtype)


def matmul_cat(a_list, w, residual, *, tm, tn, name="matmul_cat"):
    m = a_list[0].shape[0]
    k, n = w.shape
    offs, k0 = [], 0
    for a in a_list:
        offs.append((k0, k0 + a.shape[1]))
        k0 += a.shape[1]
    tm = min(tm, m)
    assert k0 == k and m % tm == 0 and n % tn == 0
    in_specs = [pl.BlockSpec((tm, a.shape[1]), lambda j, i: (i, 0)) for a in a_list]
    in_specs += [pl.BlockSpec((k, tn), lambda j, i: (0, j)), pl.BlockSpec((tm, tn), lambda j, i: (i, j))]
    return pl.pallas_call(
        functools.partial(_matmul_cat_kernel, offs=tuple(offs)),
        grid=(n // tn, m // tm),
        in_specs=in_specs,
        out_specs=pl.BlockSpec((tm, tn), lambda j, i: (i, j)),
        out_shape=jax.ShapeDtypeStruct((m, n), F32),
        compiler_params=_cp(("parallel", "parallel")),
        name=name,
    )(*a_list, w, residual)


def matmul(a, w, *, tm, tn, out_dtype, residual=None, name="matmul"):
    m, k = a.shape
    n = w.shape[1]
    tm = min(tm, m)
    tn = min(tn, n)
    assert m % tm == 0 and n % tn == 0
    in_specs = [pl.BlockSpec((tm, k), lambda j, i: (i, 0)),
                pl.BlockSpec((k, tn), lambda j, i: (0, j))]
    args = [a, w]
    if residual is not None:
        in_specs.append(pl.BlockSpec((tm, tn), lambda j, i: (i, j)))
        args.append(residual)
    return pl.pallas_call(
        functools.partial(_matmul_kernel, has_res=residual is not None),
        grid=(n // tn, m // tm),
        in_specs=in_specs,
        out_specs=pl.BlockSpec((tm, tn), lambda j, i: (i, j)),
        out_shape=jax.ShapeDtypeStruct((m, n), out_dtype),
        compiler_params=_cp(("parallel", "parallel")),
        name=name,
    )(*args)


def _ffn_gate_up_kernel(h_ref, wg_ref, wu_ref, cw_ref, cb_ref, o_ref, carry_ref):
    i = pl.program_id(1)

    @pl.when(i == 0)
    def _():
        carry_ref[...] = jnp.zeros_like(carry_ref)

    h = h_ref[...]
    u = jnp.dot(h, wg_ref[...], preferred_element_type=F32)
    up = jnp.dot(h, wu_ref[...], preferred_element_type=F32)
    tm = u.shape[0]
    prev = carry_ref[...]
    row = lax.broadcasted_iota(jnp.int32, u.shape, 0)
    u1 = jnp.where(row == 0, prev[7:8, :], pltpu.roll(u, 1, 0))
    u2 = pltpu.roll(u, 2, 0)
    u2 = jnp.where(row == 0, prev[6:7, :], jnp.where(row == 1, prev[7:8, :], u2))
    cw = cw_ref[...]
    g = cw[0:1, :] * u2 + cw[1:2, :] * u1 + cw[2:3, :] * u + cb_ref[...]
    o_ref[...] = (g * jax.nn.sigmoid(g) * up).astype(o_ref.dtype)
    carry_ref[...] = u[tm - 8:, :]


def ffn_gate_up(h, wg, wu, conv_w, conv_b, tm=1024, tf=256):
    m, d = h.shape
    f = wg.shape[1]
    tm = min(tm, m)
    assert f % tf == 0 and m % tm == 0
    return pl.pallas_call(
        _ffn_gate_up_kernel,
        grid=(f // tf, m // tm),
        in_specs=[pl.BlockSpec((tm, d), lambda j, i: (i, 0)),
                  pl.BlockSpec((d, tf), lambda j, i: (0, j)),
                  pl.BlockSpec((d, tf), lambda j, i: (0, j)),
                  pl.BlockSpec((3, tf), lambda j, i: (0, j)),
                  pl.BlockSpec((1, tf), lambda j, i: (0, j))],
        out_specs=pl.BlockSpec((tm, tf), lambda j, i: (i, j)),
        out_shape=jax.ShapeDtypeStruct((m, f), BF16),
        scratch_shapes=[pltpu.VMEM((8, tf), F32)],
        compiler_params=_cp(("arbitrary", "arbitrary")),
        name="ffn_gate_up",
    )(h, wg, wu, conv_w.astype(F32), conv_b.reshape(1, f).astype(F32))


def _group_norm(x, d):
    sq = x * x
    tot = jnp.sum(sq, axis=-1, keepdims=True)
    if d == LANE:
        ms = tot / d
    else:
        lane = lax.broadcasted_iota(jnp.int32, x.shape, 1)
        lo = jnp.sum(jnp.where(lane < d, sq, 0.0), axis=-1, keepdims=True)
        ms = jnp.where(lane < d, lo, tot - lo) / d
    return x * lax.rsqrt(ms + EPS)


def _head_norm_kernel(x_ref, g_ref, o_ref, *, d, scale):
    x = x_ref[...].astype(F32)
    o_ref[...] = (_group_norm(x, d) * (g_ref[...] * scale)).astype(o_ref.dtype)


def head_rmsnorm(src, col0, ncb, gain, d, scale, tm=4096):
    s = src.shape[0]
    tm = min(tm, s)
    cb0 = col0 // LANE
    g = jnp.tile(gain.astype(F32), LANE // d).reshape(1, LANE)
    return pl.pallas_call(
        functools.partial(_head_norm_kernel, d=d, scale=scale),
        grid=(s // tm, ncb),
        in_specs=[pl.BlockSpec((tm, LANE), lambda i, c: (i, cb0 + c)),
                  pl.BlockSpec((1, LANE), lambda i, c: (0, 0))],
        out_specs=pl.BlockSpec((tm, LANE), lambda i, c: (i, c)),
        out_shape=jax.ShapeDtypeStruct((s, ncb * LANE), BF16),
        compiler_params=_cp(("parallel", "parallel")),
        name="head_rmsnorm",
    )(src, g)


def _nsa_ksel_kernel(x_ref, g_ref, o_ref):
    tm = x_ref.shape[0]
    x = x_ref[...].astype(F32)
    o_ref[:, :LANE] = (_group_norm(x, LANE) * g_ref[...]).astype(o_ref.dtype)
    row = pl.program_id(0) * tm + lax.broadcasted_iota(jnp.int32, (tm, LANE), 0)
    col = lax.broadcasted_iota(jnp.int32, (tm, LANE), 1)
    hot = ((row // NSA_SLC_BLOCK) % SEL_HALF_BLOCKS) == col
    o_ref[:, LANE:] = jnp.where(hot, 1.0, 0.0).astype(o_ref.dtype)


def nsa_sel_keys(proj, gain, tm=4096):
    s = proj.shape[0]
    tm = min(tm, s)
    cb0 = C_NKS // LANE
    return pl.pallas_call(
        _nsa_ksel_kernel,
        grid=(s // tm, NSA_KV_HEADS),
        in_specs=[pl.BlockSpec((tm, LANE), lambda i, c: (i, cb0 + c)),
                  pl.BlockSpec((1, LANE), lambda i, c: (0, 0))],
        out_specs=pl.BlockSpec((tm, 2 * LANE), lambda i, c: (i, c)),
        out_shape=jax.ShapeDtypeStruct((s, NSA_KV_HEADS * 2 * LANE), BF16),
        compiler_params=_cp(("parallel", "parallel")),
        name="nsa_sel_keys",
    )(proj, gain.reshape(1, LANE).astype(F32))


def _rope_tables_kernel(pos_ref, inv_ref, sign_ref, cos_ref, sin_ref):
    ang = pos_ref[...].astype(F32) * inv_ref[...]
    cos_ref[...] = jnp.cos(ang)
    sin_ref[...] = jnp.sin(ang) * sign_ref[...]


def rope_tables(pos, tm=2048):
    s = pos.shape[0]
    tm = min(tm, s)
    inv, sign = _rope_consts()
    const = lambda i: (0, 0)
    spec = pl.BlockSpec((tm, LANE), lambda i: (i, 0))
    return pl.pallas_call(
        _rope_tables_kernel,
        grid=(s // tm,),
        in_specs=[pl.BlockSpec((tm, 1), lambda i: (i, 0)), pl.BlockSpec((1, LANE), const), pl.BlockSpec((1, LANE), const)],
        out_specs=[spec, spec],
        out_shape=[jax.ShapeDtypeStruct((s, LANE), F32)] * 2,
        compiler_params=_cp(("parallel",)),
        name="rope_tables",
    )(pos, inv, sign)


def _rope_apply(x, cosv, sin_signed):
    lane = lax.broadcasted_iota(jnp.int32, x.shape, 1)
    half = MLA_ROPE // 2
    swapped = jnp.where(lane < half, pltpu.roll(x, LANE - half, 1), pltpu.roll(x, half, 1))
    return x * cosv + swapped * sin_signed


def _latent_norm(c_ref, g_ref):
    c = c_ref[...].astype(F32)
    ms = jnp.mean(c * c, axis=-1, keepdims=True)
    return (c * lax.rsqrt(ms + EPS) * g_ref[...]).astype(BF16)


def _mla_q_kernel(c_ref, cos_ref, sin_ref, gq_ref, w_ref, gn_ref, gr_ref, o_ref, *, scale):
    u = jnp.dot(_latent_norm(c_ref, gq_ref), w_ref[...], preferred_element_type=F32)
    cosv, sinv = cos_ref[...], sin_ref[...]
    for h in range(MLA_HEADS):
        b = 2 * LANE * h
        nope = _group_norm(u[:, b:b + LANE], LANE) * (gn_ref[...] * scale)
        rp = u[:, b + LANE:b + 2 * LANE]
        rp = rp * lax.rsqrt(jnp.sum(rp * rp, axis=-1, keepdims=True) / MLA_ROPE + EPS) * gr_ref[...]
        rp = _rope_apply(rp, cosv, sinv) * scale
        o_ref[:, b:b + LANE] = nope.astype(o_ref.dtype)
        o_ref[:, b + LANE:b + 2 * LANE] = rp.astype(o_ref.dtype)


def _mla_kv_kernel(c_ref, kr_ref, cos_ref, sin_ref, gkv_ref, w_ref, gn_ref, gr_ref, k_ref, v_ref):
    kv = jnp.dot(_latent_norm(c_ref, gkv_ref), w_ref[...], preferred_element_type=F32)
    cosv, sinv = cos_ref[...], sin_ref[...]
    kr = kr_ref[...].astype(F32)
    kr = kr * lax.rsqrt(jnp.sum(kr * kr, axis=-1, keepdims=True) / MLA_ROPE + EPS) * gr_ref[...]
    kr = _rope_apply(kr, cosv, sinv).astype(k_ref.dtype)
    nv = MLA_HEADS * LANE
    for h in range(MLA_HEADS):
        kn = _group_norm(kv[:, h * LANE:(h + 1) * LANE], LANE) * gn_ref[...]
        k_ref[:, 2 * LANE * h:2 * LANE * h + LANE] = kn.astype(k_ref.dtype)
        k_ref[:, 2 * LANE * h + LANE:2 * LANE * (h + 1)] = kr
    v_ref[...] = kv[:, nv:].astype(v_ref.dtype)


def _rope_consts():
    half = MLA_ROPE // 2
    inv = ROPE_THETA ** (-jnp.arange(half, dtype=F32) / half)
    inv = jnp.concatenate([inv, inv, jnp.zeros((LANE - MLA_ROPE,), F32)]).reshape(1, LANE)
    sign = np.zeros((1, LANE), np.float32)
    sign[0, :half] = -1.0
    sign[0, half:MLA_ROPE] = 1.0
    return inv, jnp.asarray(sign)


def _pad_lanes(v, n=LANE):
    return jnp.concatenate([v.astype(F32), jnp.zeros((n - v.shape[0],), F32)]).reshape(1, n)


def mla_q_proj(proj, rope, q_norm, w_uq, nope_gain, rope_gain, tm=512):
    s = proj.shape[0]
    tm = min(tm, s)
    w = jnp.pad(w_uq.reshape(MLA_Q_RANK, MLA_HEADS, MLA_NOPE + MLA_ROPE),
                ((0, 0), (0, 0), (0, LANE - MLA_ROPE))).reshape(MLA_Q_RANK, MLA_HEADS * 2 * LANE).astype(BF16)
    wq = MLA_HEADS * 2 * LANE
    const = lambda i: (0, 0)
    return pl.pallas_call(
        functools.partial(_mla_q_kernel, scale=(MLA_NOPE + MLA_ROPE) ** -0.5),
        grid=(s // tm,),
        in_specs=[pl.BlockSpec((tm, MLA_Q_RANK), lambda i: (i, C_CQ // MLA_Q_RANK)),
                  pl.BlockSpec((tm, LANE), lambda i: (i, 0)), pl.BlockSpec((tm, LANE), lambda i: (i, 0)),
                  pl.BlockSpec((1, MLA_Q_RANK), const),
                  pl.BlockSpec((MLA_Q_RANK, wq), const),
                  pl.BlockSpec((1, LANE), const), pl.BlockSpec((1, LANE), const)],
        out_specs=pl.BlockSpec((tm, wq), lambda i: (i, 0)),
        out_shape=jax.ShapeDtypeStruct((s, wq), BF16),
        compiler_params=_cp(("parallel",)),
        name="mla_q_proj",
    )(proj, rope[0], rope[1], q_norm.reshape(1, -1).astype(F32), w, nope_gain.reshape(1, LANE).astype(F32),
      _pad_lanes(rope_gain))


def mla_kv_proj(proj, rope, kv_norm, w_ukv, nope_gain, rope_gain, tm=512):
    s = proj.shape[0]
    tm = min(tm, s)
    w = jnp.transpose(w_ukv.reshape(MLA_KV_RANK, MLA_HEADS, 2, LANE), (0, 2, 1, 3))
    w = w.reshape(MLA_KV_RANK, 2 * MLA_HEADS * LANE).astype(BF16)
    wk = MLA_HEADS * 2 * LANE
    wv = MLA_HEADS * MLA_V
    const = lambda i: (0, 0)
    return pl.pallas_call(
        _mla_kv_kernel,
        grid=(s // tm,),
        in_specs=[pl.BlockSpec((tm, MLA_KV_RANK), lambda i: (i, C_CKV // MLA_KV_RANK)),
                  pl.BlockSpec((tm, LANE), lambda i: (i, C_KR // LANE)),
                  pl.BlockSpec((tm, LANE), lambda i: (i, 0)), pl.BlockSpec((tm, LANE), lambda i: (i, 0)),
                  pl.BlockSpec((1, MLA_KV_RANK), const),
                  pl.BlockSpec((MLA_KV_RANK, 2 * MLA_HEADS * LANE), const),
                  pl.BlockSpec((1, LANE), const), pl.BlockSpec((1, LANE), const)],
        out_specs=[pl.BlockSpec((tm, wk), lambda i: (i, 0)),
                   pl.BlockSpec((tm, wv), lambda i: (i, 0))],
        out_shape=[jax.ShapeDtypeStruct((s, wk), BF16), jax.ShapeDtypeStruct((s, wv), BF16)],
        compiler_params=_cp(("parallel",)),
        name="mla_kv_proj",
    )(proj, proj, rope[0], rope[1], kv_norm.reshape(1, -1).astype(F32), w, nope_gain.reshape(1, LANE).astype(F32),
      _pad_lanes(rope_gain))


def _rel_bucket_np(dist):
    n = np.maximum(dist, 0)
    exact = REL_BUCKETS // 2
    ratio = np.log(np.maximum(n, 1).astype(np.float32) / np.float32(exact)) / np.float32(math.log(REL_MAX_DIST / exact))
    large = np.minimum(exact + (ratio.astype(np.float32) * (REL_BUCKETS - exact)).astype(np.int32), REL_BUCKETS - 1)
    return np.where(n < exact, n, large).astype(np.int32)


def _strip_x0(tk):
    return -(-(REL_MAX_DIST - 1 + tk) // LANE) * LANE


def bias_strips(table, tq, tk, tkb, dist_scale):
    x0 = _strip_x0(tkb)
    width = x0 + max(tq, tk)
    period = -(-(width + tq) // LANE) * LANE
    t = np.arange(period)
    t = np.where(t < width, t, t - period)
    u = table.astype(F32)[_rel_bucket_np((x0 - t) * dist_scale)].T
    h = u.shape[0]
    return pl.pallas_call(
        functools.partial(_toeplitz_kernel, tq=tq, width=width),
        grid=(h,),
        in_specs=[pl.BlockSpec((None, 1, period), lambda n: (n, 0, 0))],
        out_specs=pl.BlockSpec((None, tq, width), lambda n: (n, 0, 0)),
        out_shape=jax.ShapeDtypeStruct((h, tq, width), F32),
        compiler_params=_cp(("parallel",)),
        name="bias_strips",
    )(u.reshape(h, 1, period))


def _toeplitz_kernel(u_ref, o_ref, *, tq, width):
    rows = jnp.broadcast_to(u_ref[...], (tq, u_ref.shape[1]))
    o_ref[...] = pltpu.roll(rows, 0, 1, stride=1, stride_axis=0)[:, :width]


def _attn_kernel(*refs, mode, tq, tk, tkb, G, dq, dv, span, has_bias, far_zero, x0, n_halves, want_lse, out_scale):
    refs = list(refs)
    q_ref = refs.pop(0)
    sel_ref = refs.pop(0) if mode == "nsa_sel" else None
    k_ref = refs.pop(0)
    v_ref = refs.pop(0)
    bias_ref = refs.pop(0) if has_bias else None
    lam_ref = gout_ref = None
    if mode == "diff":
        lam_ref = refs.pop(0)
        gout_ref = refs.pop(0)
    o_ref = refs.pop(0)
    lse_ref = refs.pop(0) if want_lse else None
    m_sc, l_sc, acc_sc = refs

    qi = pl.program_id(1)
    q0 = qi * tq
    n_tiles = k_ref.shape[0] // tk

    if mode == "diff":
        q = q_ref[...]
        lane = lax.broadcasted_iota(jnp.int32, q.shape, 1)
        zero = jnp.zeros_like(q)
        lhs = [jnp.concatenate([jnp.where(lane < DIFF_QK, q, zero), jnp.where(lane >= DIFF_QK, q, zero)], axis=0)]
    elif mode == "nsa_sel":
        lhs = []
        for hf in range(n_halves):
            sel = sel_ref[hf]
            lhs.append(jnp.concatenate(
                [jnp.concatenate([q_ref[:, g * dq:(g + 1) * dq], sel], axis=1) for g in range(G)], axis=0))
    else:
        lhs = [jnp.concatenate([q_ref[:, g * dq:(g + 1) * dq] for g in range(G)], axis=0) if G > 1 else q_ref[...]]

    m_sc[...] = jnp.full_like(m_sc, NEG)
    l_sc[...] = jnp.zeros_like(l_sc)
    acc_sc[...] = jnp.zeros_like(acc_sc)

    def logits(k0, w, Q, masked, biased):
        kt = k_ref[pl.ds(k0, w), :]
        s = lax.dot_general(Q, kt, (((1,), (1,)), ((), ())), preferred_element_type=F32)
        has_bias = biased
        if has_bias or masked:
            if masked:
                rel = (q0 - k0) + lax.broadcasted_iota(jnp.int32, (tq, w), 0) - lax.broadcasted_iota(jnp.int32, (tq, w), 1)
                vis = rel >= 0
                if span is not None:
                    vis = jnp.logical_and(vis, rel <= span)
            if has_bias:
                start = pl.multiple_of(jnp.maximum(x0 - (q0 - k0), 0), LANE)
            parts = []
            for g in range(G):
                sg = s[g * tq:(g + 1) * tq]
                if has_bias:
                    sg = sg + bias_ref[0 if mode == "diff" else g, :, pl.ds(start, w)]
                if masked:
                    sg = jnp.where(vis, sg, NEG)
                parts.append(sg)
            s = jnp.concatenate(parts, axis=0) if G > 1 else parts[0]
        return s

    def update(s, k0, w):
        vt = v_ref[pl.ds(k0, w), :]
        m_prev = m_sc[...]
        m_new = jnp.maximum(m_prev, jnp.max(s, axis=-1, keepdims=True))
        alpha = jnp.exp(m_prev - m_new)
        p = jnp.exp(s - jnp.tile(m_new, (1, w // LANE)))
        l_sc[...] = alpha * l_sc[...] + jnp.sum(p, axis=-1, keepdims=True)
        acc_sc[...] = alpha * acc_sc[...] + jnp.dot(p.astype(vt.dtype), vt, preferred_element_type=F32)
        m_sc[...] = m_new

    def tile(k0, w, Q, masked, biased):
        if w == tkb and tkb > tk:
            h = w // 2
            sa = logits(k0, h, Q, masked, biased)
            sb = logits(k0 + h, h, Q, masked, biased)
            update(sa, k0, h)
            update(sb, k0 + h, h)
        else:
            update(logits(k0, w, Q, masked, biased), k0, w)

    def run(lo, hi, w, Q, masked, biased=has_bias):
        def body(j, c):
            tile(pl.multiple_of(j * w, w), w, Q, masked, biased)
            return c
        lax.fori_loop(lo, hi, body, 0)

    def run_visible(lo, hi, Q):
        if tkb == tk:
            run(lo, hi, tk, Q, False)
            return
        r = tkb // tk
        up = jnp.minimum(-(-lo // r) * r, hi)
        dn = jnp.maximum((hi // r) * r, up)
        if not (isinstance(lo, int) and lo % r == 0):
            run(lo, up, tk, Q, False)
        if has_bias and far_zero:
            far = jnp.clip(jnp.maximum(q0 - (REL_MAX_DIST - 1), 0) // tkb, up // r, dn // r)
            run(up // r, far, tkb, Q, False, False)
            run(far, dn // r, tkb, Q, False)
        else:
            run(up // r, dn // r, tkb, Q, False)
        run(dn, hi, tk, Q, False)

    j_hi = jnp.minimum((q0 + tq - 1) // tk, n_tiles - 1)
    b = (q0 + 1) // tk
    if span is None:
        j_lo = 0
        a = 0
    else:
        j_lo = jnp.maximum(q0 - span, 0) // tk
        a = jnp.maximum(q0 + tq - 1 - span + tk - 1, 0) // tk
        a = jnp.minimum(jnp.maximum(a, j_lo), b)
    tph = (SEL_HALF_BLOCKS * NSA_SLC_BLOCK) // tk
    for hf in range(n_halves):
        Q = lhs[hf]
        if mode == "nsa_sel":
            clip = lambda lo, hi: (jnp.maximum(lo, hf * tph) if hf else lo, jnp.minimum(hi, (hf + 1) * tph))
        else:
            clip = lambda lo, hi: (lo, hi)
        if span is not None:
            run(*clip(j_lo, a), tk, Q, True)
        run_visible(*clip(a, b), Q)
        run(*clip(b, j_hi + 1), tk, Q, True)

    l = l_sc[...]
    acc = acc_sc[...] / l
    if mode == "diff":
        o = acc[:tq] - lam_ref[...] * acc[tq:]
        o = o * lax.rsqrt(jnp.mean(o * o, axis=-1, keepdims=True) + EPS) * (gout_ref[...] * out_scale)
        o_ref[...] = o.astype(o_ref.dtype)
    else:
        for g in range(G):
            o_ref[:, g * dv:(g + 1) * dv] = acc[g * tq:(g + 1) * tq].astype(o_ref.dtype)
        if want_lse:
            lse = m_sc[...] + jnp.log(l)
            for g in range(G):
                lse_ref[:, g * LANE:(g + 1) * LANE] = lse[g * tq:(g + 1) * tq]


def attention(q, k, v, *, n_streams, seq, out_cols, qcol, kcol, vcol, ocol, mode="plain", tq, tk, tkb=None, G=1,
              dq, dk, dv=LANE, span=None, bias=None, bcol=None, sel=None, lam=None, gout=None, out_scale=1.0,
              want_lse=False, bias_far_zero=False, name="attention"):
    tq = min(tq, seq)
    tk = min(tk, seq)
    tkb = tk if tkb is None else min(tkb, seq)
    assert seq % tq == 0 and seq % tkb == 0 and tkb % tk == 0 and tq % LANE == 0 and tk % LANE == 0
    resident = dict(pipeline_mode=pl.Buffered(1))
    R = 2 * tq if mode == "diff" else G * tq
    Gk = 2 if mode == "diff" else G
    wq = dq if mode == "diff" else G * dq
    in_specs = [pl.BlockSpec((tq, wq), lambda n, i: (i, qcol(n)))]
    args = [q]
    n_halves = 1
    if mode == "nsa_sel":
        n_halves = sel.shape[0]
        in_specs.append(pl.BlockSpec((n_halves, tq, LANE), lambda n, i: (0, i, n)))
        args.append(sel)
    in_specs += [pl.BlockSpec((seq, dk), lambda n, i: (0, kcol(n)), **resident),
                 pl.BlockSpec((seq, dv), lambda n, i: (0, vcol(n)), **resident)]
    args += [k, v]
    x0 = 0
    if bias is not None:
        x0 = _strip_x0(tkb)
        gb = 1 if mode == "diff" else G
        in_specs.append(pl.BlockSpec((gb, tq, x0 + max(tq, tk)), lambda n, i: (bcol(n), 0, 0), **resident))
        args.append(bias)
    if mode == "diff":
        in_specs += [pl.BlockSpec((1, LANE), lambda n, i: (0, 0)), pl.BlockSpec((1, LANE), lambda n, i: (0, 0))]
        args += [lam, gout]
    wo = dv if mode == "diff" else G * dv
    out_specs = [pl.BlockSpec((tq, wo), lambda n, i: (i, ocol(n)))]
    out_shape = [jax.ShapeDtypeStruct((seq, out_cols), BF16)]
    if want_lse:
        out_specs.append(pl.BlockSpec((tq, G * LANE), lambda n, i: (i, ocol(n))))
        out_shape.append(jax.ShapeDtypeStruct((seq, out_cols), F32))
    kern = functools.partial(_attn_kernel, mode=mode, tq=tq, tk=tk, tkb=tkb, G=Gk, dq=dq, dv=dv, span=span,
                             has_bias=bias is not None, far_zero=bias_far_zero, x0=x0, n_halves=n_halves,
                             want_lse=want_lse,
                             out_scale=out_scale)
    res = pl.pallas_call(
        kern,
        grid=(n_streams, seq // tq),
        in_specs=in_specs,
        out_specs=out_specs,
        out_shape=out_shape,
        scratch_shapes=[pltpu.VMEM((R, LANE), F32), pltpu.VMEM((R, LANE), F32), pltpu.VMEM((R, dv), F32)],
        compiler_params=_cp(("parallel", "parallel")),
        name=name,
    )(*args)
    return res if want_lse else res[0]


def _banded_kernel(q_ref, k_ref, v_ref, bias_ref, o_ref, lse_ref, m_sc, l_sc, acc_sc, *, heads, tq, span, x0):
    q0 = pl.program_id(1) * tq
    tk = tq
    m_sc[...] = jnp.full_like(m_sc, NEG)
    l_sc[...] = jnp.zeros_like(l_sc)
    acc_sc[...] = jnp.zeros_like(acc_sc)

    def body(j, c):
        k0 = pl.multiple_of(j * tk, tk)
        rel = (q0 - k0) + lax.broadcasted_iota(jnp.int32, (tq, tk), 0) - lax.broadcasted_iota(jnp.int32, (tq, tk), 1)
        vis = jnp.logical_and(rel >= 0, rel <= span)
        start = pl.multiple_of(jnp.maximum(x0 - (q0 - k0), 0), LANE)
        for h in range(heads):
            cs = slice(h * LANE, (h + 1) * LANE)
            s = lax.dot_general(q_ref[:, cs], k_ref[pl.ds(k0, tk), cs], (((1,), (1,)), ((), ())),
                                preferred_element_type=F32)
            s = jnp.where(vis, s + bias_ref[h, :, pl.ds(start, tk)], NEG)
            m_prev = m_sc[h]
            m_new = jnp.maximum(m_prev, jnp.max(s, axis=-1, keepdims=True))
            alpha = jnp.exp(m_prev - m_new)
            p = jnp.exp(s - jnp.tile(m_new, (1, tk // LANE)))
            l_sc[h] = alpha * l_sc[h] + jnp.sum(p, axis=-1, keepdims=True)
            vt = v_ref[pl.ds(k0, tk), cs]
            acc_sc[h] = alpha * acc_sc[h] + jnp.dot(p.astype(vt.dtype), vt, preferred_element_type=F32)
            m_sc[h] = m_new
        return c

    lax.fori_loop(jnp.maximum(q0 - span, 0) // tk, (q0 + tq - 1) // tk + 1, body, 0)
    for h in range(heads):
        cs = slice(h * LANE, (h + 1) * LANE)
        l = l_sc[h]
        o_ref[:, cs] = (acc_sc[h] / l).astype(o_ref.dtype)
        lse_ref[:, cs] = m_sc[h] + jnp.log(l)


def banded_attention(q, k, v, bias, *, streams, seq, tq, span, qcol, vcol, name):
    heads = bias.shape[0]
    width = heads * LANE
    tq = min(tq, seq)
    assert seq % tq == 0
    x0 = _strip_x0(tq)
    resident = dict(pipeline_mode=pl.Buffered(1))
    return pl.pallas_call(
        functools.partial(_banded_kernel, heads=heads, tq=tq, span=span, x0=x0),
        grid=(streams, seq // tq),
        in_specs=[pl.BlockSpec((tq, width), lambda r, i: (i, qcol(r))),
                  pl.BlockSpec((seq, width), lambda r, i: (0, qcol(r)), **resident),
                  pl.BlockSpec((seq, width), lambda r, i: (0, vcol(r)), **resident),
                  pl.BlockSpec((heads, tq, x0 + tq), lambda r, i: (0, 0, 0), **resident)],
        out_specs=[pl.BlockSpec((tq, width), lambda r, i: (i, r)),
                   pl.BlockSpec((tq, width), lambda r, i: (i, r))],
        out_shape=[jax.ShapeDtypeStruct((seq, streams * width), BF16),
                   jax.ShapeDtypeStruct((seq, streams * width), F32)],
        scratch_shapes=[pltpu.VMEM((heads, tq, LANE), F32), pltpu.VMEM((heads, tq, LANE), F32),
                        pltpu.VMEM((heads, tq, LANE), F32)],
        compiler_params=_cp(("parallel", "parallel")),
        name=name,
    )(q, k, v, bias)


def _dil_combine_kernel(o0, o1, o2, l0, l1, l2, out_ref):
    os_ = (o0, o1, o2)
    ls_ = (l0, l1, l2)
    for j in range(DIL_HP):
        sl = slice(j * LANE, (j + 1) * LANE)
        lse = [r[:, sl] for r in ls_]
        m = jnp.maximum(jnp.maximum(lse[0], lse[1]), lse[2])
        e = [jnp.exp(x - m) for x in lse]
        inv = 1.0 / (e[0] + e[1] + e[2])
        for g in range(len(DIL_PATTERNS)):
            c = (g * DIL_HP + j) * LANE
            out_ref[:, c:c + LANE] = (os_[g][:, sl].astype(F32) * (e[g] * inv)).astype(out_ref.dtype)


def dil_combine(outs, lses, tm=512):
    s = outs[0].shape[0]
    tm = min(tm, s)
    w = DIL_HP * LANE
    spec = pl.BlockSpec((tm, w), lambda i: (i, 0))
    return pl.pallas_call(
        _dil_combine_kernel,
        grid=(s // tm,),
        in_specs=[spec] * 6,
        out_specs=pl.BlockSpec((tm, DIL_HEADS * LANE), lambda i: (i, 0)),
        out_shape=jax.ShapeDtypeStruct((s, DIL_HEADS * LANE), BF16),
        compiler_params=_cp(("parallel",)),
        name="dil_combine",
    )(*outs, *lses)


def _gelu_tanh(y):
    return 0.5 * y * (1.0 + jnp.tanh(0.7978845608028654 * (y + 0.044715 * y * y * y)))


def _compress_kernel(a_ref, pe_ref, w1_ref, w2_ref, g_ref, o_ref, *, norm, transpose):
    a = a_ref[...].astype(F32)
    nc = a.shape[0]
    lo = jnp.dot((a + pe_ref[0]).astype(BF16), w1_ref[0], preferred_element_type=F32)
    hi = jnp.dot((a + pe_ref[1]).astype(BF16), w1_ref[1], preferred_element_type=F32)
    y = lo + pltpu.roll(hi, nc - 1, 0)
    z = jnp.dot(_gelu_tanh(y).astype(BF16), w2_ref[...], preferred_element_type=F32)
    if norm:
        z = _group_norm(z, LANE) * g_ref[...]
    o_ref[...] = (z.T if transpose else z).astype(o_ref.dtype)


def nsa_compress(chunks, pe, w1, w2, gain, *, norm, transpose):
    kvh, nc, w = chunks.shape
    half = NSA_CMP_STRIDE * LANE
    pe2 = pe.astype(F32).reshape(2, 1, half)
    w1s = w1.reshape(2, half, LANE).astype(BF16)
    oshape = (kvh, LANE, nc) if transpose else (kvh, nc, LANE)
    oblock = (None, LANE, nc) if transpose else (None, nc, LANE)
    return pl.pallas_call(
        functools.partial(_compress_kernel, norm=norm, transpose=transpose),
        grid=(kvh,),
        in_specs=[pl.BlockSpec((None, nc, w), lambda h: (h, 0, 0)),
                  pl.BlockSpec((2, 1, half), lambda h: (0, 0, 0)),
                  pl.BlockSpec((2, half, LANE), lambda h: (0, 0, 0)),
                  pl.BlockSpec((LANE, LANE), lambda h: (0, 0)),
                  pl.BlockSpec((1, LANE), lambda h: (0, 0))],
        out_specs=pl.BlockSpec(oblock, lambda h: (h, 0, 0)),
        out_shape=jax.ShapeDtypeStruct(oshape, BF16),
        compiler_params=_cp(("parallel",)),
        name="nsa_compress",
    )(chunks, pe2, w1s, w2.astype(BF16), gain.reshape(1, LANE).astype(F32))


def _cmp_topk_kernel(q_ref, kc_ref, vt_ref, ov_ref, strip_ref, o_ref, sel_ref, *, tq, y0, k_sel, n_halves):
    qi = pl.program_id(1)
    q0 = qi * tq
    G = NSA_GROUP
    nc = kc_ref.shape[0]
    qs = jnp.concatenate([q_ref[:, g * LANE:(g + 1) * LANE] for g in range(G)], axis=0)
    st = lax.dot_general(kc_ref[...], qs, (((1,), (1,)), ((), ())), preferred_element_type=F32)
    n_io = lax.broadcasted_iota(jnp.int32, (nc, tq), 0)
    i_io = lax.broadcasted_iota(jnp.int32, (nc, tq), 1)
    vis = (q0 + i_io - NSA_CMP_STRIDE * n_io - (NSA_CMP_BLOCK - 1)) >= 0
    start = pl.multiple_of(y0 - qi * (tq // NSA_CMP_STRIDE), 8)
    psum = jnp.zeros((nc, tq), F32)
    pts = []
    for g in range(G):
        s = st[:, g * tq:(g + 1) * tq] + strip_ref[g, pl.ds(start, nc), :]
        s = jnp.where(vis, s, NEG)
        m = jnp.max(s, axis=0, keepdims=True)
        e = jnp.where(vis, jnp.exp(s - m), 0.0)
        ssum = jnp.sum(e, axis=0, keepdims=True)
        p = e / jnp.where(ssum > 0, ssum, 1.0)
        psum = psum + p
        pts.append(p.astype(BF16))
    pt = jnp.concatenate(pts, axis=1)
    ot = jnp.dot(vt_ref[...], pt, preferred_element_type=F32)
    for g in range(G):
        o_ref[:, g * LANE:(g + 1) * LANE] = ot[:, g * tq:(g + 1) * tq].T.astype(o_ref.dtype)
    p_hi = psum.astype(BF16)
    p_lo = (psum - p_hi.astype(F32)).astype(BF16)
    imp = (jnp.dot(ov_ref[...], p_hi, preferred_element_type=F32)
           + jnp.dot(ov_ref[...], p_lo, preferred_element_type=F32))
    nb = imp.shape[0]
    b_io = lax.broadcasted_iota(jnp.int32, (nb, tq), 0)
    qblk = (q0 + lax.broadcasted_iota(jnp.int32, (nb, tq), 1)) // NSA_SLC_BLOCK
    forced = (b_io == 0) | (b_io == qblk) | (b_io == qblk - 1)
    val = jnp.where(forced, -1.0, jnp.where(b_io <= qblk, imp, -1.0))
    seln = jnp.where(forced, 0.0, SEL_NEG)
    for _ in range(k_sel - 3):
        mx = jnp.max(val, axis=0, keepdims=True)
        idx = jnp.min(jnp.where(val == mx, b_io, nb), axis=0, keepdims=True)
        pick = (b_io == idx) & (mx >= 0.0)
        seln = jnp.where(pick, 0.0, seln)
        val = jnp.where(pick, -2.0, val)
    sel = seln.T
    for hf in range(n_halves):
        sel_ref[hf] = sel[:, hf * LANE:(hf + 1) * LANE].astype(sel_ref.dtype)


def _cmp_y0(s):
    return (LANE // NSA_CMP_STRIDE) * (s // LANE - 1)


def cmp_bias_strips(table, s):
    nc = s // NSA_CMP_STRIDE
    y0 = _cmp_y0(s)
    far_rows = -(-(REL_MAX_DIST + NSA_CMP_BLOCK) // NSA_CMP_STRIDE)
    ya = max(y0 - far_rows, 0)
    yb = min(y0 + LANE // NSA_CMP_STRIDE, y0 + nc)
    y = np.arange(ya, yb)[:, None]
    i = np.arange(LANE)[None, :]
    band = jnp.transpose(table.astype(F32)[_rel_bucket_np(NSA_CMP_STRIDE * (y0 - y) + i - (NSA_CMP_BLOCK - 1))], (2, 0, 1))
    h = table.shape[1]
    far = jnp.broadcast_to(table.astype(F32)[REL_BUCKETS - 1][:, None, None], (h, ya, LANE))
    return jnp.concatenate([far, band, jnp.zeros((h, y0 + nc - yb, LANE), F32)], axis=1)


def nsa_cmp_topk(nqn, k_cmp, v_cmp_t, strips, s):
    tq = LANE
    nc = s // NSA_CMP_STRIDE
    n_cmp = nc - NSA_CMP_BLOCK // NSA_CMP_STRIDE + 1
    n_slc = s // NSA_SLC_BLOCK
    nbp = -(-n_slc // LANE) * LANE
    n_halves = nbp // LANE
    cs = np.arange(nc) * NSA_CMP_STRIDE
    ce = cs + NSA_CMP_BLOCK - 1
    ss = np.arange(nbp) * NSA_SLC_BLOCK
    ov = ((cs[None, :] < ss[:, None] + NSA_SLC_BLOCK) & (ce[None, :] >= ss[:, None])).astype(np.float32)
    ov[:, n_cmp:] = 0.0
    ov[n_slc:, :] = 0.0
    y0 = _cmp_y0(s)
    assert n_slc >= NSA_TOPK >= 3
    k_sel = NSA_TOPK
    return pl.pallas_call(
        functools.partial(_cmp_topk_kernel, tq=tq, y0=y0, k_sel=k_sel, n_halves=n_halves),
        grid=(NSA_KV_HEADS, s // tq),
        in_specs=[pl.BlockSpec((tq, NSA_GROUP * LANE), lambda h, i: (i, h)),
                  pl.BlockSpec((None, nc, LANE), lambda h, i: (h, 0, 0)),
                  pl.BlockSpec((None, LANE, nc), lambda h, i: (h, 0, 0)),
                  pl.BlockSpec((nbp, nc), lambda h, i: (0, 0)),
                  pl.BlockSpec((NSA_GROUP, y0 + nc, tq), lambda h, i: (h, 0, 0))],
        out_specs=[pl.BlockSpec((tq, NSA_GROUP * LANE), lambda h, i: (i, h)),
                   pl.BlockSpec((n_halves, tq, LANE), lambda h, i: (0, i, h))],
        out_shape=[jax.ShapeDtypeStruct((s, NSA_HEADS * LANE), BF16),
                   jax.ShapeDtypeStruct((n_halves, s, NSA_KV_HEADS * LANE), BF16)],
        compiler_params=_cp(("parallel", "parallel")),
        name="nsa_cmp_topk",
    )(nqn, k_cmp, v_cmp_t, jnp.asarray(ov, BF16), strips)


def _nsa_combine_kernel(oc_ref, os_ref, ow_ref, g_ref, o_ref):
    gate = jax.nn.sigmoid(g_ref[...].astype(F32))
    for h in range(NSA_HEADS):
        sl = slice(h * LANE, (h + 1) * LANE)
        o = (gate[:, 3 * h:3 * h + 1] * oc_ref[:, sl].astype(F32)
             + gate[:, 3 * h + 1:3 * h + 2] * os_ref[:, sl].astype(F32)
             + gate[:, 3 * h + 2:3 * h + 3] * ow_ref[:, sl].astype(F32))
        o_ref[:, sl] = o.astype(o_ref.dtype)


def nsa_combine(o_c, o_s, o_w, proj, tm=512):
    s = o_c.shape[0]
    tm = min(tm, s)
    w = NSA_HEADS * LANE
    spec = pl.BlockSpec((tm, w), lambda i: (i, 0))
    return pl.pallas_call(
        _nsa_combine_kernel,
        grid=(s // tm,),
        in_specs=[spec, spec, spec, pl.BlockSpec((tm, LANE), lambda i: (i, C_NG // LANE))],
        out_specs=spec,
        out_shape=jax.ShapeDtypeStruct((s, w), BF16),
        compiler_params=_cp(("parallel",)),
        name="nsa_combine",
    )(o_c, o_s, o_w, proj)


def _xattn_kernel(q_ref, kv_ref, gq_ref, gk_ref, o_ref, *, scale):
    nk = XATTN_HEADS * LANE
    for h in range(XATTN_HEADS):
        sl = slice(h * LANE, (h + 1) * LANE)
        qn = (_group_norm(q_ref[:, sl].astype(F32), LANE) * (gq_ref[...] * scale)).astype(BF16)
        kn = (_group_norm(kv_ref[:, sl].astype(F32), LANE) * gk_ref[...]).astype(BF16)
        s = lax.dot_general(qn, kn, (((1,), (1,)), ((), ())), preferred_element_type=F32)
        m = jnp.max(s, axis=-1, keepdims=True)
        e = jnp.exp(s - m)
        p = e / jnp.sum(e, axis=-1, keepdims=True)
        v = kv_ref[:, nk + h * LANE:nk + (h + 1) * LANE]
        o_ref[:, sl] = jnp.dot(p.astype(v.dtype), v, preferred_element_type=F32).astype(o_ref.dtype)


def xattn_core(qx, kv, qk_norm, tq=512):
    s, w = qx.shape
    tq = min(tq, s)
    m = kv.shape[0]
    return pl.pallas_call(
        functools.partial(_xattn_kernel, scale=HEAD_DIM ** -0.5),
        grid=(s // tq,),
        in_specs=[pl.BlockSpec((tq, w), lambda i: (i, 0)),
                  pl.BlockSpec((m, 2 * w), lambda i: (0, 0)),
                  pl.BlockSpec((1, LANE), lambda i: (0, 0)),
                  pl.BlockSpec((1, LANE), lambda i: (0, 0))],
        out_specs=pl.BlockSpec((tq, w), lambda i: (i, 0)),
        out_shape=jax.ShapeDtypeStruct((s, w), BF16),
        compiler_params=_cp(("parallel",)),
        name="xattn_core",
    )(qx, kv, qk_norm[0].reshape(1, LANE).astype(F32), qk_norm[1].reshape(1, LANE).astype(F32))


def _pack_w_in(w):
    d = w.shape[0]
    z = lambda n: jnp.zeros((d, n), w.dtype)
    parts = [w[:, 0:896], w[:, 1408:1472], z(64), w[:, 896:1408], w[:, 1472:7488],
             w[:, 7488:7512], z(LANE - 3 * NSA_HEADS), w[:, 7512:10200], z(NP_COLS - 10368)]
    return jnp.concatenate(parts, axis=1).astype(BF16)


def _mla_mixer(proj, rope, l, p):
    s = proj.shape[0]
    qm = mla_q_proj(proj, rope, p["mla_q_norm"][l], p["mla_w_uq"][l], p["mla_nope_norm"][l, 0], p["mla_rope_norm"][l, 0])
    km, vm = mla_kv_proj(proj, rope, p["mla_kv_norm"][l], p["mla_w_ukv"][l], p["mla_nope_norm"][l, 1],
                         p["mla_rope_norm"][l, 1])
    ident = lambda n: n
    return attention(qm, km, vm, n_streams=MLA_HEADS, seq=s, out_cols=MLA_HEADS * MLA_V, qcol=ident, kcol=ident,
                     vcol=ident, ocol=ident, tq=ATT_ROWS, tk=ATT_TK, tkb=ATT_TKB, dq=2 * LANE, dk=2 * LANE,
                     name="mla_attention")


def bias_tables(rel_bias, s):
    tabs = {"dil": []}
    for g, (_, dil) in enumerate(DIL_PATTERNS):
        tq = min(DIL_TQ, s // dil)
        tabs["dil"].append(bias_strips(rel_bias[:, BIAS_DIL + g * DIL_HP:BIAS_DIL + (g + 1) * DIL_HP], tq, tq, tq, dil))
    far_zero = lambda t: t - t[REL_BUCKETS - 1:REL_BUCKETS]
    nsa = rel_bias[:, BIAS_NSA:BIAS_NSA + NSA_HEADS].astype(F32)
    tk, tkb = min(ATT_TK, s), min(ATT_TKB, s)
    tabs["nsa"] = bias_strips(far_zero(nsa), ATT_ROWS // NSA_GROUP, tk, tkb, 1)
    tabs["nsa_cmp"] = cmp_bias_strips(nsa, s)
    diff = rel_bias[:, BIAS_DIFF:BIAS_DIFF + DIFF_HEADS].astype(F32)
    tabs["diff"] = bias_strips(far_zero(diff), min(ATT_ROWS // 2, s), tk, tkb, 1)
    return tabs


def _dilated_mixer(proj, l, p, tabs):
    s = proj.shape[0]
    w = DIL_HP * LANE
    dqn = head_rmsnorm(proj, C_DQ, DIL_HEADS, p["dil_qk_norm"][l, 0], LANE, HEAD_DIM ** -0.5)
    dkn = head_rmsnorm(proj, C_DK, DIL_HEADS, p["dil_qk_norm"][l, 1], LANE, 1.0)
    outs, lses = [], []
    ident = lambda n: n
    for g, (window, dil) in enumerate(DIL_PATTERNS):
        n_sub = s // dil
        tq = min(DIL_TQ, n_sub)
        if dil == 1:
            qkv = (dqn, dkn, proj)
            qcol = lambda r, g=g: g
            vcol = lambda r, g=g: C_DV // w + g
        else:
            fold = lambda t, c0: t[:, c0 + g * w:c0 + (g + 1) * w].reshape(n_sub, dil * w)
            qkv = (fold(dqn, 0), fold(dkn, 0), fold(proj, C_DV))
            qcol = vcol = ident
        o, lse = banded_attention(*qkv, tabs["dil"][g], streams=dil, seq=n_sub, tq=tq, span=window // dil,
                                  qcol=qcol, vcol=vcol, name=f"dil_attention_{g}")
        outs.append(o.reshape(s, w))
        lses.append(lse.reshape(s, w))
    return dil_combine(outs, lses)


def _nsa_mixer(proj, l, p, tabs):
    s = proj.shape[0]
    gains = p["nsa_qk_norm"][l]
    nqn = head_rmsnorm(proj, C_NQ, NSA_HEADS, gains[0], LANE, HEAD_DIM ** -0.5)
    kwn = head_rmsnorm(proj, C_NKW, NSA_KV_HEADS, gains[3], LANE, 1.0)
    ksel = nsa_sel_keys(proj, gains[2])
    nc = s // NSA_CMP_STRIDE

    def chunks(c0):
        t = proj[:, c0:c0 + NSA_KV_HEADS * LANE].reshape(nc, NSA_CMP_STRIDE, NSA_KV_HEADS, LANE)
        return jnp.transpose(t, (2, 0, 1, 3)).reshape(NSA_KV_HEADS, nc, NSA_CMP_STRIDE * LANE)

    k_cmp = nsa_compress(chunks(C_NKC), p["nsa_cmp_pe"][l, 0], p["nsa_cmp_w1"][l, 0], p["nsa_cmp_w2"][l, 0],
                         gains[1], norm=True, transpose=False)
    v_cmp_t = nsa_compress(chunks(C_NVC), p["nsa_cmp_pe"][l, 1], p["nsa_cmp_w1"][l, 1], p["nsa_cmp_w2"][l, 1],
                           gains[1], norm=False, transpose=True)
    o_c, sel = nsa_cmp_topk(nqn, k_cmp, v_cmp_t, tabs["nsa_cmp"], s)
    ident = lambda n: n
    common = dict(n_streams=NSA_KV_HEADS, seq=s, out_cols=NSA_HEADS * LANE, qcol=ident, ocol=ident,
                  tq=ATT_ROWS // NSA_GROUP, tk=ATT_TK, tkb=ATT_TKB, G=NSA_GROUP, dq=LANE, bias=tabs["nsa"], bcol=ident,
                  bias_far_zero=True)
    o_s = attention(nqn, ksel, proj, kcol=ident, vcol=lambda n: C_NVS // LANE + n, mode="nsa_sel", dk=2 * LANE,
                    sel=sel, name="nsa_sel_attention", **common)
    o_w = attention(nqn, kwn, proj, kcol=ident, vcol=lambda n: C_NVW // LANE + n, dk=LANE, span=NSA_WINDOW - 1,
                    name="nsa_win_attention", **common)
    return nsa_combine(o_c, o_s, o_w, proj)


def _diff_mixer(proj, l, p, tabs):
    s = proj.shape[0]
    fqn = head_rmsnorm(proj, C_FQ, DIFF_HEADS, p["diff_qk_norm"][l, 0], DIFF_QK, DIFF_QK ** -0.5)
    fkn = head_rmsnorm(proj, C_FK, DIFF_HEADS, p["diff_qk_norm"][l, 1], DIFF_QK, 1.0)
    lam_init = 0.8 - 0.6 * math.exp(-0.3 * l)
    lv = p["diff_lambda"][l].astype(F32)
    lam = jnp.exp(jnp.sum(lv[0] * lv[1])) - jnp.exp(jnp.sum(lv[2] * lv[3])) + lam_init
    ident = lambda n: n
    return attention(fqn, fkn, proj, n_streams=DIFF_HEADS, seq=s, out_cols=DIFF_HEADS * DIFF_V, qcol=ident,
                     kcol=ident, vcol=lambda n: C_FV // LANE + n, ocol=ident, mode="diff", tq=ATT_ROWS // 2,
                     tk=ATT_TK, tkb=ATT_TKB, dq=LANE, dk=LANE, bias=tabs["diff"], bcol=ident, bias_far_zero=True,
                     lam=jnp.full((1, LANE), lam, F32),
                     gout=p["diff_out_norm"][l].reshape(1, LANE).astype(F32), out_scale=1.0 - lam_init,
                     name="diff_attention")


def kernel(x, mem, positions, rel_bias, norm_mix, w_in, mla_q_norm, mla_kv_norm, mla_w_uq, mla_w_ukv, mla_nope_norm, mla_rope_norm, dil_qk_norm, nsa_qk_norm, nsa_cmp_pe, nsa_cmp_w1, nsa_cmp_w2, diff_qk_norm, diff_lambda, diff_out_norm, w_out, norm_xattn, norm_mem, xattn_wq, xattn_wkv, xattn_qk_norm, xattn_wo, norm_ffn, ffn_w_gate, ffn_w_up, ffn_conv_w, ffn_conv_b, ffn_w_down):
    p = dict(mla_q_norm=mla_q_norm, mla_kv_norm=mla_kv_norm, mla_w_uq=mla_w_uq, mla_w_ukv=mla_w_ukv,
             mla_nope_norm=mla_nope_norm, mla_rope_norm=mla_rope_norm, dil_qk_norm=dil_qk_norm,
             nsa_qk_norm=nsa_qk_norm, nsa_cmp_pe=nsa_cmp_pe, nsa_cmp_w1=nsa_cmp_w1, nsa_cmp_w2=nsa_cmp_w2,
             diff_qk_norm=diff_qk_norm, diff_lambda=diff_lambda, diff_out_norm=diff_out_norm)
    b, s, d = x.shape
    assert b == 1
    xs = x.reshape(s, d).astype(F32)
    mems = mem.reshape(mem.shape[1], d).astype(F32)
    pos = positions.reshape(s, 1).astype(jnp.int32)
    tabs = bias_tables(rel_bias, s)
    rope = rope_tables(pos)
    for l in range(DEPTH):
        h = rmsnorm_rows(xs, norm_mix[l])
        proj = matmul(h, _pack_w_in(w_in[l]), tm=512, tn=1536, out_dtype=BF16, name="in_proj")
        mix = [_mla_mixer(proj, rope, l, p),_dilated_mixer(proj, l, p, tabs), _nsa_mixer(proj, l, p, tabs),
               _diff_mixer(proj, l, p, tabs)]
        xs = matmul_cat(mix, w_out[l].astype(BF16), xs, tm=512, tn=1024, name="out_proj")
        hx = rmsnorm_rows(xs, norm_xattn[l])
        qx = matmul(hx, xattn_wq[l].astype(BF16), tm=1024, tn=512, out_dtype=BF16, name="xattn_q")
        kvm = matmul(rmsnorm_rows(mems, norm_mem[l]), xattn_wkv[l].astype(BF16), tm=256, tn=1024, out_dtype=BF16,
                     name="xattn_kv")
        ox = xattn_core(qx, kvm, xattn_qk_norm[l])
        xs = matmul(ox, xattn_wo[l].astype(BF16), tm=1024, tn=1024, out_dtype=F32, residual=xs, name="xattn_o")
        hf = rmsnorm_rows(xs, norm_ffn[l])
        act = ffn_gate_up(hf, ffn_w_gate[l].astype(BF16), ffn_w_up[l].astype(BF16), ffn_conv_w[l], ffn_conv_b[l])
        xs = matmul(act, ffn_w_down[l].astype(BF16), tm=512, tn=512, out_dtype=F32, residual=xs, name="ffn_down")
    return xs.reshape(b, s, d)
```

```python
import functools
import math

import numpy as np
import jax
import jax.numpy as jnp
from jax import lax
from jax.experimental import pallas as pl
from jax.experimental.pallas import tpu as pltpu

F32 = jnp.float32
BF16 = jnp.bfloat16

D_MODEL = 4096
DEPTH = 2
HEAD_DIM = 128
EPS = 1e-6
MLA_HEADS = 8
MLA_Q_RANK = 896
MLA_KV_RANK = 512
MLA_NOPE = 128
MLA_ROPE = 64
MLA_V = 128
ROPE_THETA = 10000.0
DIL_PATTERNS = ((128, 1), (512, 4), (2048, 16))
DIL_HP = 3
DIL_HEADS = DIL_HP * len(DIL_PATTERNS)
NSA_HEADS = 8
NSA_KV_HEADS = 2
NSA_GROUP = NSA_HEADS // NSA_KV_HEADS
NSA_CMP_BLOCK = 32
NSA_CMP_STRIDE = 16
NSA_SLC_BLOCK = 64
NSA_TOPK = 16
NSA_WINDOW = 512
DIFF_HEADS = 7
DIFF_QK = 64
DIFF_V = 128
XATTN_HEADS = 4
FFN_DIM = 11008
REL_BUCKETS = 32
REL_MAX_DIST = 2048
BIAS_DIL = 0
BIAS_NSA = DIL_HEADS
BIAS_DIFF = DIL_HEADS + NSA_HEADS

LANE = 128
VMEM_LIMIT = 56 * 1024 * 1024
NEG = -1e30
SEL_NEG = -float(2 ** 30)
SEL_HALF_BLOCKS = 128
ATT_ROWS = 1024
ATT_TK = 1024
ATT_TKB = 2048
DIL_TQ = 512
WIN_TK = 512

C_CQ = 0
C_KR = 896
C_CKV = 1024
C_DQ = 1536
C_DK = 2688
C_DV = 3840
C_NQ = 4992
C_NKC = 6016
C_NVC = 6272
C_NKS = 6528
C_NVS = 6784
C_NKW = 7040
C_NVW = 7296
C_NG = 7552
C_FQ = 7680
C_FK = 8576
C_FV = 9472
NP_COLS = 10752


def _cp(sem):
    return pltpu.CompilerParams(dimension_semantics=sem, vmem_limit_bytes=VMEM_LIMIT)


def _rmsnorm_rows_kernel(x_ref, g_ref, o_ref):
    x = x_ref[...].astype(F32)
    ms = jnp.mean(x * x, axis=-1, keepdims=True)
    o_ref[...] = (x * lax.rsqrt(ms + EPS) * g_ref[...]).astype(o_ref.dtype)


def rmsnorm_rows(x, g, tm=256):
    m, d = x.shape
    tm = min(tm, m)
    return pl.pallas_call(
        _rmsnorm_rows_kernel,
        grid=(m // tm,),
        in_specs=[pl.BlockSpec((tm, d), lambda i: (i, 0)),
                  pl.BlockSpec((1, d), lambda i: (0, 0))],
        out_specs=pl.BlockSpec((tm, d), lambda i: (i, 0)),
        out_shape=jax.ShapeDtypeStruct((m, d), BF16),
        compiler_params=_cp(("parallel",)),
        name="rmsnorm_rows",
    )(x, g.reshape(1, d).astype(F32))


def _matmul_kernel(*refs, has_res):
    if has_res:
        a_ref, w_ref, r_ref, o_ref = refs
    else:
        a_ref, w_ref, o_ref = refs
    acc = jnp.dot(a_ref[...], w_ref[...], preferred_element_type=F32)
    if has_res:
        acc = acc + r_ref[...]
    o_ref[...] = acc.astype(o_ref.dtype)


def _matmul_cat_kernel(*refs, offs):
    a_refs = refs[:len(offs)]
    w_ref, r_ref, o_ref = refs[len(offs):]
    acc = r_ref[...]
    for a_ref, (k0, k1) in zip(a_refs, offs):
        acc = acc + jnp.dot(a_ref[...], w_ref[k0:k1, :], preferred_element_type=F32)
    o_ref[...] = acc.astype(o_ref.dtype)


def matmul_cat(a_list, w, residual, *, tm, tn, name="matmul_cat"):
    m = a_list[0].shape[0]
    k, n = w.shape
    offs, k0 = [], 0
    for a in a_list:
        offs.append((k0, k0 + a.shape[1]))
        k0 += a.shape[1]
    tm = min(tm, m)
    assert k0 == k and m % tm == 0 and n % tn == 0
    in_specs = [pl.BlockSpec((tm, a.shape[1]), lambda j, i: (i, 0)) for a in a_list]
    in_specs += [pl.BlockSpec((k, tn), lambda j, i: (0, j)), pl.BlockSpec((tm, tn), lambda j, i: (i, j))]
    return pl.pallas_call(
        functools.partial(_matmul_cat_kernel, offs=tuple(offs)),
        grid=(n // tn, m // tm),
        in_specs=in_specs,
        out_specs=pl.BlockSpec((tm, tn), lambda j, i: (i, j)),
        out_shape=jax.ShapeDtypeStruct((m, n), F32),
        compiler_params=_cp(("parallel", "parallel")),
        name=name,
    )(*a_list, w, residual)


def matmul(a, w, *, tm, tn, out_dtype, residual=None, name="matmul"):
    m, k = a.shape
    n = w.shape[1]
    tm = min(tm, m)
    tn = min(tn, n)
    assert m % tm == 0 and n % tn == 0
    in_specs = [pl.BlockSpec((tm, k), lambda j, i: (i, 0)),
                pl.BlockSpec((k, tn), lambda j, i: (0, j))]
    args = [a, w]
    if residual is not None:
        in_specs.append(pl.BlockSpec((tm, tn), lambda j, i: (i, j)))
        args.append(residual)
    return pl.pallas_call(
        functools.partial(_matmul_kernel, has_res=residual is not None),
        grid=(n // tn, m // tm),
        in_specs=in_specs,
        out_specs=pl.BlockSpec((tm, tn), lambda j, i: (i, j)),
        out_shape=jax.ShapeDtypeStruct((m, n), out_dtype),
        compiler_params=_cp(("parallel", "parallel")),
        name=name,
    )(*args)


def _ffn_gate_up_kernel(h_ref, wg_ref, wu_ref, cw_ref, cb_ref, o_ref, carry_ref, wgb_ref, wub_ref):
    i = pl.program_id(1)

    @pl.when(i == 0)
    def _():
        carry_ref[...] = jnp.zeros_like(carry_ref)
        wgb_ref[...] = wg_ref[...].astype(BF16)
        wub_ref[...] = wu_ref[...].astype(BF16)

    h = h_ref[...]
    u = jnp.dot(h, wgb_ref[...], preferred_element_type=F32)
    up = jnp.dot(h, wub_ref[...], preferred_element_type=F32)
    tm = u.shape[0]
    prev = carry_ref[...]
    row = lax.broadcasted_iota(jnp.int32, u.shape, 0)
    u1 = jnp.where(row == 0, prev[7:8, :], pltpu.roll(u, 1, 0))
    u2 = pltpu.roll(u, 2, 0)
    u2 = jnp.where(row == 0, prev[6:7, :], jnp.where(row == 1, prev[7:8, :], u2))
    cw = cw_ref[...]
    g = cw[0:1, :] * u2 + cw[1:2, :] * u1 + cw[2:3, :] * u + cb_ref[...]
    o_ref[...] = (g * jax.nn.sigmoid(g) * up).astype(o_ref.dtype)
    carry_ref[...] = u[tm - 8:, :]


def ffn_gate_up(h, wg, wu, layer, conv_w, conv_b, tm=1024, tf=256):
    m, d = h.shape
    f = wg.shape[2]
    tm = min(tm, m)
    assert f % tf == 0 and m % tm == 0
    return pl.pallas_call(
        _ffn_gate_up_kernel,
        grid=(f // tf, m // tm),
        in_specs=[pl.BlockSpec((tm, d), lambda j, i: (i, 0)),
                  pl.BlockSpec((None, d, tf), lambda j, i: (layer, 0, j)),
                  pl.BlockSpec((None, d, tf), lambda j, i: (layer, 0, j)),
                  pl.BlockSpec((3, tf), lambda j, i: (0, j)),
                  pl.BlockSpec((1, tf), lambda j, i: (0, j))],
        out_specs=pl.BlockSpec((tm, tf), lambda j, i: (i, j)),
        out_shape=jax.ShapeDtypeStruct((m, f), BF16),
        scratch_shapes=[pltpu.VMEM((8, tf), F32), pltpu.VMEM((d, tf), BF16), pltpu.VMEM((d, tf), BF16)],
        compiler_params=_cp(("arbitrary", "arbitrary")),
        name="ffn_gate_up",
    )(h, wg, wu, conv_w.astype(F32), conv_b.reshape(1, f).astype(F32))


def _group_norm(x, d):
    sq = x * x
    tot = jnp.sum(sq, axis=-1, keepdims=True)
    if d == LANE:
        ms = tot / d
    else:
        lane = lax.broadcasted_iota(jnp.int32, x.shape, 1)
        lo = jnp.sum(jnp.where(lane < d, sq, 0.0), axis=-1, keepdims=True)
        ms = jnp.where(lane < d, lo, tot - lo) / d
    return x * lax.rsqrt(ms + EPS)


def _head_norm_kernel(x_ref, g_ref, o_ref, *, d, scale):
    x = x_ref[...].astype(F32)
    o_ref[...] = (_group_norm(x, d) * (g_ref[...] * scale)).astype(o_ref.dtype)


def head_rmsnorm(src, col0, ncb, gain, d, scale, tm=4096):
    s = src.shape[0]
    tm = min(tm, s)
    cb0 = col0 // LANE
    g = jnp.tile(gain.astype(F32), LANE // d).reshape(1, LANE)
    return pl.pallas_call(
        functools.partial(_head_norm_kernel, d=d, scale=scale),
        grid=(s // tm, ncb),
        in_specs=[pl.BlockSpec((tm, LANE), lambda i, c: (i, cb0 + c)),
                  pl.BlockSpec((1, LANE), lambda i, c: (0, 0))],
        out_specs=pl.BlockSpec((tm, LANE), lambda i, c: (i, c)),
        out_shape=jax.ShapeDtypeStruct((s, ncb * LANE), BF16),
        compiler_params=_cp(("parallel", "parallel")),
        name="head_rmsnorm",
    )(src, g)


def _nsa_ksel_kernel(x_ref, g_ref, o_ref):
    tm = x_ref.shape[0]
    x = x_ref[...].astype(F32)
    o_ref[:, :LANE] = (_group_norm(x, LANE) * g_ref[...]).astype(o_ref.dtype)
    row = pl.program_id(0) * tm + lax.broadcasted_iota(jnp.int32, (tm, LANE), 0)
    col = lax.broadcasted_iota(jnp.int32, (tm, LANE), 1)
    hot = ((row // NSA_SLC_BLOCK) % SEL_HALF_BLOCKS) == col
    o_ref[:, LANE:] = jnp.where(hot, 1.0, 0.0).astype(o_ref.dtype)


def nsa_sel_keys(proj, gain, tm=4096):
    s = proj.shape[0]
    tm = min(tm, s)
    cb0 = C_NKS // LANE
    return pl.pallas_call(
        _nsa_ksel_kernel,
        grid=(s // tm, NSA_KV_HEADS),
        in_specs=[pl.BlockSpec((tm, LANE), lambda i, c: (i, cb0 + c)),
                  pl.BlockSpec((1, LANE), lambda i, c: (0, 0))],
        out_specs=pl.BlockSpec((tm, 2 * LANE), lambda i, c: (i, c)),
        out_shape=jax.ShapeDtypeStruct((s, NSA_KV_HEADS * 2 * LANE), BF16),
        compiler_params=_cp(("parallel", "parallel")),
        name="nsa_sel_keys",
    )(proj, gain.reshape(1, LANE).astype(F32))


def _rope_tables_kernel(pos_ref, inv_ref, sign_ref, cos_ref, sin_ref):
    ang = pos_ref[...].astype(F32) * inv_ref[...]
    cos_ref[...] = jnp.cos(ang)
    sin_ref[...] = jnp.sin(ang) * sign_ref[...]


def rope_tables(pos, tm=2048):
    s = pos.shape[0]
    tm = min(tm, s)
    inv, sign = _rope_consts()
    const = lambda i: (0, 0)
    spec = pl.BlockSpec((tm, LANE), lambda i: (i, 0))
    return pl.pallas_call(
        _rope_tables_kernel,
        grid=(s // tm,),
        in_specs=[pl.BlockSpec((tm, 1), lambda i: (i, 0)), pl.BlockSpec((1, LANE), const), pl.BlockSpec((1, LANE), const)],
        out_specs=[spec, spec],
        out_shape=[jax.ShapeDtypeStruct((s, LANE), F32)] * 2,
        compiler_params=_cp(("parallel",)),
        name="rope_tables",
    )(pos, inv, sign)


def _rope_apply(x, cosv, sin_signed):
    lane = lax.broadcasted_iota(jnp.int32, x.shape, 1)
    half = MLA_ROPE // 2
    swapped = jnp.where(lane < half, pltpu.roll(x, LANE - half, 1), pltpu.roll(x, half, 1))
    return x * cosv + swapped * sin_signed


def _latent_norm(c_ref, g_ref):
    c = c_ref[...].astype(F32)
    ms = jnp.mean(c * c, axis=-1, keepdims=True)
    return (c * lax.rsqrt(ms + EPS) * g_ref[...]).astype(BF16)


def _mla_q_kernel(c_ref, cos_ref, sin_ref, gq_ref, w_ref, gn_ref, gr_ref, o_ref, *, scale):
    u = jnp.dot(_latent_norm(c_ref, gq_ref), w_ref[...], preferred_element_type=F32)
    cosv, sinv = cos_ref[...], sin_ref[...]
    for h in range(MLA_HEADS):
        b = 2 * LANE * h
        nope = _group_norm(u[:, b:b + LANE], LANE) * (gn_ref[...] * scale)
        rp = u[:, b + LANE:b + 2 * LANE]
        rp = rp * lax.rsqrt(jnp.sum(rp * rp, axis=-1, keepdims=True) / MLA_ROPE + EPS) * gr_ref[...]
        rp = _rope_apply(rp, cosv, sinv) * scale
        o_ref[:, b:b + LANE] = nope.astype(o_ref.dtype)
        o_ref[:, b + LANE:b + 2 * LANE] = rp.astype(o_ref.dtype)


def _mla_kv_kernel(c_ref, kr_ref, cos_ref, sin_ref, gkv_ref, w_ref, gn_ref, gr_ref, k_ref, v_ref):
    kv = jnp.dot(_latent_norm(c_ref, gkv_ref), w_ref[...], preferred_element_type=F32)
    cosv, sinv = cos_ref[...], sin_ref[...]
    kr = kr_ref[...].astype(F32)
    kr = kr * lax.rsqrt(jnp.sum(kr * kr, axis=-1, keepdims=True) / MLA_ROPE + EPS) * gr_ref[...]
    kr = _rope_apply(kr, cosv, sinv).astype(k_ref.dtype)
    nv = MLA_HEADS * LANE
    for h in range(MLA_HEADS):
        kn = _group_norm(kv[:, h * LANE:(h + 1) * LANE], LANE) * gn_ref[...]
        k_ref[:, 2 * LANE * h:2 * LANE * h + LANE] = kn.astype(k_ref.dtype)
        k_ref[:, 2 * LANE * h + LANE:2 * LANE * (h + 1)] = kr
    v_ref[...] = kv[:, nv:].astype(v_ref.dtype)


def _rope_consts():
    half = MLA_ROPE // 2
    inv = ROPE_THETA ** (-jnp.arange(half, dtype=F32) / half)
    inv = jnp.concatenate([inv, inv, jnp.zeros((LANE - MLA_ROPE,), F32)]).reshape(1, LANE)
    sign = np.zeros((1, LANE), np.float32)
    sign[0, :half] = -1.0
    sign[0, half:MLA_ROPE] = 1.0
    return inv, jnp.asarray(sign)


def _pad_lanes(v, n=LANE):
    return jnp.concatenate([v.astype(F32), jnp.zeros((n - v.shape[0],), F32)]).reshape(1, n)


def mla_q_proj(proj, rope, q_norm, w_uq, nope_gain, rope_gain, tm=512):
    s = proj.shape[0]
    tm = min(tm, s)
    w = jnp.pad(w_uq.reshape(MLA_Q_RANK, MLA_HEADS, MLA_NOPE + MLA_ROPE),
                ((0, 0), (0, 0), (0, LANE - MLA_ROPE))).reshape(MLA_Q_RANK, MLA_HEADS * 2 * LANE).astype(BF16)
    wq = MLA_HEADS * 2 * LANE
    const = lambda i: (0, 0)
    return pl.pallas_call(
        functools.partial(_mla_q_kernel, scale=(MLA_NOPE + MLA_ROPE) ** -0.5),
        grid=(s // tm,),
        in_specs=[pl.BlockSpec((tm, MLA_Q_RANK), lambda i: (i, C_CQ // MLA_Q_RANK)),
                  pl.BlockSpec((tm, LANE), lambda i: (i, 0)), pl.BlockSpec((tm, LANE), lambda i: (i, 0)),
                  pl.BlockSpec((1, MLA_Q_RANK), const),
                  pl.BlockSpec((MLA_Q_RANK, wq), const),
                  pl.BlockSpec((1, LANE), const), pl.BlockSpec((1, LANE), const)],
        out_specs=pl.BlockSpec((tm, wq), lambda i: (i, 0)),
        out_shape=jax.ShapeDtypeStruct((s, wq), BF16),
        compiler_params=_cp(("parallel",)),
        name="mla_q_proj",
    )(proj, rope[0], rope[1], q_norm.reshape(1, -1).astype(F32), w, nope_gain.reshape(1, LANE).astype(F32),
      _pad_lanes(rope_gain))


def mla_kv_proj(proj, rope, kv_norm, w_ukv, nope_gain, rope_gain, tm=512):
    s = proj.shape[0]
    tm = min(tm, s)
    w = jnp.transpose(w_ukv.reshape(MLA_KV_RANK, MLA_HEADS, 2, LANE), (0, 2, 1, 3))
    w = w.reshape(MLA_KV_RANK, 2 * MLA_HEADS * LANE).astype(BF16)
    wk = MLA_HEADS * 2 * LANE
    wv = MLA_HEADS * MLA_V
    const = lambda i: (0, 0)
    return pl.pallas_call(
        _mla_kv_kernel,
        grid=(s // tm,),
        in_specs=[pl.BlockSpec((tm, MLA_KV_RANK), lambda i: (i, C_CKV // MLA_KV_RANK)),
                  pl.BlockSpec((tm, LANE), lambda i: (i, C_KR // LANE)),
                  pl.BlockSpec((tm, LANE), lambda i: (i, 0)), pl.BlockSpec((tm, LANE), lambda i: (i, 0)),
                  pl.BlockSpec((1, MLA_KV_RANK), const),
                  pl.BlockSpec((MLA_KV_RANK, 2 * MLA_HEADS * LANE), const),
                  pl.BlockSpec((1, LANE), const), pl.BlockSpec((1, LANE), const)],
        out_specs=[pl.BlockSpec((tm, wk), lambda i: (i, 0)),
                   pl.BlockSpec((tm, wv), lambda i: (i, 0))],
        out_shape=[jax.ShapeDtypeStruct((s, wk), BF16), jax.ShapeDtypeStruct((s, wv), BF16)],
        compiler_params=_cp(("parallel",)),
        name="mla_kv_proj",
    )(proj, proj, rope[0], rope[1], kv_norm.reshape(1, -1).astype(F32), w, nope_gain.reshape(1, LANE).astype(F32),
      _pad_lanes(rope_gain))


def _rel_bucket_np(dist):
    n = np.maximum(dist, 0)
    exact = REL_BUCKETS // 2
    ratio = np.log(np.maximum(n, 1).astype(np.float32) / np.float32(exact)) / np.float32(math.log(REL_MAX_DIST / exact))
    large = np.minimum(exact + (ratio.astype(np.float32) * (REL_BUCKETS - exact)).astype(np.int32), REL_BUCKETS - 1)
    return np.where(n < exact, n, large).astype(np.int32)


def _strip_x0(tk):
    return -(-(REL_MAX_DIST - 1 + tk) // LANE) * LANE


def bias_strips(table, tq, tk, tkb, dist_scale):
    x0 = _strip_x0(tkb)
    width = x0 + max(tq, tk)
    period = -(-(width + tq) // LANE) * LANE
    t = np.arange(period)
    t = np.where(t < width, t, t - period)
    u = table.astype(F32)[_rel_bucket_np((x0 - t) * dist_scale)].T
    h = u.shape[0]
    return pl.pallas_call(
        functools.partial(_toeplitz_kernel, tq=tq, width=width),
        grid=(h,),
        in_specs=[pl.BlockSpec((None, 1, period), lambda n: (n, 0, 0))],
        out_specs=pl.BlockSpec((None, tq, width), lambda n: (n, 0, 0)),
        out_shape=jax.ShapeDtypeStruct((h, tq, width), F32),
        compiler_params=_cp(("parallel",)),
        name="bias_strips",
    )(u.reshape(h, 1, period))


def _toeplitz_kernel(u_ref, o_ref, *, tq, width):
    rows = jnp.broadcast_to(u_ref[...], (tq, u_ref.shape[1]))
    o_ref[...] = pltpu.roll(rows, 0, 1, stride=1, stride_axis=0)[:, :width]


def _attn_kernel(*refs, mode, tq, tk, tkb, G, dq, dv, span, has_bias, far_zero, x0, n_halves, want_lse, out_scale):
    refs = list(refs)
    q_ref = refs.pop(0)
    sel_ref = refs.pop(0) if mode == "nsa_sel" else None
    k_ref = refs.pop(0)
    v_ref = refs.pop(0)
    bias_ref = refs.pop(0) if has_bias else None
    lam_ref = gout_ref = None
    if mode == "diff":
        lam_ref = refs.pop(0)
        gout_ref = refs.pop(0)
    o_ref = refs.pop(0)
    lse_ref = refs.pop(0) if want_lse else None
    m_sc, l_sc, acc_sc = refs

    qi = pl.program_id(1)
    q0 = qi * tq
    n_tiles = k_ref.shape[0] // tk

    if mode == "diff":
        q = q_ref[...]
        lane = lax.broadcasted_iota(jnp.int32, q.shape, 1)
        zero = jnp.zeros_like(q)
        lhs = [jnp.concatenate([jnp.where(lane < DIFF_QK, q, zero), jnp.where(lane >= DIFF_QK, q, zero)], axis=0)]
    elif mode == "nsa_sel":
        lhs = []
        for hf in range(n_halves):
            sel = sel_ref[hf]
            lhs.append(jnp.concatenate(
                [jnp.concatenate([q_ref[:, g * dq:(g + 1) * dq], sel], axis=1) for g in range(G)], axis=0))
    else:
        lhs = [jnp.concatenate([q_ref[:, g * dq:(g + 1) * dq] for g in range(G)], axis=0) if G > 1 else q_ref[...]]

    m_sc[...] = jnp.full_like(m_sc, NEG)
    l_sc[...] = jnp.zeros_like(l_sc)
    acc_sc[...] = jnp.zeros_like(acc_sc)

    def logits(k0, w, Q, masked, biased):
        kt = k_ref[pl.ds(k0, w), :]
        s = lax.dot_general(Q, kt, (((1,), (1,)), ((), ())), preferred_element_type=F32)
        has_bias = biased
        if has_bias or masked:
            if masked:
                rel = (q0 - k0) + lax.broadcasted_iota(jnp.int32, (tq, w), 0) - lax.broadcasted_iota(jnp.int32, (tq, w), 1)
                vis = rel >= 0
                if span is not None:
                    vis = jnp.logical_and(vis, rel <= span)
            if has_bias:
                start = pl.multiple_of(jnp.maximum(x0 - (q0 - k0), 0), LANE)
            parts = []
            for g in range(G):
                sg = s[g * tq:(g + 1) * tq]
                if has_bias:
                    sg = sg + bias_ref[0 if mode == "diff" else g, :, pl.ds(start, w)]
                if masked:
                    sg = jnp.where(vis, sg, NEG)
                parts.append(sg)
            s = jnp.concatenate(parts, axis=0) if G > 1 else parts[0]
        return s

    def update(s, k0, w):
        vt = v_ref[pl.ds(k0, w), :]
        m_prev = m_sc[...]
        m_new = jnp.maximum(m_prev, jnp.max(s, axis=-1, keepdims=True))
        alpha = jnp.exp(m_prev - m_new)
        p = jnp.exp(s - jnp.tile(m_new, (1, w // LANE)))
        l_sc[...] = alpha * l_sc[...] + jnp.sum(p, axis=-1, keepdims=True)
        acc_sc[...] = alpha * acc_sc[...] + jnp.dot(p.astype(vt.dtype), vt, preferred_element_type=F32)
        m_sc[...] = m_new

    def tile(k0, w, Q, masked, biased):
        if w == tkb and tkb > tk:
            h = w // 2
            sa = logits(k0, h, Q, masked, biased)
            sb = logits(k0 + h, h, Q, masked, biased)
            update(sa, k0, h)
            update(sb, k0 + h, h)
        else:
            update(logits(k0, w, Q, masked, biased), k0, w)

    def run(lo, hi, w, Q, masked, biased=has_bias):
        def body(j, c):
            tile(pl.multiple_of(j * w, w), w, Q, masked, biased)
            return c
        lax.fori_loop(lo, hi, body, 0)

    def run_visible(lo, hi, Q):
        if tkb == tk:
            run(lo, hi, tk, Q, False)
            return
        r = tkb // tk
        up = jnp.minimum(-(-lo // r) * r, hi)
        dn = jnp.maximum((hi // r) * r, up)
        if not (isinstance(lo, int) and lo % r == 0):
            run(lo, up, tk, Q, False)
        if has_bias and far_zero:
            far = jnp.clip(jnp.maximum(q0 - (REL_MAX_DIST - 1), 0) // tkb, up // r, dn // r)
            run(up // r, far, tkb, Q, False, False)
            run(far, dn // r, tkb, Q, False)
        else:
            run(up // r, dn // r, tkb, Q, False)
        run(dn, hi, tk, Q, False)

    j_hi = jnp.minimum((q0 + tq - 1) // tk, n_tiles - 1)
    b = (q0 + 1) // tk
    if span is None:
        j_lo = 0
        a = 0
    else:
        j_lo = jnp.maximum(q0 - span, 0) // tk
        a = jnp.maximum(q0 + tq - 1 - span + tk - 1, 0) // tk
        a = jnp.minimum(jnp.maximum(a, j_lo), b)
    tph = (SEL_HALF_BLOCKS * NSA_SLC_BLOCK) // tk
    for hf in range(n_halves):
        Q = lhs[hf]
        if mode == "nsa_sel":
            clip = lambda lo, hi: (jnp.maximum(lo, hf * tph) if hf else lo, jnp.minimum(hi, (hf + 1) * tph))
        else:
            clip = lambda lo, hi: (lo, hi)
        if span is not None:
            run(*clip(j_lo, a), tk, Q, True)
        run_visible(*clip(a, b), Q)
        run(*clip(b, j_hi + 1), tk, Q, True)

    l = l_sc[...]
    acc = acc_sc[...] / l
    if mode == "diff":
        o = acc[:tq] - lam_ref[...] * acc[tq:]
        o = o * lax.rsqrt(jnp.mean(o * o, axis=-1, keepdims=True) + EPS) * (gout_ref[...] * out_scale)
        o_ref[...] = o.astype(o_ref.dtype)
    else:
        for g in range(G):
            o_ref[:, g * dv:(g + 1) * dv] = acc[g * tq:(g + 1) * tq].astype(o_ref.dtype)
        if want_lse:
            lse = m_sc[...] + jnp.log(l)
            for g in range(G):
                lse_ref[:, g * LANE:(g + 1) * LANE] = lse[g * tq:(g + 1) * tq]


def attention(q, k, v, *, n_streams, seq, out_cols, qcol, kcol, vcol, ocol, mode="plain", tq, tk, tkb=None, G=1,
              dq, dk, dv=LANE, span=None, bias=None, bcol=None, sel=None, lam=None, gout=None, out_scale=1.0,
              want_lse=False, bias_far_zero=False, name="attention"):
    tq = min(tq, seq)
    tk = min(tk, seq)
    tkb = tk if tkb is None else min(tkb, seq)
    assert seq % tq == 0 and seq % tkb == 0 and tkb % tk == 0 and tq % LANE == 0 and tk % LANE == 0
    resident = dict(pipeline_mode=pl.Buffered(1))
    R = 2 * tq if mode == "diff" else G * tq
    Gk = 2 if mode == "diff" else G
    wq = dq if mode == "diff" else G * dq
    in_specs = [pl.BlockSpec((tq, wq), lambda n, i: (i, qcol(n)))]
    args = [q]
    n_halves = 1
    if mode == "nsa_sel":
        n_halves = sel.shape[0]
        in_specs.append(pl.BlockSpec((n_halves, tq, LANE), lambda n, i: (0, i, n)))
        args.append(sel)
    in_specs += [pl.BlockSpec((seq, dk), lambda n, i: (0, kcol(n)), **resident),
                 pl.BlockSpec((seq, dv), lambda n, i: (0, vcol(n)), **resident)]
    args += [k, v]
    x0 = 0
    if bias is not None:
        x0 = _strip_x0(tkb)
        gb = 1 if mode == "diff" else G
        in_specs.append(pl.BlockSpec((gb, tq, x0 + max(tq, tk)), lambda n, i: (bcol(n), 0, 0), **resident))
        args.append(bias)
    if mode == "diff":
        in_specs += [pl.BlockSpec((1, LANE), lambda n, i: (0, 0)), pl.BlockSpec((1, LANE), lambda n, i: (0, 0))]
        args += [lam, gout]
    wo = dv if mode == "diff" else G * dv
    out_specs = [pl.BlockSpec((tq, wo), lambda n, i: (i, ocol(n)))]
    out_shape = [jax.ShapeDtypeStruct((seq, out_cols), BF16)]
    if want_lse:
        out_specs.append(pl.BlockSpec((tq, G * LANE), lambda n, i: (i, ocol(n))))
        out_shape.append(jax.ShapeDtypeStruct((seq, out_cols), F32))
    kern = functools.partial(_attn_kernel, mode=mode, tq=tq, tk=tk, tkb=tkb, G=Gk, dq=dq, dv=dv, span=span,
                             has_bias=bias is not None, far_zero=bias_far_zero, x0=x0, n_halves=n_halves,
                             want_lse=want_lse,
                             out_scale=out_scale)
    res = pl.pallas_call(
        kern,
        grid=(n_streams, seq // tq),
        in_specs=in_specs,
        out_specs=out_specs,
        out_shape=out_shape,
        scratch_shapes=[pltpu.VMEM((R, LANE), F32), pltpu.VMEM((R, LANE), F32), pltpu.VMEM((R, dv), F32)],
        compiler_params=_cp(("parallel", "parallel")),
        name=name,
    )(*args)
    return res if want_lse else res[0]


def _banded_kernel(q_ref, k_ref, v_ref, bias_ref, o_ref, lse_ref, m_sc, l_sc, acc_sc, *, heads, tq, span, x0):
    q0 = pl.program_id(1) * tq
    tk = tq
    m_sc[...] = jnp.full_like(m_sc, NEG)
    l_sc[...] = jnp.zeros_like(l_sc)
    acc_sc[...] = jnp.zeros_like(acc_sc)

    def body(j, c):
        k0 = pl.multiple_of(j * tk, tk)
        rel = (q0 - k0) + lax.broadcasted_iota(jnp.int32, (tq, tk), 0) - lax.broadcasted_iota(jnp.int32, (tq, tk), 1)
        vis = jnp.logical_and(rel >= 0, rel <= span)
        start = pl.multiple_of(jnp.maximum(x0 - (q0 - k0), 0), LANE)
        for h in range(heads):
            cs = slice(h * LANE, (h + 1) * LANE)
            s = lax.dot_general(q_ref[:, cs], k_ref[pl.ds(k0, tk), cs], (((1,), (1,)), ((), ())),
                                preferred_element_type=F32)
            s = jnp.where(vis, s + bias_ref[h, :, pl.ds(start, tk)], NEG)
            m_prev = m_sc[h]
            m_new = jnp.maximum(m_prev, jnp.max(s, axis=-1, keepdims=True))
            alpha = jnp.exp(m_prev - m_new)
            p = jnp.exp(s - jnp.tile(m_new, (1, tk // LANE)))
            l_sc[h] = alpha * l_sc[h] + jnp.sum(p, axis=-1, keepdims=True)
            vt = v_ref[pl.ds(k0, tk), cs]
            acc_sc[h] = alpha * acc_sc[h] + jnp.dot(p.astype(vt.dtype), vt, preferred_element_type=F32)
            m_sc[h] = m_new
        return c

    lax.fori_loop(jnp.maximum(q0 - span, 0) // tk, (q0 + tq - 1) // tk + 1, body, 0)
    for h in range(heads):
        cs = slice(h * LANE, (h + 1) * LANE)
        l = l_sc[h]
        o_ref[:, cs] = (acc_sc[h] / l).astype(o_ref.dtype)
        lse_ref[:, cs] = m_sc[h] + jnp.log(l)


def banded_attention(q, k, v, bias, *, streams, seq, tq, span, qcol, vcol, name):
    heads = bias.shape[0]
    width = heads * LANE
    tq = min(tq, seq)
    assert seq % tq == 0
    x0 = _strip_x0(tq)
    resident = dict(pipeline_mode=pl.Buffered(1))
    return pl.pallas_call(
        functools.partial(_banded_kernel, heads=heads, tq=tq, span=span, x0=x0),
        grid=(streams, seq // tq),
        in_specs=[pl.BlockSpec((tq, width), lambda r, i: (i, qcol(r))),
                  pl.BlockSpec((seq, width), lambda r, i: (0, qcol(r)), **resident),
                  pl.BlockSpec((seq, width), lambda r, i: (0, vcol(r)), **resident),
                  pl.BlockSpec((heads, tq, x0 + tq), lambda r, i: (0, 0, 0), **resident)],
        out_specs=[pl.BlockSpec((tq, width), lambda r, i: (i, r)),
                   pl.BlockSpec((tq, width), lambda r, i: (i, r))],
        out_shape=[jax.ShapeDtypeStruct((seq, streams * width), BF16),
                   jax.ShapeDtypeStruct((seq, streams * width), F32)],
        scratch_shapes=[pltpu.VMEM((heads, tq, LANE), F32), pltpu.VMEM((heads, tq, LANE), F32),
                        pltpu.VMEM((heads, tq, LANE), F32)],
        compiler_params=_cp(("parallel", "parallel")),
        name=name,
    )(q, k, v, bias)


def _dil_combine_kernel(o0, o1, o2, l0, l1, l2, out_ref):
    os_ = (o0, o1, o2)
    ls_ = (l0, l1, l2)
    for j in range(DIL_HP):
        sl = slice(j * LANE, (j + 1) * LANE)
        lse = [r[:, sl] for r in ls_]
        m = jnp.maximum(jnp.maximum(lse[0], lse[1]), lse[2])
        e = [jnp.exp(x - m) for x in lse]
        inv = 1.0 / (e[0] + e[1] + e[2])
        for g in range(len(DIL_PATTERNS)):
            c = (g * DIL_HP + j) * LANE
            out_ref[:, c:c + LANE] = (os_[g][:, sl].astype(F32) * (e[g] * inv)).astype(out_ref.dtype)


def dil_combine(outs, lses, tm=512):
    s = outs[0].shape[0]
    tm = min(tm, s)
    w = DIL_HP * LANE
    spec = pl.BlockSpec((tm, w), lambda i: (i, 0))
    return pl.pallas_call(
        _dil_combine_kernel,
        grid=(s // tm,),
        in_specs=[spec] * 6,
        out_specs=pl.BlockSpec((tm, DIL_HEADS * LANE), lambda i: (i, 0)),
        out_shape=jax.ShapeDtypeStruct((s, DIL_HEADS * LANE), BF16),
        compiler_params=_cp(("parallel",)),
        name="dil_combine",
    )(*outs, *lses)


def _gelu_tanh(y):
    return 0.5 * y * (1.0 + jnp.tanh(0.7978845608028654 * (y + 0.044715 * y * y * y)))


def _compress_kernel(a_ref, pe_ref, w1_ref, w2_ref, g_ref, o_ref, *, norm, transpose):
    a = a_ref[...].astype(F32)
    nc = a.shape[0]
    lo = jnp.dot((a + pe_ref[0]).astype(BF16), w1_ref[0], preferred_element_type=F32)
    hi = jnp.dot((a + pe_ref[1]).astype(BF16), w1_ref[1], preferred_element_type=F32)
    y = lo + pltpu.roll(hi, nc - 1, 0)
    z = jnp.dot(_gelu_tanh(y).astype(BF16), w2_ref[...], preferred_element_type=F32)
    if norm:
        z = _group_norm(z, LANE) * g_ref[...]
    o_ref[...] = (z.T if transpose else z).astype(o_ref.dtype)


def nsa_compress(chunks, pe, w1, w2, gain, *, norm, transpose):
    kvh, nc, w = chunks.shape
    half = NSA_CMP_STRIDE * LANE
    pe2 = pe.astype(F32).reshape(2, 1, half)
    w1s = w1.reshape(2, half, LANE).astype(BF16)
    oshape = (kvh, LANE, nc) if transpose else (kvh, nc, LANE)
    oblock = (None, LANE, nc) if transpose else (None, nc, LANE)
    return pl.pallas_call(
        functools.partial(_compress_kernel, norm=norm, transpose=transpose),
        grid=(kvh,),
        in_specs=[pl.BlockSpec((None, nc, w), lambda h: (h, 0, 0)),
                  pl.BlockSpec((2, 1, half), lambda h: (0, 0, 0)),
                  pl.BlockSpec((2, half, LANE), lambda h: (0, 0, 0)),
                  pl.BlockSpec((LANE, LANE), lambda h: (0, 0)),
                  pl.BlockSpec((1, LANE), lambda h: (0, 0))],
        out_specs=pl.BlockSpec(oblock, lambda h: (h, 0, 0)),
        out_shape=jax.ShapeDtypeStruct(oshape, BF16),
        compiler_params=_cp(("parallel",)),
        name="nsa_compress",
    )(chunks, pe2, w1s, w2.astype(BF16), gain.reshape(1, LANE).astype(F32))


def _cmp_topk_kernel(q_ref, kc_ref, vt_ref, ov_ref, strip_ref, o_ref, sel_ref, *, tq, y0, k_sel, n_halves):
    qi = pl.program_id(1)
    q0 = qi * tq
    G = NSA_GROUP
    nc = kc_ref.shape[0]
    qs = jnp.concatenate([q_ref[:, g * LANE:(g + 1) * LANE] for g in range(G)], axis=0)
    st = lax.dot_general(kc_ref[...], qs, (((1,), (1,)), ((), ())), preferred_element_type=F32)
    n_io = lax.broadcasted_iota(jnp.int32, (nc, tq), 0)
    i_io = lax.broadcasted_iota(jnp.int32, (nc, tq), 1)
    vis = (q0 + i_io - NSA_CMP_STRIDE * n_io - (NSA_CMP_BLOCK - 1)) >= 0
    start = pl.multiple_of(y0 - qi * (tq // NSA_CMP_STRIDE), 8)
    psum = jnp.zeros((nc, tq), F32)
    pts = []
    for g in range(G):
        s = st[:, g * tq:(g + 1) * tq] + strip_ref[g, pl.ds(start, nc), :]
        s = jnp.where(vis, s, NEG)
        m = jnp.max(s, axis=0, keepdims=True)
        e = jnp.where(vis, jnp.exp(s - m), 0.0)
        ssum = jnp.sum(e, axis=0, keepdims=True)
        p = e / jnp.where(ssum > 0, ssum, 1.0)
        psum = psum + p
        pts.append(p.astype(BF16))
    pt = jnp.concatenate(pts, axis=1)
    ot = jnp.dot(vt_ref[...], pt, preferred_element_type=F32)
    for g in range(G):
        o_ref[:, g * LANE:(g + 1) * LANE] = ot[:, g * tq:(g + 1) * tq].T.astype(o_ref.dtype)
    p_hi = psum.astype(BF16)
    p_lo = (psum - p_hi.astype(F32)).astype(BF16)
    imp = (jnp.dot(ov_ref[...], p_hi, preferred_element_type=F32)
           + jnp.dot(ov_ref[...], p_lo, preferred_element_type=F32))
    nb = imp.shape[0]
    b_io = lax.broadcasted_iota(jnp.int32, (nb, tq), 0)
    qblk = (q0 + lax.broadcasted_iota(jnp.int32, (nb, tq), 1)) // NSA_SLC_BLOCK
    forced = (b_io == 0) | (b_io == qblk) | (b_io == qblk - 1)
    val = jnp.where(forced, -1.0, jnp.where(b_io <= qblk, imp, -1.0))
    seln = jnp.where(forced, 0.0, SEL_NEG)
    for _ in range(k_sel - 3):
        mx = jnp.max(val, axis=0, keepdims=True)
        idx = jnp.min(jnp.where(val == mx, b_io, nb), axis=0, keepdims=True)
        pick = (b_io == idx) & (mx >= 0.0)
        seln = jnp.where(pick, 0.0, seln)
        val = jnp.where(pick, -2.0, val)
    sel = seln.T
    for hf in range(n_halves):
        sel_ref[hf] = sel[:, hf * LANE:(hf + 1) * LANE].astype(sel_ref.dtype)


def _cmp_y0(s):
    return (LANE // NSA_CMP_STRIDE) * (s // LANE - 1)


def cmp_bias_strips(table, s):
    nc = s // NSA_CMP_STRIDE
    y0 = _cmp_y0(s)
    far_rows = -(-(REL_MAX_DIST + NSA_CMP_BLOCK) // NSA_CMP_STRIDE)
    ya = max(y0 - far_rows, 0)
    yb = min(y0 + LANE // NSA_CMP_STRIDE, y0 + nc)
    y = np.arange(ya, yb)[:, None]
    i = np.arange(LANE)[None, :]
    band = jnp.transpose(table.astype(F32)[_rel_bucket_np(NSA_CMP_STRIDE * (y0 - y) + i - (NSA_CMP_BLOCK - 1))], (2, 0, 1))
    h = table.shape[1]
    far = jnp.broadcast_to(table.astype(F32)[REL_BUCKETS - 1][:, None, None], (h, ya, LANE))
    return jnp.concatenate([far, band, jnp.zeros((h, y0 + nc - yb, LANE), F32)], axis=1)


def nsa_cmp_topk(nqn, k_cmp, v_cmp_t, strips, s):
    tq = LANE
    nc = s // NSA_CMP_STRIDE
    n_cmp = nc - NSA_CMP_BLOCK // NSA_CMP_STRIDE + 1
    n_slc = s // NSA_SLC_BLOCK
    nbp = -(-n_slc // LANE) * LANE
    n_halves = nbp // LANE
    cs = np.arange(nc) * NSA_CMP_STRIDE
    ce = cs + NSA_CMP_BLOCK - 1
    ss = np.arange(nbp) * NSA_SLC_BLOCK
    ov = ((cs[None, :] < ss[:, None] + NSA_SLC_BLOCK) & (ce[None, :] >= ss[:, None])).astype(np.float32)
    ov[:, n_cmp:] = 0.0
    ov[n_slc:, :] = 0.0
    y0 = _cmp_y0(s)
    assert n_slc >= NSA_TOPK >= 3
    k_sel = NSA_TOPK
    return pl.pallas_call(
        functools.partial(_cmp_topk_kernel, tq=tq, y0=y0, k_sel=k_sel, n_halves=n_halves),
        grid=(NSA_KV_HEADS, s // tq),
        in_specs=[pl.BlockSpec((tq, NSA_GROUP * LANE), lambda h, i: (i, h)),
                  pl.BlockSpec((None, nc, LANE), lambda h, i: (h, 0, 0)),
                  pl.BlockSpec((None, LANE, nc), lambda h, i: (h, 0, 0)),
                  pl.BlockSpec((nbp, nc), lambda h, i: (0, 0)),
                  pl.BlockSpec((NSA_GROUP, y0 + nc, tq), lambda h, i: (h, 0, 0))],
        out_specs=[pl.BlockSpec((tq, NSA_GROUP * LANE), lambda h, i: (i, h)),
                   pl.BlockSpec((n_halves, tq, LANE), lambda h, i: (0, i, h))],
        out_shape=[jax.ShapeDtypeStruct((s, NSA_HEADS * LANE), BF16),
                   jax.ShapeDtypeStruct((n_halves, s, NSA_KV_HEADS * LANE), BF16)],
        compiler_params=_cp(("parallel", "parallel")),
        name="nsa_cmp_topk",
    )(nqn, k_cmp, v_cmp_t, jnp.asarray(ov, BF16), strips)


def _nsa_combine_kernel(oc_ref, os_ref, ow_ref, g_ref, o_ref):
    gate = jax.nn.sigmoid(g_ref[...].astype(F32))
    for h in range(NSA_HEADS):
        sl = slice(h * LANE, (h + 1) * LANE)
        o = (gate[:, 3 * h:3 * h + 1] * oc_ref[:, sl].astype(F32)
             + gate[:, 3 * h + 1:3 * h + 2] * os_ref[:, sl].astype(F32)
             + gate[:, 3 * h + 2:3 * h + 3] * ow_ref[:, sl].astype(F32))
        o_ref[:, sl] = o.astype(o_ref.dtype)


def nsa_combine(o_c, o_s, o_w, proj, tm=512):
    s = o_c.shape[0]
    tm = min(tm, s)
    w = NSA_HEADS * LANE
    spec = pl.BlockSpec((tm, w), lambda i: (i, 0))
    return pl.pallas_call(
        _nsa_combine_kernel,
        grid=(s // tm,),
        in_specs=[spec, spec, spec, pl.BlockSpec((tm, LANE), lambda i: (i, C_NG // LANE))],
        out_specs=spec,
        out_shape=jax.ShapeDtypeStruct((s, w), BF16),
        compiler_params=_cp(("parallel",)),
        name="nsa_combine",
    )(o_c, o_s, o_w, proj)


def _xattn_kernel(q_ref, kv_ref, gq_ref, gk_ref, o_ref, *, scale):
    nk = XATTN_HEADS * LANE
    for h in range(XATTN_HEADS):
        sl = slice(h * LANE, (h + 1) * LANE)
        qn = (_group_norm(q_ref[:, sl].astype(F32), LANE) * (gq_ref[...] * scale)).astype(BF16)
        kn = (_group_norm(kv_ref[:, sl].astype(F32), LANE) * gk_ref[...]).astype(BF16)
        s = lax.dot_general(qn, kn, (((1,), (1,)), ((), ())), preferred_element_type=F32)
        m = jnp.max(s, axis=-1, keepdims=True)
        e = jnp.exp(s - m)
        p = e / jnp.sum(e, axis=-1, keepdims=True)
        v = kv_ref[:, nk + h * LANE:nk + (h + 1) * LANE]
        o_ref[:, sl] = jnp.dot(p.astype(v.dtype), v, preferred_element_type=F32).astype(o_ref.dtype)


def xattn_core(qx, kv, qk_norm, tq=512):
    s, w = qx.shape
    tq = min(tq, s)
    m = kv.shape[0]
    return pl.pallas_call(
        functools.partial(_xattn_kernel, scale=HEAD_DIM ** -0.5),
        grid=(s // tq,),
        in_specs=[pl.BlockSpec((tq, w), lambda i: (i, 0)),
                  pl.BlockSpec((m, 2 * w), lambda i: (0, 0)),
                  pl.BlockSpec((1, LANE), lambda i: (0, 0)),
                  pl.BlockSpec((1, LANE), lambda i: (0, 0))],
        out_specs=pl.BlockSpec((tq, w), lambda i: (i, 0)),
        out_shape=jax.ShapeDtypeStruct((s, w), BF16),
        compiler_params=_cp(("parallel",)),
        name="xattn_core",
    )(qx, kv, qk_norm[0].reshape(1, LANE).astype(F32), qk_norm[1].reshape(1, LANE).astype(F32))


def _pack_w_in(w):
    d = w.shape[0]
    z = lambda n: jnp.zeros((d, n), w.dtype)
    parts = [w[:, 0:896], w[:, 1408:1472], z(64), w[:, 896:1408], w[:, 1472:7488],
             w[:, 7488:7512], z(LANE - 3 * NSA_HEADS), w[:, 7512:10200], z(NP_COLS - 10368)]
    return jnp.concatenate(parts, axis=1).astype(BF16)


def _mla_mixer(proj, rope, l, p):
    s = proj.shape[0]
    qm = mla_q_proj(proj, rope, p["mla_q_norm"][l], p["mla_w_uq"][l], p["mla_nope_norm"][l, 0], p["mla_rope_norm"][l, 0])
    km, vm = mla_kv_proj(proj, rope, p["mla_kv_norm"][l], p["mla_w_ukv"][l], p["mla_nope_norm"][l, 1],
                         p["mla_rope_norm"][l, 1])
    ident = lambda n: n
    return attention(qm, km, vm, n_streams=MLA_HEADS, seq=s, out_cols=MLA_HEADS * MLA_V, qcol=ident, kcol=ident,
                     vcol=ident, ocol=ident, tq=ATT_ROWS, tk=ATT_TK, tkb=ATT_TKB, dq=2 * LANE, dk=2 * LANE,
                     name="mla_attention")


def bias_tables(rel_bias, s):
    tabs = {"dil": []}
    for g, (_, dil) in enumerate(DIL_PATTERNS):
        tq = min(DIL_TQ, s // dil)
        tabs["dil"].append(bias_strips(rel_bias[:, BIAS_DIL + g * DIL_HP:BIAS_DIL + (g + 1) * DIL_HP], tq, tq, tq, dil))
    far_zero = lambda t: t - t[REL_BUCKETS - 1:REL_BUCKETS]
    nsa = rel_bias[:, BIAS_NSA:BIAS_NSA + NSA_HEADS].astype(F32)
    tk, tkb = min(ATT_TK, s), min(ATT_TKB, s)
    tabs["nsa"] = bias_strips(far_zero(nsa), ATT_ROWS // NSA_GROUP, tk, tkb, 1)
    tabs["nsa_win"] = bias_strips(far_zero(nsa), ATT_ROWS // NSA_GROUP, min(WIN_TK, s), min(WIN_TK, s), 1)
    tabs["nsa_cmp"] = cmp_bias_strips(nsa, s)
    diff = rel_bias[:, BIAS_DIFF:BIAS_DIFF + DIFF_HEADS].astype(F32)
    tabs["diff"] = bias_strips(far_zero(diff), min(ATT_ROWS // 2, s), tk, tkb, 1)
    return tabs


def _dilated_mixer(proj, l, p, tabs):
    s = proj.shape[0]
    w = DIL_HP * LANE
    dqn = head_rmsnorm(proj, C_DQ, DIL_HEADS, p["dil_qk_norm"][l, 0], LANE, HEAD_DIM ** -0.5)
    dkn = head_rmsnorm(proj, C_DK, DIL_HEADS, p["dil_qk_norm"][l, 1], LANE, 1.0)
    outs, lses = [], []
    ident = lambda n: n
    for g, (window, dil) in enumerate(DIL_PATTERNS):
        n_sub = s // dil
        tq = min(DIL_TQ, n_sub)
        if dil == 1:
            qkv = (dqn, dkn, proj)
            qcol = lambda r, g=g: g
            vcol = lambda r, g=g: C_DV // w + g
        else:
            fold = lambda t, c0: t[:, c0 + g * w:c0 + (g + 1) * w].reshape(n_sub, dil * w)
            qkv = (fold(dqn, 0), fold(dkn, 0), fold(proj, C_DV))
            qcol = vcol = ident
        o, lse = banded_attention(*qkv, tabs["dil"][g], streams=dil, seq=n_sub, tq=tq, span=window // dil,
                                  qcol=qcol, vcol=vcol, name=f"dil_attention_{g}")
        outs.append(o.reshape(s, w))
        lses.append(lse.reshape(s, w))
    return dil_combine(outs, lses)


def _nsa_mixer(proj, l, p, tabs):
    s = proj.shape[0]
    gains = p["nsa_qk_norm"][l]
    nqn = head_rmsnorm(proj, C_NQ, NSA_HEADS, gains[0], LANE, HEAD_DIM ** -0.5)
    kwn = head_rmsnorm(proj, C_NKW, NSA_KV_HEADS, gains[3], LANE, 1.0)
    ksel = nsa_sel_keys(proj, gains[2])
    nc = s // NSA_CMP_STRIDE

    def chunks(c0):
        t = proj[:, c0:c0 + NSA_KV_HEADS * LANE].reshape(nc, NSA_CMP_STRIDE, NSA_KV_HEADS, LANE)
        return jnp.transpose(t, (2, 0, 1, 3)).reshape(NSA_KV_HEADS, nc, NSA_CMP_STRIDE * LANE)

    k_cmp = nsa_compress(chunks(C_NKC), p["nsa_cmp_pe"][l, 0], p["nsa_cmp_w1"][l, 0], p["nsa_cmp_w2"][l, 0],
                         gains[1], norm=True, transpose=False)
    v_cmp_t = nsa_compress(chunks(C_NVC), p["nsa_cmp_pe"][l, 1], p["nsa_cmp_w1"][l, 1], p["nsa_cmp_w2"][l, 1],
                           gains[1], norm=False, transpose=True)
    o_c, sel = nsa_cmp_topk(nqn, k_cmp, v_cmp_t, tabs["nsa_cmp"], s)
    ident = lambda n: n
    common = dict(n_streams=NSA_KV_HEADS, seq=s, out_cols=NSA_HEADS * LANE, qcol=ident, ocol=ident,
                  tq=ATT_ROWS // NSA_GROUP, tk=ATT_TK, tkb=ATT_TKB, G=NSA_GROUP, dq=LANE, bias=tabs["nsa"], bcol=ident,
                  bias_far_zero=True)
    o_s = attention(nqn, ksel, proj, kcol=ident, vcol=lambda n: C_NVS // LANE + n, mode="nsa_sel", dk=2 * LANE,
                    sel=sel, name="nsa_sel_attention", **common)
    common.update(tk=WIN_TK, tkb=WIN_TK, bias=tabs["nsa_win"])
    o_w = attention(nqn, kwn, proj, kcol=ident, vcol=lambda n: C_NVW // LANE + n, dk=LANE, span=NSA_WINDOW - 1,
                    name="nsa_win_attention", **common)
    return nsa_combine(o_c, o_s, o_w, proj)


def _diff_mixer(proj, l, p, tabs):
    s = proj.shape[0]
    fqn = head_rmsnorm(proj, C_FQ, DIFF_HEADS, p["diff_qk_norm"][l, 0], DIFF_QK, DIFF_QK ** -0.5)
    fkn = head_rmsnorm(proj, C_FK, DIFF_HEADS, p["diff_qk_norm"][l, 1], DIFF_QK, 1.0)
    lam_init = 0.8 - 0.6 * math.exp(-0.3 * l)
    lv = p["diff_lambda"][l].astype(F32)
    lam = jnp.exp(jnp.sum(lv[0] * lv[1])) - jnp.exp(jnp.sum(lv[2] * lv[3])) + lam_init
    ident = lambda n: n
    return attention(fqn, fkn, proj, n_streams=DIFF_HEADS, seq=s, out_cols=DIFF_HEADS * DIFF_V, qcol=ident,
                     kcol=ident, vcol=lambda n: C_FV // LANE + n, ocol=ident, mode="diff", tq=ATT_ROWS // 2,
                     tk=ATT_TK, tkb=ATT_TKB, dq=LANE, dk=LANE, bias=tabs["diff"], bcol=ident, bias_far_zero=True,
                     lam=jnp.full((1, LANE), lam, F32),
                     gout=p["diff_out_norm"][l].reshape(1, LANE).astype(F32), out_scale=1.0 - lam_init,
                     name="diff_attention")


def kernel(x, mem, positions, rel_bias, norm_mix, w_in, mla_q_norm, mla_kv_norm, mla_w_uq, mla_w_ukv, mla_nope_norm, mla_rope_norm, dil_qk_norm, nsa_qk_norm, nsa_cmp_pe, nsa_cmp_w1, nsa_cmp_w2, diff_qk_norm, diff_lambda, diff_out_norm, w_out, norm_xattn, norm_mem, xattn_wq, xattn_wkv, xattn_qk_norm, xattn_wo, norm_ffn, ffn_w_gate, ffn_w_up, ffn_conv_w, ffn_conv_b, ffn_w_down):
    p = dict(mla_q_norm=mla_q_norm, mla_kv_norm=mla_kv_norm, mla_w_uq=mla_w_uq, mla_w_ukv=mla_w_ukv,
             mla_nope_norm=mla_nope_norm, mla_rope_norm=mla_rope_norm, dil_qk_norm=dil_qk_norm,
             nsa_qk_norm=nsa_qk_norm, nsa_cmp_pe=nsa_cmp_pe, nsa_cmp_w1=nsa_cmp_w1, nsa_cmp_w2=nsa_cmp_w2,
             diff_qk_norm=diff_qk_norm, diff_lambda=diff_lambda, diff_out_norm=diff_out_norm)
    b, s, d = x.shape
    assert b == 1
    xs = x.reshape(s, d).astype(F32)
    mems = mem.reshape(mem.shape[1], d).astype(F32)
    pos = positions.reshape(s, 1).astype(jnp.int32)
    tabs = bias_tables(rel_bias, s)
    rope = rope_tables(pos)
    for l in range(DEPTH):
        h = rmsnorm_rows(xs, norm_mix[l])
        proj = matmul(h, _pack_w_in(w_in[l]), tm=512, tn=1536, out_dtype=BF16, name="in_proj")
        mix = [_mla_mixer(proj, rope, l, p),_dilated_mixer(proj, l, p, tabs), _nsa_mixer(proj, l, p, tabs),
               _diff_mixer(proj, l, p, tabs)]
        xs = matmul_cat(mix, w_out[l].astype(BF16), xs, tm=512, tn=1024, name="out_proj")
        hx = rmsnorm_rows(xs, norm_xattn[l])
        qx = matmul(hx, xattn_wq[l].astype(BF16), tm=1024, tn=512, out_dtype=BF16, name="xattn_q")
        kvm = matmul(rmsnorm_rows(mems, norm_mem[l]), xattn_wkv[l].astype(BF16), tm=256, tn=1024, out_dtype=BF16,
                     name="xattn_kv")
        ox = xattn_core(qx, kvm, xattn_qk_norm[l])
        xs = matmul(ox, xattn_wo[l].astype(BF16), tm=1024, tn=1024, out_dtype=F32, residual=xs, name="xattn_o")
        hf = rmsnorm_rows(xs, norm_ffn[l])
        act = ffn_gate_up(hf, ffn_w_gate, ffn_w_up, l, ffn_conv_w[l], ffn_conv_b[l])
        xs = matmul(act, ffn_w_down[l].astype(BF16), tm=512, tn=512, out_dtype=F32, residual=xs, name="ffn_down")
    return xs.reshape(b, s, d)
```

```python
import functools
import math

import numpy as np
import jax
import jax.numpy as jnp
from jax import lax
from jax.experimental import pallas as pl
from jax.experimental.pallas import tpu as pltpu

F32 = jnp.float32
BF16 = jnp.bfloat16

D_MODEL = 4096
DEPTH = 2
HEAD_DIM = 128
EPS = 1e-6
MLA_HEADS = 8
MLA_Q_RANK = 896
MLA_KV_RANK = 512
MLA_NOPE = 128
MLA_ROPE = 64
MLA_V = 128
ROPE_THETA = 10000.0
DIL_PATTERNS = ((128, 1), (512, 4), (2048, 16))
DIL_HP = 3
DIL_HEADS = DIL_HP * len(DIL_PATTERNS)
NSA_HEADS = 8
NSA_KV_HEADS = 2
NSA_GROUP = NSA_HEADS // NSA_KV_HEADS
NSA_CMP_BLOCK = 32
NSA_CMP_STRIDE = 16
NSA_SLC_BLOCK = 64
NSA_TOPK = 16
NSA_WINDOW = 512
DIFF_HEADS = 7
DIFF_QK = 64
DIFF_V = 128
XATTN_HEADS = 4
FFN_DIM = 11008
REL_BUCKETS = 32
REL_MAX_DIST = 2048
BIAS_DIL = 0
BIAS_NSA = DIL_HEADS
BIAS_DIFF = DIL_HEADS + NSA_HEADS

LANE = 128
VMEM_LIMIT = 56 * 1024 * 1024
NEG = -1e30
LOG2E = math.log2(math.e)
SEL_NEG = -float(2 ** 30)
SEL_HALF_BLOCKS = 128
ATT_ROWS = 1024
ATT_TK = 1024
ATT_TKB = 2048
DIL_TQ = 512
WIN_TK = 512

C_CQ = 0
C_KR = 896
C_CKV = 1024
C_DQ = 1536
C_DK = 2688
C_DV = 3840
C_NQ = 4992
C_NKC = 6016
C_NVC = 6272
C_NKS = 6528
C_NVS = 6784
C_NKW = 7040
C_NVW = 7296
C_NG = 7552
C_FQ = 7680
C_FK = 8576
C_FV = 9472
NP_COLS = 10752


def _cp(sem):
    return pltpu.CompilerParams(dimension_semantics=sem, vmem_limit_bytes=VMEM_LIMIT)


def _rmsnorm_rows_kernel(x_ref, g_ref, o_ref):
    x = x_ref[...].astype(F32)
    ms = jnp.mean(x * x, axis=-1, keepdims=True)
    o_ref[...] = (x * lax.rsqrt(ms + EPS) * g_ref[...]).astype(o_ref.dtype)


def rmsnorm_rows(x, g, tm=256):
    m, d = x.shape
    tm = min(tm, m)
    return pl.pallas_call(
        _rmsnorm_rows_kernel,
        grid=(m // tm,),
        in_specs=[pl.BlockSpec((tm, d), lambda i: (i, 0)),
                  pl.BlockSpec((1, d), lambda i: (0, 0))],
        out_specs=pl.BlockSpec((tm, d), lambda i: (i, 0)),
        out_shape=jax.ShapeDtypeStruct((m, d), BF16),
        compiler_params=_cp(("parallel",)),
        name="rmsnorm_rows",
    )(x, g.reshape(1, d).astype(F32))


def _matmul_kernel(*refs, has_res):
    if has_res:
        a_ref, w_ref, r_ref, o_ref = refs
    else:
        a_ref, w_ref, o_ref = refs
    acc = jnp.dot(a_ref[...], w_ref[...], preferred_element_type=F32)
    if has_res:
        acc = acc + r_ref[...]
    o_ref[...] = acc.astype(o_ref.dtype)


def _matmul_cat_kernel(*refs, offs):
    a_refs = refs[:len(offs)]
    w_ref, r_ref, o_ref = refs[len(offs):]
    acc = r_ref[...]
    for a_ref, (k0, k1) in zip(a_refs, offs):
        acc = acc + jnp.dot(a_ref[...], w_ref[k0:k1, :], preferred_element_type=F32)
    o_ref[...] = acc.astype(o_ref.dtype)


def matmul_cat(a_list, w, residual, *, tm, tn, name="matmul_cat"):
    m = a_list[0].shape[0]
    k, n = w.shape
    offs, k0 = [], 0
    for a in a_list:
        offs.append((k0, k0 + a.shape[1]))
        k0 += a.shape[1]
    tm = min(tm, m)
    assert k0 == k and m % tm == 0 and n % tn == 0
    in_specs = [pl.BlockSpec((tm, a.shape[1]), lambda j, i: (i, 0)) for a in a_list]
    in_specs += [pl.BlockSpec((k, tn), lambda j, i: (0, j)), pl.BlockSpec((tm, tn), lambda j, i: (i, j))]
    return pl.pallas_call(
        functools.partial(_matmul_cat_kernel, offs=tuple(offs)),
        grid=(n // tn, m // tm),
        in_specs=in_specs,
        out_specs=pl.BlockSpec((tm, tn), lambda j, i: (i, j)),
        out_shape=jax.ShapeDtypeStruct((m, n), F32),
        compiler_params=_cp(("parallel", "parallel")),
        name=name,
    )(*a_list, w, residual)


def matmul(a, w, *, tm, tn, out_dtype, residual=None, name="matmul"):
    m, k = a.shape
    n = w.shape[1]
    tm = min(tm, m)
    tn = min(tn, n)
    assert m % tm == 0 and n % tn == 0
    in_specs = [pl.BlockSpec((tm, k), lambda j, i: (i, 0)),
                pl.BlockSpec((k, tn), lambda j, i: (0, j))]
    args = [a, w]
    if residual is not None:
        in_specs.append(pl.BlockSpec((tm, tn), lambda j, i: (i, j)))
        args.append(residual)
    return pl.pallas_call(
        functools.partial(_matmul_kernel, has_res=residual is not None),
        grid=(n // tn, m // tm),
        in_specs=in_specs,
        out_specs=pl.BlockSpec((tm, tn), lambda j, i: (i, j)),
        out_shape=jax.ShapeDtypeStruct((m, n), out_dtype),
        compiler_params=_cp(("parallel", "parallel")),
        name=name,
    )(*args)


def _ffn_gate_up_kernel(h_ref, wg_ref, wu_ref, cw_ref, cb_ref, o_ref, carry_ref, wgb_ref, wub_ref):
    i = pl.program_id(1)

    @pl.when(i == 0)
    def _():
        carry_ref[...] = jnp.zeros_like(carry_ref)
        wgb_ref[...] = wg_ref[...].astype(BF16)
        wub_ref[...] = wu_ref[...].astype(BF16)

    h = h_ref[...]
    u = jnp.dot(h, wgb_ref[...], preferred_element_type=F32)
    up = jnp.dot(h, wub_ref[...], preferred_element_type=F32)
    tm = u.shape[0]
    prev = carry_ref[...]
    row = lax.broadcasted_iota(jnp.int32, u.shape, 0)
    u1 = jnp.where(row == 0, prev[7:8, :], pltpu.roll(u, 1, 0))
    u2 = pltpu.roll(u, 2, 0)
    u2 = jnp.where(row == 0, prev[6:7, :], jnp.where(row == 1, prev[7:8, :], u2))
    cw = cw_ref[...]
    g = cw[0:1, :] * u2 + cw[1:2, :] * u1 + cw[2:3, :] * u + cb_ref[...]
    o_ref[...] = (g * jax.nn.sigmoid(g) * up).astype(o_ref.dtype)
    carry_ref[...] = u[tm - 8:, :]


def ffn_gate_up(h, wg, wu, layer, conv_w, conv_b, tm=1024, tf=256):
    m, d = h.shape
    f = wg.shape[2]
    tm = min(tm, m)
    assert f % tf == 0 and m % tm == 0
    return pl.pallas_call(
        _ffn_gate_up_kernel,
        grid=(f // tf, m // tm),
        in_specs=[pl.BlockSpec((tm, d), lambda j, i: (i, 0)),
                  pl.BlockSpec((None, d, tf), lambda j, i: (layer, 0, j)),
                  pl.BlockSpec((None, d, tf), lambda j, i: (layer, 0, j)),
                  pl.BlockSpec((3, tf), lambda j, i: (0, j)),
                  pl.BlockSpec((1, tf), lambda j, i: (0, j))],
        out_specs=pl.BlockSpec((tm, tf), lambda j, i: (i, j)),
        out_shape=jax.ShapeDtypeStruct((m, f), BF16),
        scratch_shapes=[pltpu.VMEM((8, tf), F32), pltpu.VMEM((d, tf), BF16), pltpu.VMEM((d, tf), BF16)],
        compiler_params=_cp(("arbitrary", "arbitrary")),
        name="ffn_gate_up",
    )(h, wg, wu, conv_w.astype(F32), conv_b.reshape(1, f).astype(F32))


def _group_norm(x, d):
    sq = x * x
    tot = jnp.sum(sq, axis=-1, keepdims=True)
    if d == LANE:
        ms = tot / d
    else:
        lane = lax.broadcasted_iota(jnp.int32, x.shape, 1)
        lo = jnp.sum(jnp.where(lane < d, sq, 0.0), axis=-1, keepdims=True)
        ms = jnp.where(lane < d, lo, tot - lo) / d
    return x * lax.rsqrt(ms + EPS)


def _head_norm_kernel(x_ref, g_ref, o_ref, *, d, scale):
    x = x_ref[...].astype(F32)
    o_ref[...] = (_group_norm(x, d) * (g_ref[...] * scale)).astype(o_ref.dtype)


def head_rmsnorm(src, col0, ncb, gain, d, scale, tm=4096):
    s = src.shape[0]
    tm = min(tm, s)
    cb0 = col0 // LANE
    g = jnp.tile(gain.astype(F32), LANE // d).reshape(1, LANE)
    return pl.pallas_call(
        functools.partial(_head_norm_kernel, d=d, scale=scale),
        grid=(s // tm, ncb),
        in_specs=[pl.BlockSpec((tm, LANE), lambda i, c: (i, cb0 + c)),
                  pl.BlockSpec((1, LANE), lambda i, c: (0, 0))],
        out_specs=pl.BlockSpec((tm, LANE), lambda i, c: (i, c)),
        out_shape=jax.ShapeDtypeStruct((s, ncb * LANE), BF16),
        compiler_params=_cp(("parallel", "parallel")),
        name="head_rmsnorm",
    )(src, g)


def _nsa_ksel_kernel(x_ref, g_ref, o_ref):
    tm = x_ref.shape[0]
    x = x_ref[...].astype(F32)
    o_ref[:, :LANE] = (_group_norm(x, LANE) * g_ref[...]).astype(o_ref.dtype)
    row = pl.program_id(0) * tm + lax.broadcasted_iota(jnp.int32, (tm, LANE), 0)
    col = lax.broadcasted_iota(jnp.int32, (tm, LANE), 1)
    hot = ((row // NSA_SLC_BLOCK) % SEL_HALF_BLOCKS) == col
    o_ref[:, LANE:] = jnp.where(hot, 1.0, 0.0).astype(o_ref.dtype)


def nsa_sel_keys(proj, gain, tm=4096):
    s = proj.shape[0]
    tm = min(tm, s)
    cb0 = C_NKS // LANE
    return pl.pallas_call(
        _nsa_ksel_kernel,
        grid=(s // tm, NSA_KV_HEADS),
        in_specs=[pl.BlockSpec((tm, LANE), lambda i, c: (i, cb0 + c)),
                  pl.BlockSpec((1, LANE), lambda i, c: (0, 0))],
        out_specs=pl.BlockSpec((tm, 2 * LANE), lambda i, c: (i, c)),
        out_shape=jax.ShapeDtypeStruct((s, NSA_KV_HEADS * 2 * LANE), BF16),
        compiler_params=_cp(("parallel", "parallel")),
        name="nsa_sel_keys",
    )(proj, gain.reshape(1, LANE).astype(F32))


def _rope_tables_kernel(pos_ref, inv_ref, sign_ref, cos_ref, sin_ref):
    ang = pos_ref[...].astype(F32) * inv_ref[...]
    cos_ref[...] = jnp.cos(ang)
    sin_ref[...] = jnp.sin(ang) * sign_ref[...]


def rope_tables(pos, tm=2048):
    s = pos.shape[0]
    tm = min(tm, s)
    inv, sign = _rope_consts()
    const = lambda i: (0, 0)
    spec = pl.BlockSpec((tm, LANE), lambda i: (i, 0))
    return pl.pallas_call(
        _rope_tables_kernel,
        grid=(s // tm,),
        in_specs=[pl.BlockSpec((tm, 1), lambda i: (i, 0)), pl.BlockSpec((1, LANE), const), pl.BlockSpec((1, LANE), const)],
        out_specs=[spec, spec],
        out_shape=[jax.ShapeDtypeStruct((s, LANE), F32)] * 2,
        compiler_params=_cp(("parallel",)),
        name="rope_tables",
    )(pos, inv, sign)


def _rope_apply(x, cosv, sin_signed):
    lane = lax.broadcasted_iota(jnp.int32, x.shape, 1)
    half = MLA_ROPE // 2
    swapped = jnp.where(lane < half, pltpu.roll(x, LANE - half, 1), pltpu.roll(x, half, 1))
    return x * cosv + swapped * sin_signed


def _latent_norm(c_ref, g_ref):
    c = c_ref[...].astype(F32)
    ms = jnp.mean(c * c, axis=-1, keepdims=True)
    return (c * lax.rsqrt(ms + EPS) * g_ref[...]).astype(BF16)


def _mla_q_kernel(c_ref, cos_ref, sin_ref, gq_ref, w_ref, gn_ref, gr_ref, o_ref, *, scale):
    u = jnp.dot(_latent_norm(c_ref, gq_ref), w_ref[...], preferred_element_type=F32)
    cosv, sinv = cos_ref[...], sin_ref[...]
    for h in range(MLA_HEADS):
        b = 2 * LANE * h
        nope = _group_norm(u[:, b:b + LANE], LANE) * (gn_ref[...] * scale)
        rp = u[:, b + LANE:b + 2 * LANE]
        rp = rp * lax.rsqrt(jnp.sum(rp * rp, axis=-1, keepdims=True) / MLA_ROPE + EPS) * gr_ref[...]
        rp = _rope_apply(rp, cosv, sinv) * scale
        o_ref[:, b:b + LANE] = nope.astype(o_ref.dtype)
        o_ref[:, b + LANE:b + 2 * LANE] = rp.astype(o_ref.dtype)


def _mla_kv_kernel(c_ref, kr_ref, cos_ref, sin_ref, gkv_ref, w_ref, gn_ref, gr_ref, k_ref, v_ref):
    kv = jnp.dot(_latent_norm(c_ref, gkv_ref), w_ref[...], preferred_element_type=F32)
    cosv, sinv = cos_ref[...], sin_ref[...]
    kr = kr_ref[...].astype(F32)
    kr = kr * lax.rsqrt(jnp.sum(kr * kr, axis=-1, keepdims=True) / MLA_ROPE + EPS) * gr_ref[...]
    kr = _rope_apply(kr, cosv, sinv).astype(k_ref.dtype)
    nv = MLA_HEADS * LANE
    for h in range(MLA_HEADS):
        kn = _group_norm(kv[:, h * LANE:(h + 1) * LANE], LANE) * gn_ref[...]
        k_ref[:, 2 * LANE * h:2 * LANE * h + LANE] = kn.astype(k_ref.dtype)
        k_ref[:, 2 * LANE * h + LANE:2 * LANE * (h + 1)] = kr
    v_ref[...] = kv[:, nv:].astype(v_ref.dtype)


def _rope_consts():
    half = MLA_ROPE // 2
    inv = ROPE_THETA ** (-jnp.arange(half, dtype=F32) / half)
    inv = jnp.concatenate([inv, inv, jnp.zeros((LANE - MLA_ROPE,), F32)]).reshape(1, LANE)
    sign = np.zeros((1, LANE), np.float32)
    sign[0, :half] = -1.0
    sign[0, half:MLA_ROPE] = 1.0
    return inv, jnp.asarray(sign)


def _pad_lanes(v, n=LANE):
    return jnp.concatenate([v.astype(F32), jnp.zeros((n - v.shape[0],), F32)]).reshape(1, n)


def mla_q_proj(proj, rope, q_norm, w_uq, nope_gain, rope_gain, tm=512):
    s = proj.shape[0]
    tm = min(tm, s)
    w = jnp.pad(w_uq.reshape(MLA_Q_RANK, MLA_HEADS, MLA_NOPE + MLA_ROPE),
                ((0, 0), (0, 0), (0, LANE - MLA_ROPE))).reshape(MLA_Q_RANK, MLA_HEADS * 2 * LANE).astype(BF16)
    wq = MLA_HEADS * 2 * LANE
    const = lambda i: (0, 0)
    return pl.pallas_call(
        functools.partial(_mla_q_kernel, scale=LOG2E * (MLA_NOPE + MLA_ROPE) ** -0.5),
        grid=(s // tm,),
        in_specs=[pl.BlockSpec((tm, MLA_Q_RANK), lambda i: (i, C_CQ // MLA_Q_RANK)),
                  pl.BlockSpec((tm, LANE), lambda i: (i, 0)), pl.BlockSpec((tm, LANE), lambda i: (i, 0)),
                  pl.BlockSpec((1, MLA_Q_RANK), const),
                  pl.BlockSpec((MLA_Q_RANK, wq), const),
                  pl.BlockSpec((1, LANE), const), pl.BlockSpec((1, LANE), const)],
        out_specs=pl.BlockSpec((tm, wq), lambda i: (i, 0)),
        out_shape=jax.ShapeDtypeStruct((s, wq), BF16),
        compiler_params=_cp(("parallel",)),
        name="mla_q_proj",
    )(proj, rope[0], rope[1], q_norm.reshape(1, -1).astype(F32), w, nope_gain.reshape(1, LANE).astype(F32),
      _pad_lanes(rope_gain))


def mla_kv_proj(proj, rope, kv_norm, w_ukv, nope_gain, rope_gain, tm=512):
    s = proj.shape[0]
    tm = min(tm, s)
    w = jnp.transpose(w_ukv.reshape(MLA_KV_RANK, MLA_HEADS, 2, LANE), (0, 2, 1, 3))
    w = w.reshape(MLA_KV_RANK, 2 * MLA_HEADS * LANE).astype(BF16)
    wk = MLA_HEADS * 2 * LANE
    wv = MLA_HEADS * MLA_V
    const = lambda i: (0, 0)
    return pl.pallas_call(
        _mla_kv_kernel,
        grid=(s // tm,),
        in_specs=[pl.BlockSpec((tm, MLA_KV_RANK), lambda i: (i, C_CKV // MLA_KV_RANK)),
                  pl.BlockSpec((tm, LANE), lambda i: (i, C_KR // LANE)),
                  pl.BlockSpec((tm, LANE), lambda i: (i, 0)), pl.BlockSpec((tm, LANE), lambda i: (i, 0)),
                  pl.BlockSpec((1, MLA_KV_RANK), const),
                  pl.BlockSpec((MLA_KV_RANK, 2 * MLA_HEADS * LANE), const),
                  pl.BlockSpec((1, LANE), const), pl.BlockSpec((1, LANE), const)],
        out_specs=[pl.BlockSpec((tm, wk), lambda i: (i, 0)),
                   pl.BlockSpec((tm, wv), lambda i: (i, 0))],
        out_shape=[jax.ShapeDtypeStruct((s, wk), BF16), jax.ShapeDtypeStruct((s, wv), BF16)],
        compiler_params=_cp(("parallel",)),
        name="mla_kv_proj",
    )(proj, proj, rope[0], rope[1], kv_norm.reshape(1, -1).astype(F32), w, nope_gain.reshape(1, LANE).astype(F32),
      _pad_lanes(rope_gain))


def _rel_bucket_np(dist):
    n = np.maximum(dist, 0)
    exact = REL_BUCKETS // 2
    ratio = np.log(np.maximum(n, 1).astype(np.float32) / np.float32(exact)) / np.float32(math.log(REL_MAX_DIST / exact))
    large = np.minimum(exact + (ratio.astype(np.float32) * (REL_BUCKETS - exact)).astype(np.int32), REL_BUCKETS - 1)
    return np.where(n < exact, n, large).astype(np.int32)


def _strip_x0(tk):
    return -(-(REL_MAX_DIST - 1 + tk) // LANE) * LANE


def bias_strips(table, tq, tk, tkb, dist_scale):
    x0 = _strip_x0(tkb)
    width = x0 + max(tq, tk)
    period = -(-(width + tq) // LANE) * LANE
    t = np.arange(period)
    t = np.where(t < width, t, t - period)
    u = table.astype(F32)[_rel_bucket_np((x0 - t) * dist_scale)].T
    h = u.shape[0]
    return pl.pallas_call(
        functools.partial(_toeplitz_kernel, tq=tq, width=width),
        grid=(h,),
        in_specs=[pl.BlockSpec((None, 1, period), lambda n: (n, 0, 0))],
        out_specs=pl.BlockSpec((None, tq, width), lambda n: (n, 0, 0)),
        out_shape=jax.ShapeDtypeStruct((h, tq, width), F32),
        compiler_params=_cp(("parallel",)),
        name="bias_strips",
    )(u.reshape(h, 1, period))


def _toeplitz_kernel(u_ref, o_ref, *, tq, width):
    rows = jnp.broadcast_to(u_ref[...], (tq, u_ref.shape[1]))
    o_ref[...] = pltpu.roll(rows, 0, 1, stride=1, stride_axis=0)[:, :width]


def _attn_kernel(*refs, mode, tq, tk, tkb, G, dq, dv, span, has_bias, far_zero, x0, n_halves, want_lse, out_scale):
    refs = list(refs)
    q_ref = refs.pop(0)
    sel_ref = refs.pop(0) if mode == "nsa_sel" else None
    k_ref = refs.pop(0)
    v_ref = refs.pop(0)
    bias_ref = refs.pop(0) if has_bias else None
    lam_ref = gout_ref = None
    if mode == "diff":
        lam_ref = refs.pop(0)
        gout_ref = refs.pop(0)
    o_ref = refs.pop(0)
    lse_ref = refs.pop(0) if want_lse else None
    m_sc, l_sc, acc_sc = refs

    qi = pl.program_id(1)
    q0 = qi * tq
    n_tiles = k_ref.shape[0] // tk

    if mode == "diff":
        q = q_ref[...]
        lane = lax.broadcasted_iota(jnp.int32, q.shape, 1)
        zero = jnp.zeros_like(q)
        lhs = [jnp.concatenate([jnp.where(lane < DIFF_QK, q, zero), jnp.where(lane >= DIFF_QK, q, zero)], axis=0)]
    elif mode == "nsa_sel":
        lhs = []
        for hf in range(n_halves):
            sel = sel_ref[hf]
            lhs.append(jnp.concatenate(
                [jnp.concatenate([q_ref[:, g * dq:(g + 1) * dq], sel], axis=1) for g in range(G)], axis=0))
    else:
        lhs = [jnp.concatenate([q_ref[:, g * dq:(g + 1) * dq] for g in range(G)], axis=0) if G > 1 else q_ref[...]]

    m_sc[...] = jnp.full_like(m_sc, NEG)
    l_sc[...] = jnp.zeros_like(l_sc)
    acc_sc[...] = jnp.zeros_like(acc_sc)

    def logits(k0, w, Q, masked, biased):
        kt = k_ref[pl.ds(k0, w), :]
        s = lax.dot_general(Q, kt, (((1,), (1,)), ((), ())), preferred_element_type=F32)
        has_bias = biased
        if has_bias or masked:
            if masked:
                rel = (q0 - k0) + lax.broadcasted_iota(jnp.int32, (tq, w), 0) - lax.broadcasted_iota(jnp.int32, (tq, w), 1)
                vis = rel >= 0
                if span is not None:
                    vis = jnp.logical_and(vis, rel <= span)
            if has_bias:
                start = pl.multiple_of(jnp.maximum(x0 - (q0 - k0), 0), LANE)
            parts = []
            for g in range(G):
                sg = s[g * tq:(g + 1) * tq]
                if has_bias:
                    sg = sg + bias_ref[0 if mode == "diff" else g, :, pl.ds(start, w)]
                if masked:
                    sg = jnp.where(vis, sg, NEG)
                parts.append(sg)
            s = jnp.concatenate(parts, axis=0) if G > 1 else parts[0]
        return s

    def update(s, k0, w):
        vt = v_ref[pl.ds(k0, w), :]
        m_prev = m_sc[...]
        m_new = jnp.maximum(m_prev, jnp.max(s, axis=-1, keepdims=True))
        alpha = jnp.exp2(m_prev - m_new)
        p = jnp.exp2(s - jnp.tile(m_new, (1, w // LANE)))
        l_sc[...] = alpha * l_sc[...] + jnp.sum(p, axis=-1, keepdims=True)
        acc_sc[...] = alpha * acc_sc[...] + jnp.dot(p.astype(vt.dtype), vt, preferred_element_type=F32)
        m_sc[...] = m_new

    def tile(k0, w, Q, masked, biased):
        if w == tkb and tkb > tk:
            h = w // 2
            sa = logits(k0, h, Q, masked, biased)
            sb = logits(k0 + h, h, Q, masked, biased)
            update(sa, k0, h)
            update(sb, k0 + h, h)
        else:
            update(logits(k0, w, Q, masked, biased), k0, w)

    def run(lo, hi, w, Q, masked, biased=has_bias):
        def body(j, c):
            tile(pl.multiple_of(j * w, w), w, Q, masked, biased)
            return c
        lax.fori_loop(lo, hi, body, 0)

    def run_visible(lo, hi, Q):
        if tkb == tk:
            run(lo, hi, tk, Q, False)
            return
        r = tkb // tk
        up = jnp.minimum(-(-lo // r) * r, hi)
        dn = jnp.maximum((hi // r) * r, up)
        if not (isinstance(lo, int) and lo % r == 0):
            run(lo, up, tk, Q, False)
        if has_bias and far_zero:
            far = jnp.clip(jnp.maximum(q0 - (REL_MAX_DIST - 1), 0) // tkb, up // r, dn // r)
            run(up // r, far, tkb, Q, False, False)
            run(far, dn // r, tkb, Q, False)
        else:
            run(up // r, dn // r, tkb, Q, False)
        run(dn, hi, tk, Q, False)

    j_hi = jnp.minimum((q0 + tq - 1) // tk, n_tiles - 1)
    b = (q0 + 1) // tk
    if span is None:
        j_lo = 0
        a = 0
    else:
        j_lo = jnp.maximum(q0 - span, 0) // tk
        a = jnp.maximum(q0 + tq - 1 - span + tk - 1, 0) // tk
        a = jnp.minimum(jnp.maximum(a, j_lo), b)
    tph = (SEL_HALF_BLOCKS * NSA_SLC_BLOCK) // tk
    for hf in range(n_halves):
        Q = lhs[hf]
        if mode == "nsa_sel":
            clip = lambda lo, hi: (jnp.maximum(lo, hf * tph) if hf else lo, jnp.minimum(hi, (hf + 1) * tph))
        else:
            clip = lambda lo, hi: (lo, hi)
        if span is not None:
            run(*clip(j_lo, a), tk, Q, True)
        run_visible(*clip(a, b), Q)
        run(*clip(b, j_hi + 1), tk, Q, True)

    l = l_sc[...]
    acc = acc_sc[...] / l
    if mode == "diff":
        o = acc[:tq] - lam_ref[...] * acc[tq:]
        o = o * lax.rsqrt(jnp.mean(o * o, axis=-1, keepdims=True) + EPS) * (gout_ref[...] * out_scale)
        o_ref[...] = o.astype(o_ref.dtype)
    else:
        for g in range(G):
            o_ref[:, g * dv:(g + 1) * dv] = acc[g * tq:(g + 1) * tq].astype(o_ref.dtype)
        if want_lse:
            lse = (m_sc[...] + jnp.log2(l)) / LOG2E
            for g in range(G):
                lse_ref[:, g * LANE:(g + 1) * LANE] = lse[g * tq:(g + 1) * tq]


def attention(q, k, v, *, n_streams, seq, out_cols, qcol, kcol, vcol, ocol, mode="plain", tq, tk, tkb=None, G=1,
              dq, dk, dv=LANE, span=None, bias=None, bcol=None, sel=None, lam=None, gout=None, out_scale=1.0,
              want_lse=False, bias_far_zero=False, name="attention"):
    tq = min(tq, seq)
    tk = min(tk, seq)
    tkb = tk if tkb is None else min(tkb, seq)
    assert seq % tq == 0 and seq % tkb == 0 and tkb % tk == 0 and tq % LANE == 0 and tk % LANE == 0
    resident = dict(pipeline_mode=pl.Buffered(1))
    R = 2 * tq if mode == "diff" else G * tq
    Gk = 2 if mode == "diff" else G
    wq = dq if mode == "diff" else G * dq
    in_specs = [pl.BlockSpec((tq, wq), lambda n, i: (i, qcol(n)))]
    args = [q]
    n_halves = 1
    if mode == "nsa_sel":
        n_halves = sel.shape[0]
        in_specs.append(pl.BlockSpec((n_halves, tq, LANE), lambda n, i: (0, i, n)))
        args.append(sel)
    in_specs += [pl.BlockSpec((seq, dk), lambda n, i: (0, kcol(n)), **resident),
                 pl.BlockSpec((seq, dv), lambda n, i: (0, vcol(n)), **resident)]
    args += [k, v]
    x0 = 0
    if bias is not None:
        x0 = _strip_x0(tkb)
        gb = 1 if mode == "diff" else G
        in_specs.append(pl.BlockSpec((gb, tq, x0 + max(tq, tk)), lambda n, i: (bcol(n), 0, 0), **resident))
        args.append(bias)
    if mode == "diff":
        in_specs += [pl.BlockSpec((1, LANE), lambda n, i: (0, 0)), pl.BlockSpec((1, LANE), lambda n, i: (0, 0))]
        args += [lam, gout]
    wo = dv if mode == "diff" else G * dv
    out_specs = [pl.BlockSpec((tq, wo), lambda n, i: (i, ocol(n)))]
    out_shape = [jax.ShapeDtypeStruct((seq, out_cols), BF16)]
    if want_lse:
        out_specs.append(pl.BlockSpec((tq, G * LANE), lambda n, i: (i, ocol(n))))
        out_shape.append(jax.ShapeDtypeStruct((seq, out_cols), F32))
    kern = functools.partial(_attn_kernel, mode=mode, tq=tq, tk=tk, tkb=tkb, G=Gk, dq=dq, dv=dv, span=span,
                             has_bias=bias is not None, far_zero=bias_far_zero, x0=x0, n_halves=n_halves,
                             want_lse=want_lse,
                             out_scale=out_scale)
    res = pl.pallas_call(
        kern,
        grid=(n_streams, seq // tq),
        in_specs=in_specs,
        out_specs=out_specs,
        out_shape=out_shape,
        scratch_shapes=[pltpu.VMEM((R, LANE), F32), pltpu.VMEM((R, LANE), F32), pltpu.VMEM((R, dv), F32)],
        compiler_params=_cp(("parallel", "parallel")),
        name=name,
    )(*args)
    return res if want_lse else res[0]


def _banded_kernel(q_ref, k_ref, v_ref, bias_ref, o_ref, lse_ref, m_sc, l_sc, acc_sc, *, heads, tq, span, x0):
    q0 = pl.program_id(1) * tq
    tk = tq
    m_sc[...] = jnp.full_like(m_sc, NEG)
    l_sc[...] = jnp.zeros_like(l_sc)
    acc_sc[...] = jnp.zeros_like(acc_sc)

    def body(j, c):
        k0 = pl.multiple_of(j * tk, tk)
        rel = (q0 - k0) + lax.broadcasted_iota(jnp.int32, (tq, tk), 0) - lax.broadcasted_iota(jnp.int32, (tq, tk), 1)
        vis = jnp.logical_and(rel >= 0, rel <= span)
        start = pl.multiple_of(jnp.maximum(x0 - (q0 - k0), 0), LANE)
        for h in range(heads):
            cs = slice(h * LANE, (h + 1) * LANE)
            s = lax.dot_general(q_ref[:, cs], k_ref[pl.ds(k0, tk), cs], (((1,), (1,)), ((), ())),
                                preferred_element_type=F32)
            s = jnp.where(vis, s + bias_ref[h, :, pl.ds(start, tk)], NEG)
            m_prev = m_sc[h]
            m_new = jnp.maximum(m_prev, jnp.max(s, axis=-1, keepdims=True))
            alpha = jnp.exp(m_prev - m_new)
            p = jnp.exp(s - jnp.tile(m_new, (1, tk // LANE)))
            l_sc[h] = alpha * l_sc[h] + jnp.sum(p, axis=-1, keepdims=True)
            vt = v_ref[pl.ds(k0, tk), cs]
            acc_sc[h] = alpha * acc_sc[h] + jnp.dot(p.astype(vt.dtype), vt, preferred_element_type=F32)
            m_sc[h] = m_new
        return c

    lax.fori_loop(jnp.maximum(q0 - span, 0) // tk, (q0 + tq - 1) // tk + 1, body, 0)
    for h in range(heads):
        cs = slice(h * LANE, (h + 1) * LANE)
        l = l_sc[h]
        o_ref[:, cs] = (acc_sc[h] / l).astype(o_ref.dtype)
        lse_ref[:, cs] = m_sc[h] + jnp.log(l)


def banded_attention(q, k, v, bias, *, streams, seq, tq, span, qcol, vcol, name):
    heads = bias.shape[0]
    width = heads * LANE
    tq = min(tq, seq)
    assert seq % tq == 0
    x0 = _strip_x0(tq)
    resident = dict(pipeline_mode=pl.Buffered(1))
    return pl.pallas_call(
        functools.partial(_banded_kernel, heads=heads, tq=tq, span=span, x0=x0),
        grid=(streams, seq // tq),
        in_specs=[pl.BlockSpec((tq, width), lambda r, i: (i, qcol(r))),
                  pl.BlockSpec((seq, width), lambda r, i: (0, qcol(r)), **resident),
                  pl.BlockSpec((seq, width), lambda r, i: (0, vcol(r)), **resident),
                  pl.BlockSpec((heads, tq, x0 + tq), lambda r, i: (0, 0, 0), **resident)],
        out_specs=[pl.BlockSpec((tq, width), lambda r, i: (i, r)),
                   pl.BlockSpec((tq, width), lambda r, i: (i, r))],
        out_shape=[jax.ShapeDtypeStruct((seq, streams * width), BF16),
                   jax.ShapeDtypeStruct((seq, streams * width), F32)],
        scratch_shapes=[pltpu.VMEM((heads, tq, LANE), F32), pltpu.VMEM((heads, tq, LANE), F32),
                        pltpu.VMEM((heads, tq, LANE), F32)],
        compiler_params=_cp(("parallel", "parallel")),
        name=name,
    )(q, k, v, bias)


def _dil_combine_kernel(o0, o1, o2, l0, l1, l2, out_ref):
    os_ = (o0, o1, o2)
    ls_ = (l0, l1, l2)
    for j in range(DIL_HP):
        sl = slice(j * LANE, (j + 1) * LANE)
        lse = [r[:, sl] for r in ls_]
        m = jnp.maximum(jnp.maximum(lse[0], lse[1]), lse[2])
        e = [jnp.exp(x - m) for x in lse]
        inv = 1.0 / (e[0] + e[1] + e[2])
        for g in range(len(DIL_PATTERNS)):
            c = (g * DIL_HP + j) * LANE
            out_ref[:, c:c + LANE] = (os_[g][:, sl].astype(F32) * (e[g] * inv)).astype(out_ref.dtype)


def dil_combine(outs, lses, tm=512):
    s = outs[0].shape[0]
    tm = min(tm, s)
    w = DIL_HP * LANE
    spec = pl.BlockSpec((tm, w), lambda i: (i, 0))
    return pl.pallas_call(
        _dil_combine_kernel,
        grid=(s // tm,),
        in_specs=[spec] * 6,
        out_specs=pl.BlockSpec((tm, DIL_HEADS * LANE), lambda i: (i, 0)),
        out_shape=jax.ShapeDtypeStruct((s, DIL_HEADS * LANE), BF16),
        compiler_params=_cp(("parallel",)),
        name="dil_combine",
    )(*outs, *lses)


def _gelu_tanh(y):
    return 0.5 * y * (1.0 + jnp.tanh(0.7978845608028654 * (y + 0.044715 * y * y * y)))


def _compress_kernel(a_ref, pe_ref, w1_ref, w2_ref, g_ref, o_ref, *, norm, transpose):
    a = a_ref[...].astype(F32)
    nc = a.shape[0]
    lo = jnp.dot((a + pe_ref[0]).astype(BF16), w1_ref[0], preferred_element_type=F32)
    hi = jnp.dot((a + pe_ref[1]).astype(BF16), w1_ref[1], preferred_element_type=F32)
    y = lo + pltpu.roll(hi, nc - 1, 0)
    z = jnp.dot(_gelu_tanh(y).astype(BF16), w2_ref[...], preferred_element_type=F32)
    if norm:
        z = _group_norm(z, LANE) * g_ref[...]
    o_ref[...] = (z.T if transpose else z).astype(o_ref.dtype)


def nsa_compress(chunks, pe, w1, w2, gain, *, norm, transpose):
    kvh, nc, w = chunks.shape
    half = NSA_CMP_STRIDE * LANE
    pe2 = pe.astype(F32).reshape(2, 1, half)
    w1s = w1.reshape(2, half, LANE).astype(BF16)
    oshape = (kvh, LANE, nc) if transpose else (kvh, nc, LANE)
    oblock = (None, LANE, nc) if transpose else (None, nc, LANE)
    return pl.pallas_call(
        functools.partial(_compress_kernel, norm=norm, transpose=transpose),
        grid=(kvh,),
        in_specs=[pl.BlockSpec((None, nc, w), lambda h: (h, 0, 0)),
                  pl.BlockSpec((2, 1, half), lambda h: (0, 0, 0)),
                  pl.BlockSpec((2, half, LANE), lambda h: (0, 0, 0)),
                  pl.BlockSpec((LANE, LANE), lambda h: (0, 0)),
                  pl.BlockSpec((1, LANE), lambda h: (0, 0))],
        out_specs=pl.BlockSpec(oblock, lambda h: (h, 0, 0)),
        out_shape=jax.ShapeDtypeStruct(oshape, BF16),
        compiler_params=_cp(("parallel",)),
        name="nsa_compress",
    )(chunks, pe2, w1s, w2.astype(BF16), gain.reshape(1, LANE).astype(F32))


def _cmp_topk_kernel(q_ref, kc_ref, vt_ref, ov_ref, strip_ref, o_ref, sel_ref, *, tq, y0, k_sel, n_halves):
    qi = pl.program_id(1)
    q0 = qi * tq
    G = NSA_GROUP
    nc = kc_ref.shape[0]
    qs = jnp.concatenate([q_ref[:, g * LANE:(g + 1) * LANE] for g in range(G)], axis=0)
    st = lax.dot_general(kc_ref[...], qs, (((1,), (1,)), ((), ())), preferred_element_type=F32)
    n_io = lax.broadcasted_iota(jnp.int32, (nc, tq), 0)
    i_io = lax.broadcasted_iota(jnp.int32, (nc, tq), 1)
    vis = (q0 + i_io - NSA_CMP_STRIDE * n_io - (NSA_CMP_BLOCK - 1)) >= 0
    start = pl.multiple_of(y0 - qi * (tq // NSA_CMP_STRIDE), 8)
    psum = jnp.zeros((nc, tq), F32)
    pts = []
    for g in range(G):
        s = st[:, g * tq:(g + 1) * tq] + strip_ref[g, pl.ds(start, nc), :]
        s = jnp.where(vis, s, NEG)
        m = jnp.max(s, axis=0, keepdims=True)
        e = jnp.where(vis, jnp.exp2(s - m), 0.0)
        ssum = jnp.sum(e, axis=0, keepdims=True)
        p = e / jnp.where(ssum > 0, ssum, 1.0)
        psum = psum + p
        pts.append(p.astype(BF16))
    pt = jnp.concatenate(pts, axis=1)
    ot = jnp.dot(vt_ref[...], pt, preferred_element_type=F32)
    for g in range(G):
        o_ref[:, g * LANE:(g + 1) * LANE] = ot[:, g * tq:(g + 1) * tq].T.astype(o_ref.dtype)
    p_hi = psum.astype(BF16)
    p_lo = (psum - p_hi.astype(F32)).astype(BF16)
    imp = (jnp.dot(ov_ref[...], p_hi, preferred_element_type=F32)
           + jnp.dot(ov_ref[...], p_lo, preferred_element_type=F32))
    nb = imp.shape[0]
    b_io = lax.broadcasted_iota(jnp.int32, (nb, tq), 0)
    qblk = (q0 + lax.broadcasted_iota(jnp.int32, (nb, tq), 1)) // NSA_SLC_BLOCK
    forced = (b_io == 0) | (b_io == qblk) | (b_io == qblk - 1)
    val = jnp.where(forced, -1.0, jnp.where(b_io <= qblk, imp, -1.0))
    seln = jnp.where(forced, 0.0, SEL_NEG)
    for _ in range(k_sel - 3):
        mx = jnp.max(val, axis=0, keepdims=True)
        idx = jnp.min(jnp.where(val == mx, b_io, nb), axis=0, keepdims=True)
        pick = (b_io == idx) & (mx >= 0.0)
        seln = jnp.where(pick, 0.0, seln)
        val = jnp.where(pick, -2.0, val)
    sel = seln.T
    for hf in range(n_halves):
        sel_ref[hf] = sel[:, hf * LANE:(hf + 1) * LANE].astype(sel_ref.dtype)


def _cmp_y0(s):
    return (LANE // NSA_CMP_STRIDE) * (s // LANE - 1)


def cmp_bias_strips(table, s):
    nc = s // NSA_CMP_STRIDE
    y0 = _cmp_y0(s)
    far_rows = -(-(REL_MAX_DIST + NSA_CMP_BLOCK) // NSA_CMP_STRIDE)
    ya = max(y0 - far_rows, 0)
    yb = min(y0 + LANE // NSA_CMP_STRIDE, y0 + nc)
    y = np.arange(ya, yb)[:, None]
    i = np.arange(LANE)[None, :]
    band = jnp.transpose(table.astype(F32)[_rel_bucket_np(NSA_CMP_STRIDE * (y0 - y) + i - (NSA_CMP_BLOCK - 1))], (2, 0, 1))
    h = table.shape[1]
    far = jnp.broadcast_to(table.astype(F32)[REL_BUCKETS - 1][:, None, None], (h, ya, LANE))
    return jnp.concatenate([far, band, jnp.zeros((h, y0 + nc - yb, LANE), F32)], axis=1)


def nsa_cmp_topk(nqn, k_cmp, v_cmp_t, strips, s):
    tq = LANE
    nc = s // NSA_CMP_STRIDE
    n_cmp = nc - NSA_CMP_BLOCK // NSA_CMP_STRIDE + 1
    n_slc = s // NSA_SLC_BLOCK
    nbp = -(-n_slc // LANE) * LANE
    n_halves = nbp // LANE
    cs = np.arange(nc) * NSA_CMP_STRIDE
    ce = cs + NSA_CMP_BLOCK - 1
    ss = np.arange(nbp) * NSA_SLC_BLOCK
    ov = ((cs[None, :] < ss[:, None] + NSA_SLC_BLOCK) & (ce[None, :] >= ss[:, None])).astype(np.float32)
    ov[:, n_cmp:] = 0.0
    ov[n_slc:, :] = 0.0
    y0 = _cmp_y0(s)
    assert n_slc >= NSA_TOPK >= 3
    k_sel = NSA_TOPK
    return pl.pallas_call(
        functools.partial(_cmp_topk_kernel, tq=tq, y0=y0, k_sel=k_sel, n_halves=n_halves),
        grid=(NSA_KV_HEADS, s // tq),
        in_specs=[pl.BlockSpec((tq, NSA_GROUP * LANE), lambda h, i: (i, h)),
                  pl.BlockSpec((None, nc, LANE), lambda h, i: (h, 0, 0)),
                  pl.BlockSpec((None, LANE, nc), lambda h, i: (h, 0, 0)),
                  pl.BlockSpec((nbp, nc), lambda h, i: (0, 0)),
                  pl.BlockSpec((NSA_GROUP, y0 + nc, tq), lambda h, i: (h, 0, 0))],
        out_specs=[pl.BlockSpec((tq, NSA_GROUP * LANE), lambda h, i: (i, h)),
                   pl.BlockSpec((n_halves, tq, LANE), lambda h, i: (0, i, h))],
        out_shape=[jax.ShapeDtypeStruct((s, NSA_HEADS * LANE), BF16),
                   jax.ShapeDtypeStruct((n_halves, s, NSA_KV_HEADS * LANE), BF16)],
        compiler_params=_cp(("parallel", "parallel")),
        name="nsa_cmp_topk",
    )(nqn, k_cmp, v_cmp_t, jnp.asarray(ov, BF16), strips)


def _nsa_combine_kernel(oc_ref, os_ref, ow_ref, g_ref, o_ref):
    gate = jax.nn.sigmoid(g_ref[...].astype(F32))
    for h in range(NSA_HEADS):
        sl = slice(h * LANE, (h + 1) * LANE)
        o = (gate[:, 3 * h:3 * h + 1] * oc_ref[:, sl].astype(F32)
             + gate[:, 3 * h + 1:3 * h + 2] * os_ref[:, sl].astype(F32)
             + gate[:, 3 * h + 2:3 * h + 3] * ow_ref[:, sl].astype(F32))
        o_ref[:, sl] = o.astype(o_ref.dtype)


def nsa_combine(o_c, o_s, o_w, proj, tm=512):
    s = o_c.shape[0]
    tm = min(tm, s)
    w = NSA_HEADS * LANE
    spec = pl.BlockSpec((tm, w), lambda i: (i, 0))
    return pl.pallas_call(
        _nsa_combine_kernel,
        grid=(s // tm,),
        in_specs=[spec, spec, spec, pl.BlockSpec((tm, LANE), lambda i: (i, C_NG // LANE))],
        out_specs=spec,
        out_shape=jax.ShapeDtypeStruct((s, w), BF16),
        compiler_params=_cp(("parallel",)),
        name="nsa_combine",
    )(o_c, o_s, o_w, proj)


def _xattn_kernel(q_ref, kv_ref, gq_ref, gk_ref, o_ref, *, scale):
    nk = XATTN_HEADS * LANE
    for h in range(XATTN_HEADS):
        sl = slice(h * LANE, (h + 1) * LANE)
        qn = (_group_norm(q_ref[:, sl].astype(F32), LANE) * (gq_ref[...] * scale)).astype(BF16)
        kn = (_group_norm(kv_ref[:, sl].astype(F32), LANE) * gk_ref[...]).astype(BF16)
        s = lax.dot_general(qn, kn, (((1,), (1,)), ((), ())), preferred_element_type=F32)
        m = jnp.max(s, axis=-1, keepdims=True)
        e = jnp.exp(s - m)
        p = e / jnp.sum(e, axis=-1, keepdims=True)
        v = kv_ref[:, nk + h * LANE:nk + (h + 1) * LANE]
        o_ref[:, sl] = jnp.dot(p.astype(v.dtype), v, preferred_element_type=F32).astype(o_ref.dtype)


def xattn_core(qx, kv, qk_norm, tq=512):
    s, w = qx.shape
    tq = min(tq, s)
    m = kv.shape[0]
    return pl.pallas_call(
        functools.partial(_xattn_kernel, scale=HEAD_DIM ** -0.5),
        grid=(s // tq,),
        in_specs=[pl.BlockSpec((tq, w), lambda i: (i, 0)),
                  pl.BlockSpec((m, 2 * w), lambda i: (0, 0)),
                  pl.BlockSpec((1, LANE), lambda i: (0, 0)),
                  pl.BlockSpec((1, LANE), lambda i: (0, 0))],
        out_specs=pl.BlockSpec((tq, w), lambda i: (i, 0)),
        out_shape=jax.ShapeDtypeStruct((s, w), BF16),
        compiler_params=_cp(("parallel",)),
        name="xattn_core",
    )(qx, kv, qk_norm[0].reshape(1, LANE).astype(F32), qk_norm[1].reshape(1, LANE).astype(F32))


def _pack_w_in(w):
    d = w.shape[0]
    z = lambda n: jnp.zeros((d, n), w.dtype)
    parts = [w[:, 0:896], w[:, 1408:1472], z(64), w[:, 896:1408], w[:, 1472:7488],
             w[:, 7488:7512], z(LANE - 3 * NSA_HEADS), w[:, 7512:10200], z(NP_COLS - 10368)]
    return jnp.concatenate(parts, axis=1).astype(BF16)


def _mla_mixer(proj, rope, l, p):
    s = proj.shape[0]
    qm = mla_q_proj(proj, rope, p["mla_q_norm"][l], p["mla_w_uq"][l], p["mla_nope_norm"][l, 0], p["mla_rope_norm"][l, 0])
    km, vm = mla_kv_proj(proj, rope, p["mla_kv_norm"][l], p["mla_w_ukv"][l], p["mla_nope_norm"][l, 1],
                         p["mla_rope_norm"][l, 1])
    ident = lambda n: n
    return attention(qm, km, vm, n_streams=MLA_HEADS, seq=s, out_cols=MLA_HEADS * MLA_V, qcol=ident, kcol=ident,
                     vcol=ident, ocol=ident, tq=ATT_ROWS, tk=ATT_TK, tkb=ATT_TKB, dq=2 * LANE, dk=2 * LANE,
                     name="mla_attention")


def bias_tables(rel_bias, s):
    tabs = {"dil": []}
    for g, (_, dil) in enumerate(DIL_PATTERNS):
        tq = min(DIL_TQ, s // dil)
        tabs["dil"].append(bias_strips(rel_bias[:, BIAS_DIL + g * DIL_HP:BIAS_DIL + (g + 1) * DIL_HP], tq, tq, tq, dil))
    far_zero = lambda t: t - t[REL_BUCKETS - 1:REL_BUCKETS]
    nsa = rel_bias[:, BIAS_NSA:BIAS_NSA + NSA_HEADS].astype(F32)
    tk, tkb = min(ATT_TK, s), min(ATT_TKB, s)
    tabs["nsa"] = bias_strips(LOG2E * far_zero(nsa), ATT_ROWS // NSA_GROUP, tk, tkb, 1)
    tabs["nsa_win"] = bias_strips(LOG2E * far_zero(nsa), ATT_ROWS // NSA_GROUP, min(WIN_TK, s), min(WIN_TK, s), 1)
    tabs["nsa_cmp"] = cmp_bias_strips(LOG2E * nsa, s)
    diff = rel_bias[:, BIAS_DIFF:BIAS_DIFF + DIFF_HEADS].astype(F32)
    tabs["diff"] = bias_strips(LOG2E * far_zero(diff), min(ATT_ROWS // 2, s), tk, tkb, 1)
    return tabs


def _dilated_mixer(proj, l, p, tabs):
    s = proj.shape[0]
    w = DIL_HP * LANE
    dqn = head_rmsnorm(proj, C_DQ, DIL_HEADS, p["dil_qk_norm"][l, 0], LANE, HEAD_DIM ** -0.5)
    dkn = head_rmsnorm(proj, C_DK, DIL_HEADS, p["dil_qk_norm"][l, 1], LANE, 1.0)
    outs, lses = [], []
    ident = lambda n: n
    for g, (window, dil) in enumerate(DIL_PATTERNS):
        n_sub = s // dil
        tq = min(DIL_TQ, n_sub)
        if dil == 1:
            qkv = (dqn, dkn, proj)
            qcol = lambda r, g=g: g
            vcol = lambda r, g=g: C_DV // w + g
        else:
            fold = lambda t, c0: t[:, c0 + g * w:c0 + (g + 1) * w].reshape(n_sub, dil * w)
            qkv = (fold(dqn, 0), fold(dkn, 0), fold(proj, C_DV))
            qcol = vcol = ident
        o, lse = banded_attention(*qkv, tabs["dil"][g], streams=dil, seq=n_sub, tq=tq, span=window // dil,
                                  qcol=qcol, vcol=vcol, name=f"dil_attention_{g}")
        outs.append(o.reshape(s, w))
        lses.append(lse.reshape(s, w))
    return dil_combine(outs, lses)


def _nsa_mixer(proj, l, p, tabs):
    s = proj.shape[0]
    gains = p["nsa_qk_norm"][l]
    nqn = head_rmsnorm(proj, C_NQ, NSA_HEADS, gains[0], LANE, LOG2E * HEAD_DIM ** -0.5)
    kwn = head_rmsnorm(proj, C_NKW, NSA_KV_HEADS, gains[3], LANE, 1.0)
    ksel = nsa_sel_keys(proj, gains[2])
    nc = s // NSA_CMP_STRIDE

    def chunks(c0):
        t = proj[:, c0:c0 + NSA_KV_HEADS * LANE].reshape(nc, NSA_CMP_STRIDE, NSA_KV_HEADS, LANE)
        return jnp.transpose(t, (2, 0, 1, 3)).reshape(NSA_KV_HEADS, nc, NSA_CMP_STRIDE * LANE)

    k_cmp = nsa_compress(chunks(C_NKC), p["nsa_cmp_pe"][l, 0], p["nsa_cmp_w1"][l, 0], p["nsa_cmp_w2"][l, 0],
                         gains[1], norm=True, transpose=False)
    v_cmp_t = nsa_compress(chunks(C_NVC), p["nsa_cmp_pe"][l, 1], p["nsa_cmp_w1"][l, 1], p["nsa_cmp_w2"][l, 1],
                           gains[1], norm=False, transpose=True)
    o_c, sel = nsa_cmp_topk(nqn, k_cmp, v_cmp_t, tabs["nsa_cmp"], s)
    ident = lambda n: n
    common = dict(n_streams=NSA_KV_HEADS, seq=s, out_cols=NSA_HEADS * LANE, qcol=ident, ocol=ident,
                  tq=ATT_ROWS // NSA_GROUP, tk=ATT_TK, tkb=ATT_TKB, G=NSA_GROUP, dq=LANE, bias=tabs["nsa"], bcol=ident,
                  bias_far_zero=True)
    o_s = attention(nqn, ksel, proj, kcol=ident, vcol=lambda n: C_NVS // LANE + n, mode="nsa_sel", dk=2 * LANE,
                    sel=sel, name="nsa_sel_attention", **common)
    common.update(tk=WIN_TK, tkb=WIN_TK, bias=tabs["nsa_win"])
    o_w = attention(nqn, kwn, proj, kcol=ident, vcol=lambda n: C_NVW // LANE + n, dk=LANE, span=NSA_WINDOW - 1,
                    name="nsa_win_attention", **common)
    return nsa_combine(o_c, o_s, o_w, proj)


def _diff_mixer(proj, l, p, tabs):
    s = proj.shape[0]
    fqn = head_rmsnorm(proj, C_FQ, DIFF_HEADS, p["diff_qk_norm"][l, 0], DIFF_QK, LOG2E * DIFF_QK ** -0.5)
    fkn = head_rmsnorm(proj, C_FK, DIFF_HEADS, p["diff_qk_norm"][l, 1], DIFF_QK, 1.0)
    lam_init = 0.8 - 0.6 * math.exp(-0.3 * l)
    lv = p["diff_lambda"][l].astype(F32)
    lam = jnp.exp(jnp.sum(lv[0] * lv[1])) - jnp.exp(jnp.sum(lv[2] * lv[3])) + lam_init
    ident = lambda n: n
    return attention(fqn, fkn, proj, n_streams=DIFF_HEADS, seq=s, out_cols=DIFF_HEADS * DIFF_V, qcol=ident,
                     kcol=ident, vcol=lambda n: C_FV // LANE + n, ocol=ident, mode="diff", tq=ATT_ROWS // 2,
                     tk=ATT_TK, tkb=ATT_TKB, dq=LANE, dk=LANE, bias=tabs["diff"], bcol=ident, bias_far_zero=True,
                     lam=jnp.full((1, LANE), lam, F32),
                     gout=p["diff_out_norm"][l].reshape(1, LANE).astype(F32), out_scale=1.0 - lam_init,
                     name="diff_attention")


def kernel(x, mem, positions, rel_bias, norm_mix, w_in, mla_q_norm, mla_kv_norm, mla_w_uq, mla_w_ukv, mla_nope_norm, mla_rope_norm, dil_qk_norm, nsa_qk_norm, nsa_cmp_pe, nsa_cmp_w1, nsa_cmp_w2, diff_qk_norm, diff_lambda, diff_out_norm, w_out, norm_xattn, norm_mem, xattn_wq, xattn_wkv, xattn_qk_norm, xattn_wo, norm_ffn, ffn_w_gate, ffn_w_up, ffn_conv_w, ffn_conv_b, ffn_w_down):
    p = dict(mla_q_norm=mla_q_norm, mla_kv_norm=mla_kv_norm, mla_w_uq=mla_w_uq, mla_w_ukv=mla_w_ukv,
             mla_nope_norm=mla_nope_norm, mla_rope_norm=mla_rope_norm, dil_qk_norm=dil_qk_norm,
             nsa_qk_norm=nsa_qk_norm, nsa_cmp_pe=nsa_cmp_pe, nsa_cmp_w1=nsa_cmp_w1, nsa_cmp_w2=nsa_cmp_w2,
             diff_qk_norm=diff_qk_norm, diff_lambda=diff_lambda, diff_out_norm=diff_out_norm)
    b, s, d = x.shape
    assert b == 1
    xs = x.reshape(s, d).astype(F32)
    mems = mem.reshape(mem.shape[1], d).astype(F32)
    pos = positions.reshape(s, 1).astype(jnp.int32)
    tabs = bias_tables(rel_bias, s)
    rope = rope_tables(pos)
    for l in range(DEPTH):
        h = rmsnorm_rows(xs, norm_mix[l])
        proj = matmul(h, _pack_w_in(w_in[l]), tm=512, tn=1536, out_dtype=BF16, name="in_proj")
        mix = [_mla_mixer(proj, rope, l, p),_dilated_mixer(proj, l, p, tabs), _nsa_mixer(proj, l, p, tabs),
               _diff_mixer(proj, l, p, tabs)]
        xs = matmul_cat(mix, w_out[l].astype(BF16), xs, tm=512, tn=1024, name="out_proj")
        hx = rmsnorm_rows(xs, norm_xattn[l])
        qx = matmul(hx, xattn_wq[l].astype(BF16), tm=1024, tn=512, out_dtype=BF16, name="xattn_q")
        kvm = matmul(rmsnorm_rows(mems, norm_mem[l]), xattn_wkv[l].astype(BF16), tm=256, tn=1024, out_dtype=BF16,
                     name="xattn_kv")
        ox = xattn_core(qx, kvm, xattn_qk_norm[l])
        xs = matmul(ox, xattn_wo[l].astype(BF16), tm=1024, tn=1024, out_dtype=F32, residual=xs, name="xattn_o")
        hf = rmsnorm_rows(xs, norm_ffn[l])
        act = ffn_gate_up(hf, ffn_w_gate, ffn_w_up, l, ffn_conv_w[l], ffn_conv_b[l])
        xs = matmul(act, ffn_w_down[l].astype(BF16), tm=512, tn=512, out_dtype=F32, residual=xs, name="ffn_down")
    return xs.reshape(b, s, d)
```
